```python
import math
import jax, jax.numpy as jnp
from jax import lax
import numpy as np

D_MODEL = 2048
BATCH = 1
SEQ = 8192
DEPTH = 2
DEC_BATCH = 8
DEC_SEQ = 16
PAST_LEN = 4096

CHUNK = 64
EPS = 1e-6
POOL_WIDTH = D_MODEL // 2
POOL_WINDOWS = (2, 4, 8, 16)
POOL_GROUP = POOL_WIDTH // len(POOL_WINDOWS)
POOL_HIST = max(POOL_WINDOWS) - 1
FOX_HEADS = 8
HEAD_DIM = 128
FOX_WIDTH = FOX_HEADS * HEAD_DIM
Q_BLOCK = 128
AB_IN = POOL_WIDTH + 3 * FOX_WIDTH + FOX_HEADS
SG_CHUNK = 128
SG_WIDTH = D_MODEL
SG_GROUPS = 8
SG_GROUP = SG_WIDTH // SG_GROUPS
PEER_HEADS = 8
PEER_KEYS = 128
PEER_EXPERTS = PEER_KEYS * PEER_KEYS
PEER_QDIM = 256
PEER_HALF = PEER_QDIM // 2
PEER_TOPK = 16
PEER_TBLOCK = 128
N_AB = (DEPTH + 1) // 2
N_C = DEPTH // 2

kernel_name = 'hybrid_pool_fox_sgmlp_peer_step'


def rms_norm(x, g):
    xf = x.astype(jnp.float32)
    return xf * lax.rsqrt(jnp.mean(xf * xf, axis=-1, keepdims=True) + EPS) * g


def ada_params(c, w_ada, b_ada):
    m = jax.nn.silu(c.astype(jnp.float32)) @ w_ada + b_ada
    return jnp.split(m[:, None, :], 6, axis=-1)


def modulate(x, g, shift, scale):
    return rms_norm(x, g) * (1.0 + scale) + shift


def ab_project(h, w_in, b_f, q_gain, k_gain):
    B, L, _ = h.shape
    z = h @ w_in
    p = z[..., :POOL_WIDTH]
    o = POOL_WIDTH
    q = z[..., o:o + FOX_WIDTH].reshape(B, L, FOX_HEADS, HEAD_DIM)
    k = z[..., o + FOX_WIDTH:o + 2 * FOX_WIDTH].reshape(B, L, FOX_HEADS, HEAD_DIM)
    v = z[..., o + 2 * FOX_WIDTH:o + 3 * FOX_WIDTH].reshape(B, L, FOX_HEADS, HEAD_DIM)
    f = z[..., o + 3 * FOX_WIDTH:]
    q = rms_norm(q, q_gain)
    k = rms_norm(k, k_gain)
    logf = jax.nn.log_sigmoid(f.astype(jnp.float32) + b_f)
    return p, q, k, v, logf


def multiscale_pool(p, p0, w_pool, pool_scale):
    L = p.shape[1]
    pf = p.astype(jnp.float32)
    cs = jnp.concatenate([jnp.zeros_like(pf[:, :1]), lax.cumsum(pf, axis=1)], axis=1)
    idx = jnp.arange(L)
    outs = []
    for gi, w in enumerate(POOL_WINDOWS):
        sl = slice(gi * POOL_GROUP, (gi + 1) * POOL_GROUP)
        csg = cs[..., sl]
        lo = jnp.maximum(idx + 1 - w, 0)
        wsum = csg[:, 1:] - jnp.take(csg, lo, axis=1)
        cnt = jnp.minimum(w, idx + 1 + p0).astype(jnp.float32)
        d = wsum / cnt[None, :, None] - pf[..., sl]
        outs.append(d @ w_pool[gi])
    return jnp.concatenate(outs, axis=-1) * pool_scale


def fox_block(q, k, v, fq, fk, qpos, kpos):
    s = jnp.einsum('bqhd,bkhd->bhqk', q, k) * (HEAD_DIM ** -0.5)
    bias = jnp.transpose(fq, (0, 2, 1))[..., :, None] - jnp.transpose(fk, (0, 2, 1))[..., None, :]
    mask = kpos[None, :] <= qpos[:, None]
    s = jnp.where(mask, s.astype(jnp.float32) + bias, -1e30)
    pr = jax.nn.softmax(s, axis=-1)
    return jnp.einsum('bhqk,bkhd->bqhd', pr, v)


def fox_prompt(q, k, v, f_cum):
    B, S, H, dh = q.shape
    kpos = jnp.arange(S)

    def blk(i):
        st = i * Q_BLOCK
        qb = lax.dynamic_slice_in_dim(q, st, Q_BLOCK, axis=1)
        fb = lax.dynamic_slice_in_dim(f_cum, st, Q_BLOCK, axis=1)
        return fox_block(qb, k, v, fb, f_cum, st + jnp.arange(Q_BLOCK), kpos)

    o = lax.map(blk, jnp.arange(S // Q_BLOCK))
    return jnp.moveaxis(o, 0, 1).reshape(B, S, H, dh)


def merge_ab(a_out, b_out, w_out):
    B, L = a_out.shape[:2]
    return jnp.concatenate([a_out, b_out.reshape(B, L, FOX_WIDTH)], axis=-1) @ w_out


def sg_mixer(h, w_uv, ln_g, ln_b, w_s, b_s, w_out):
    B, L, _ = h.shape
    z = jax.nn.gelu(h @ w_uv)
    u, v = z[..., :SG_WIDTH], z[..., SG_WIDTH:]
    vf = v.astype(jnp.float32)
    mu = jnp.mean(vf, axis=-1, keepdims=True)
    var = jnp.mean(jnp.square(vf - mu), axis=-1, keepdims=True)
    vn = (vf - mu) * lax.rsqrt(var + EPS) * ln_g + ln_b
    n = min(L, SG_CHUNK)
    nc = L // n
    tri = jnp.tril(jnp.ones((n, n), dtype=bool))
    ws = jnp.where(tri, w_s[:, :n, :n], 0.0)
    vr = vn.reshape(B, nc, n, SG_GROUPS, SG_GROUP)
    sv = jnp.einsum('gts,bnsgc->bntgc', ws, vr) + jnp.transpose(b_s[:, :n])[None, None, :, :, None]
    y = u * sv.reshape(B, L, SG_WIDTH)
    return y @ w_out, vn


def peer(h, w_q, sub_keys, u_tab, v_tab):
    B, L, D = h.shape
    t = h.reshape(B * L, D)
    T = t.shape[0]
    nb = -(-T // PEER_TBLOCK)
    tp = jnp.pad(t, ((0, nb * PEER_TBLOCK - T), (0, 0)))

    def blk(xb):
        tb = xb.shape[0]
        q = (xb @ w_q).reshape(tb, PEER_HEADS, 2, PEER_HALF)
        s1 = jnp.einsum('thd,hkd->thk', q[:, :, 0], sub_keys[:, 0])
        s2 = jnp.einsum('thd,hkd->thk', q[:, :, 1], sub_keys[:, 1])
        v1, i1 = lax.top_k(s1, PEER_TOPK)
        v2, i2 = lax.top_k(s2, PEER_TOPK)
        comb = (v1[..., :, None] + v2[..., None, :]).reshape(tb, PEER_HEADS, PEER_TOPK * PEER_TOPK)
        sc, ci = lax.top_k(comb, PEER_TOPK)
        e = (jnp.take_along_axis(i1, ci // PEER_TOPK, axis=-1) * PEER_KEYS
             + jnp.take_along_axis(i2, ci % PEER_TOPK, axis=-1))
        g = jax.nn.softmax(sc.astype(jnp.float32), axis=-1)
        ue = u_tab[e]
        ve = v_tab[e]
        a = jax.nn.gelu(jnp.einsum('td,thkd->thk', xb, ue))
        return jnp.einsum('thk,thkd->td', g * a, ve)

    out = lax.map(blk, tp.reshape(nb, PEER_TBLOCK, D)).reshape(nb * PEER_TBLOCK, D)[:T]
    return out.reshape(B, L, D)


def setup_inputs(seed: int = 0) -> dict:
    key = jax.random.key(seed)
    ks = iter(jax.random.split(key, 40))
    D = D_MODEL

    def nrm(shape, s):
        return jax.random.normal(next(ks), shape, jnp.float32) * s

    return {
        'x_prompt': nrm((BATCH, SEQ, D), 1.0),
        'x_sample': nrm((DEC_BATCH, DEC_SEQ, D), 1.0),
        'c_prompt': nrm((BATCH, D), 1.0),
        'c_sample': nrm((DEC_BATCH, D), 1.0),
        'cache_pool': nrm((N_AB, DEC_BATCH, POOL_HIST, POOL_WIDTH), 1.0),
        'cache_k': nrm((N_AB, DEC_BATCH, PAST_LEN, FOX_HEADS, HEAD_DIM), 1.0),
        'cache_v': nrm((N_AB, DEC_BATCH, PAST_LEN, FOX_HEADS, HEAD_DIM), 1.0),
        'cache_logf': jax.nn.log_sigmoid(2.0 + nrm((N_AB, DEC_BATCH, PAST_LEN, FOX_HEADS), 1.0)),
        'w_ada': nrm((DEPTH, D, 6 * D), 0.5 * D ** -0.5),
        'b_ada': nrm((DEPTH, 6 * D), 0.02),
        'norm1': 1.0 + nrm((DEPTH, D), 0.02),
        'norm2': 1.0 + nrm((DEPTH, D), 0.02),
        'w_in_ab': nrm((N_AB, D, AB_IN), D ** -0.5),
        'b_f': 2.0 + nrm((N_AB, FOX_HEADS), 0.1),
        'w_pool': nrm((N_AB, len(POOL_WINDOWS), POOL_GROUP, POOL_GROUP), POOL_GROUP ** -0.5),
        'pool_scale': 1.0 + nrm((N_AB, POOL_WIDTH), 0.02),
        'q_gain': 1.0 + nrm((N_AB, FOX_HEADS, HEAD_DIM), 0.02),
        'k_gain': 1.0 + nrm((N_AB, FOX_HEADS, HEAD_DIM), 0.02),
        'w_out_ab': nrm((N_AB, D, D), D ** -0.5),
        'w_uv': nrm((N_C, D, 2 * SG_WIDTH), D ** -0.5),
        'sg_ln_g': 1.0 + nrm((N_C, SG_WIDTH), 0.02),
        'sg_ln_b': nrm((N_C, SG_WIDTH), 0.02),
        'w_s': nrm((N_C, SG_GROUPS, SG_CHUNK, SG_CHUNK), SG_CHUNK ** -0.5),
        'b_s': 1.0 + nrm((N_C, SG_GROUPS, SG_CHUNK), 0.1),
        'w_out_sg': nrm((N_C, SG_WIDTH, D), SG_WIDTH ** -0.5),
        'peer_wq': nrm((DEPTH, D, PEER_HEADS * PEER_QDIM), D ** -0.5),
        'peer_keys': nrm((DEPTH, PEER_HEADS, 2, PEER_KEYS, PEER_HALF), PEER_HALF ** -0.5),
        'peer_u': nrm((DEPTH, PEER_EXPERTS, D), D ** -0.5),
        'peer_v': nrm((DEPTH, PEER_EXPERTS, D), 0.5),
    }


def reference(x_prompt, x_sample, c_prompt, c_sample, cache_pool, cache_k, cache_v, cache_logf,
              w_ada, b_ada, norm1, norm2, w_in_ab, b_f, w_pool, pool_scale, q_gain, k_gain, w_out_ab,
              w_uv, sg_ln_g, sg_ln_b, w_s, b_s, w_out_sg, peer_wq, peer_keys, peer_u, peer_v):
    xp, xs = x_prompt, x_sample
    pool_p, pool_s, kp_l, ks_l, vp_l, vs_l, lfp_l, lfs_l, sgv_l = [], [], [], [], [], [], [], [], []
    for layer in range(DEPTH):
        mp = ada_params(c_prompt, w_ada[layer], b_ada[layer])
        ms = ada_params(c_sample, w_ada[layer], b_ada[layer])
        hp = modulate(xp, norm1[layer], mp[0], mp[1])
        hs = modulate(xs, norm1[layer], ms[0], ms[1])
        j = layer // 2
        if layer % 2 == 0:
            p, q, k, v, lf = ab_project(hp, w_in_ab[j], b_f[j], q_gain[j], k_gain[j])
            a_out = multiscale_pool(p, 0, w_pool[j], pool_scale[j])
            b_out = fox_prompt(q, k, v, lax.cumsum(lf, axis=1))
            o_p = merge_ab(a_out, b_out, w_out_ab[j])
            pool_p.append(p[:, -POOL_HIST:])
            kp_l.append(k)
            vp_l.append(v)
            lfp_l.append(lf)
            p, q, k, v, lf = ab_project(hs, w_in_ab[j], b_f[j], q_gain[j], k_gain[j])
            L = p.shape[1]
            buf = jnp.concatenate([cache_pool[j], p], axis=1)
            a_out = multiscale_pool(buf, PAST_LEN - POOL_HIST, w_pool[j], pool_scale[j])[:, POOL_HIST:]
            k_all = jnp.concatenate([cache_k[j], k], axis=1)
            v_all = jnp.concatenate([cache_v[j], v], axis=1)
            f_all = lax.cumsum(jnp.concatenate([cache_logf[j], lf], axis=1), axis=1)
            b_out = fox_block(q, k_all, v_all, f_all[:, PAST_LEN:], f_all,
                              PAST_LEN + jnp.arange(L), jnp.arange(PAST_LEN + L))
            o_s = merge_ab(a_out, b_out, w_out_ab[j])
            pool_s.append(buf[:, -POOL_HIST:])
            ks_l.append(k)
            vs_l.append(v)
            lfs_l.append(lf)
        else:
            o_p, _ = sg_mixer(hp, w_uv[j], sg_ln_g[j], sg_ln_b[j], w_s[j], b_s[j], w_out_sg[j])
            o_s, vn = sg_mixer(hs, w_uv[j], sg_ln_g[j], sg_ln_b[j], w_s[j], b_s[j], w_out_sg[j])
            sgv_l.append(vn)
        xp = xp + mp[2] * o_p
        xs = xs + ms[2] * o_s
        hp = modulate(xp, norm2[layer], mp[3], mp[4])
        hs = modulate(xs, norm2[layer], ms[3], ms[4])
        xp = xp + mp[5] * peer(hp, peer_wq[layer], peer_keys[layer], peer_u[layer], peer_v[layer])
        xs = xs + ms[5] * peer(hs, peer_wq[layer], peer_keys[layer], peer_u[layer], peer_v[layer])
    pool_prompt = jnp.stack(pool_p)
    pool_sample = jnp.stack(pool_s)
    k_prompt = jnp.stack(kp_l)
    k_sample = jnp.stack(ks_l)
    v_prompt = jnp.stack(vp_l)
    v_sample = jnp.stack(vs_l)
    logf_prompt = jnp.stack(lfp_l)
    logf_sample = jnp.stack(lfs_l)
    sgv_sample = jnp.stack(sgv_l)
    return (xp, xs, pool_prompt, pool_sample, k_prompt, k_sample, v_prompt, v_sample,
            logf_prompt, logf_sample, sgv_sample)
```

```python
import functools

import jax
import jax.numpy as jnp
from jax import lax
from jax.experimental import pallas as pl
from jax.experimental.pallas import tpu as pltpu

F32 = jnp.float32
BF16 = jnp.bfloat16
EPS = 1e-6

LANES = 128
POOL_WINDOWS = (2, 4, 8, 16)
POOL_HALO = 128
FOX_HEADS = 8
HEAD_DIM = 128
SG_GROUPS = 8
SG_CHUNK = 128
PEER_HEADS = 8
PEER_KEYS = 128
PEER_TOPK = 16
NEG_BIG = -1e30
RANK_NONE = 64.0
VMEM_LIMIT = 56 * 1024 * 1024

NT_DIMS = (((1,), (1,)), ((), ()))


def _cp(*sem):
    return pltpu.CompilerParams(dimension_semantics=sem, vmem_limit_bytes=VMEM_LIMIT)


def _dot(a, b):
    return jnp.dot(a, b, preferred_element_type=F32)


def _dot_nt(a, b):
    return lax.dot_general(a, b, NT_DIMS, preferred_element_type=F32)


def _modulate(x, g, shift, scale):
    ms = jnp.mean(x * x, axis=-1, keepdims=True)
    return x * lax.rsqrt(ms + EPS) * g * (1.0 + scale) + shift


def _gelu(x):
    cdf = 0.5 * (1.0 + jnp.tanh(0.7978845608028654 * (x + 0.044715 * (x * x * x))))
    return x * cdf


def _log_sigmoid(x):
    return -(jnp.maximum(-x, 0.0) + jnp.log(1.0 + jnp.exp(-jnp.abs(x))))


def _lane_cumsum(x):
    n = x.shape[-1]
    lane = lax.broadcasted_iota(jnp.int32, x.shape, x.ndim - 1)
    s = 1
    while s < n:
        x = x + jnp.where(lane >= s, pltpu.roll(x, s, x.ndim - 1), 0.0)
        s *= 2
    return x


def _ada_kernel(c_ref, w_ref, b_ref, o_ref):
    c = c_ref[...]
    a = (c * (1.0 / (1.0 + jnp.exp(-c)))).astype(BF16)
    o_ref[0] = _dot(a, w_ref[0].astype(BF16)) + b_ref[0]


def _ada(c_rows, w_ada, b_ada):
    nl, d, n = w_ada.shape
    r = c_rows.shape[0]
    tn = 1024
    return pl.pallas_call(
        _ada_kernel,
        grid=(nl, n // tn),
        in_specs=[pl.BlockSpec((r, d), lambda l, j: (0, 0)),
                  pl.BlockSpec((1, d, tn), lambda l, j: (l, 0, j)),
                  pl.BlockSpec((1, 1, tn), lambda l, j: (l, 0, j))],
        out_specs=pl.BlockSpec((1, r, tn), lambda l, j: (l, 0, j)),
        out_shape=jax.ShapeDtypeStruct((nl, r, n), F32),
        compiler_params=_cp("parallel", "parallel"),
        name="ada",
    )(c_rows, w_ada, b_ada.reshape(nl, 1, n))


def _mod_spec(per_row, tm, d, nj):
    if nj:
        return (pl.BlockSpec((tm, d), lambda i, j: (i, 0)) if per_row
                else pl.BlockSpec((1, d), lambda i, j: (0, 0)))
    return (pl.BlockSpec((tm, d), lambda i: (i, 0)) if per_row
            else pl.BlockSpec((1, d), lambda i: (0, 0)))


def _head_rms(z, gain):
    outs = []
    for h in range(FOX_HEADS):
        zh = z[:, h * HEAD_DIM:(h + 1) * HEAD_DIM]
        ms = jnp.mean(zh * zh, axis=-1, keepdims=True)
        outs.append(zh * lax.rsqrt(ms + EPS))
    return jnp.concatenate(outs, axis=-1) * gain


def _proj_kernel(x_ref, g_ref, sh_ref, sc_ref, w_ref, wf_ref, bf_ref, qg_ref, kg_ref,
                 p_ref, q_ref, k_ref, kb_ref, v_ref, vb_ref, lf_ref, fc_ref,
                 h_scr, carry_scr, *, do_cumsum):
    i = pl.program_id(0)
    j = pl.program_id(1)

    @pl.when(j == 0)
    def _():
        hb = _modulate(x_ref[...], g_ref[...], sh_ref[...], sc_ref[...]).astype(BF16)
        h_scr[...] = hb
        lf = _log_sigmoid(_dot_nt(wf_ref[...], hb) + bf_ref[...])
        lf_ref[...] = lf
        if do_cumsum:
            @pl.when(i == 0)
            def _():
                carry_scr[...] = jnp.zeros_like(carry_scr)
            fc = _lane_cumsum(lf) + carry_scr[:, 0:1]
            fc_ref[...] = fc
            carry_scr[...] = jnp.broadcast_to(fc[:, -1:], carry_scr.shape)
        else:
            fc_ref[...] = lf

    z = _dot(h_scr[...], w_ref[...])

    @pl.when(j == 0)
    def _():
        p_ref[...] = z

    @pl.when(j == 1)
    def _():
        q_ref[...] = (_head_rms(z, qg_ref[...]) * (HEAD_DIM ** -0.5)).astype(BF16)

    @pl.when(j == 2)
    def _():
        kn = _head_rms(z, kg_ref[...])
        k_ref[...] = kn
        kb_ref[...] = kn.astype(BF16)

    @pl.when(j == 3)
    def _():
        v_ref[...] = z
        vb_ref[...] = z.astype(BF16)


def _proj(x, g, shift, scale, w_main, wf_t, b_f, q_gain, k_gain, *, tm, per_row, do_cumsum):
    t, d = x.shape
    w4 = w_main.shape[1] // 4
    nh = wf_t.shape[0]
    row = lambda i, j: (i, 0)
    const = lambda i, j: (0, 0)
    tok_f32 = jax.ShapeDtypeStruct((t, w4), F32)
    tok_b16 = jax.ShapeDtypeStruct((t, w4), BF16)
    head_t = jax.ShapeDtypeStruct((nh, t), F32)
    return pl.pallas_call(
        functools.partial(_proj_kernel, do_cumsum=do_cumsum),
        grid=(t // tm, 4),
        in_specs=[pl.BlockSpec((tm, d), row),
                  pl.BlockSpec((1, d), const),
                  _mod_spec(per_row, tm, d, True),
                  _mod_spec(per_row, tm, d, True),
                  pl.BlockSpec((d, w4), lambda i, j: (0, j)),
                  pl.BlockSpec((nh, d), const),
                  pl.BlockSpec((nh, 1), const),
                  pl.BlockSpec((1, w4), const),
                  pl.BlockSpec((1, w4), const)],
        out_specs=[pl.BlockSpec((tm, w4), row)] * 6
                  + [pl.BlockSpec((nh, tm), lambda i, j: (0, i))] * 2,
        out_shape=[tok_f32, tok_b16, tok_f32, tok_b16, tok_f32, tok_b16, head_t, head_t],
        scratch_shapes=[pltpu.VMEM((tm, d), BF16), pltpu.VMEM((nh, LANES), F32)],
        compiler_params=_cp("arbitrary", "arbitrary"),
        name="proj",
    )(x, g, shift, scale, w_main, wf_t, b_f, q_gain, k_gain)


def _pool_kernel(ext_ref, tok_ref, w_ref, ps_ref, o_ref, *, prompt, pos0):
    i = pl.program_id(0)
    tok = tok_ref[...]
    tm = tok.shape[0]
    if prompt:
        halo = jnp.where(i == 0, 0.0, ext_ref[...])
        ext = jnp.concatenate([halo, tok], axis=0)
    else:
        ext = ext_ref[...]
    ke = ext.shape[0]
    hi = ext.astype(BF16)
    lo = (ext - hi.astype(F32)).astype(BF16)
    m_io = lax.broadcasted_iota(jnp.int32, (tm, ke), 0)
    c_io = lax.broadcasted_iota(jnp.int32, (tm, ke), 1)
    r_io = lax.broadcasted_iota(jnp.int32, (tm, 1), 0)
    if prompt:
        tgt = m_io + POOL_HALO
        pos = pos0 + i * tm + r_io
    else:
        tgt = m_io + ((m_io >> 4) << 4) + 16
        pos = pos0 + (r_io & 15)
    dlt = tgt - c_io
    gw = w_ref.shape[1]
    for gi, win in enumerate(POOL_WINDOWS):
        sl = slice(gi * gw, (gi + 1) * gw)
        band = jnp.where(dlt >= 0, jnp.where(dlt < win, 1.0, 0.0), 0.0).astype(BF16)
        wsum = _dot(band, hi[:, sl]) + _dot(band, lo[:, sl])
        cnt = jnp.minimum(win, pos + 1).astype(F32)
        dd = wsum / cnt - tok[:, sl]
        o = _dot(dd.astype(BF16), w_ref[gi]) * ps_ref[:, sl]
        o_ref[:, sl] = o.astype(o_ref.dtype)


def _pool_prompt(p, w_pool, pool_scale, *, tm):
    t, c = p.shape
    per = tm // POOL_HALO
    return pl.pallas_call(
        functools.partial(_pool_kernel, prompt=True, pos0=0),
        grid=(t // tm,),
        in_specs=[pl.BlockSpec((POOL_HALO, c), lambda i: (jnp.maximum(i * per - 1, 0), 0)),
                  pl.BlockSpec((tm, c), lambda i: (i, 0)),
                  pl.BlockSpec(w_pool.shape, lambda i: (0, 0, 0)),
                  pl.BlockSpec((1, c), lambda i: (0, 0))],
        out_specs=pl.BlockSpec((tm, c), lambda i: (i, 0)),
        out_shape=jax.ShapeDtypeStruct((t, c), BF16),
        compiler_params=_cp("parallel"),
        name="pool_prompt",
    )(p, p, w_pool, pool_scale)


def _pool_sample(buf, p, w_pool, pool_scale, *, pos0):
    t, c = p.shape
    return pl.pallas_call(
        functools.partial(_pool_kernel, prompt=False, pos0=pos0),
        grid=(1,),
        in_specs=[pl.BlockSpec(buf.shape, lambda i: (0, 0)),
                  pl.BlockSpec((t, c), lambda i: (0, 0)),
                  pl.BlockSpec(w_pool.shape, lambda i: (0, 0, 0)),
                  pl.BlockSpec((1, c), lambda i: (0, 0))],
        out_specs=pl.BlockSpec((t, c), lambda i: (0, 0)),
        out_shape=jax.ShapeDtypeStruct((t, c), BF16),
        compiler_params=_cp("arbitrary"),
        name="pool_sample",
    )(buf, p, w_pool, pool_scale)


def _softmax_step(carry, s, v):
    m, l, acc = carry
    m_new = jnp.maximum(m, jnp.max(s, axis=-1, keepdims=True))
    alpha = jnp.exp(m - m_new)
    p = jnp.exp(s - m_new)
    l = alpha * l + jnp.sum(p, axis=-1, keepdims=True)
    acc = alpha * acc + _dot(p.astype(BF16), v)
    return m_new, l, acc


def _attn_prompt_kernel(q_ref, k_ref, v_ref, fk_ref, o_ref, *, tq):
    i = pl.program_id(1)
    q = q_ref[...]

    def chunk(kj, carry, masked):
        off = pl.multiple_of(kj * tq, tq)
        s = _dot_nt(q, k_ref[pl.ds(off, tq), :]) - fk_ref[:, pl.ds(off, tq)]
        if masked:
            r_io = lax.broadcasted_iota(jnp.int32, (tq, tq), 0)
            c_io = lax.broadcasted_iota(jnp.int32, (tq, tq), 1)
            s = jnp.where(c_io <= r_io, s, NEG_BIG)
        return _softmax_step(carry, s, v_ref[pl.ds(off, tq), :])

    init = (jnp.full((tq, 1), NEG_BIG, F32), jnp.zeros((tq, 1), F32),
            jnp.zeros((tq, HEAD_DIM), F32))
    carry = lax.fori_loop(0, i, lambda kj, c: chunk(kj, c, False), init)
    _, l, acc = chunk(i, carry, True)
    o_ref[...] = (acc / l).astype(o_ref.dtype)


def _attn_prompt(qb, kb, vb, fcum, *, tq):
    t, w = qb.shape
    nh = w // HEAD_DIM
    return pl.pallas_call(
        functools.partial(_attn_prompt_kernel, tq=tq),
        grid=(nh, t // tq),
        in_specs=[pl.BlockSpec((tq, HEAD_DIM), lambda h, i: (i, h)),
                  pl.BlockSpec((t, HEAD_DIM), lambda h, i: (0, h)),
                  pl.BlockSpec((t, HEAD_DIM), lambda h, i: (0, h)),
                  pl.BlockSpec((None, 1, t), lambda h, i: (h, 0, 0))],
        out_specs=pl.BlockSpec((tq, HEAD_DIM), lambda h, i: (i, h)),
        out_shape=jax.ShapeDtypeStruct((t, w), BF16),
        compiler_params=_cp("parallel", "parallel"),
        name="attn_prompt",
    )(qb, kb, vb, fcum.reshape(nh, 1, t))


def _attn_sample_kernel(q_ref, ck_ref, cv_ref, clf_ref, kn_ref, vn_ref, lfn_ref, o_ref,
                        qbd_scr, m_scr, l_scr, acc_scr, fcar_scr):
    c = pl.program_id(1)
    nl = q_ref.shape[0]
    rows, width = qbd_scr.shape

    @pl.when(c == 0)
    def _():
        q = q_ref[...]
        r_io = lax.broadcasted_iota(jnp.int32, (rows, width), 0)
        c_io = lax.broadcasted_iota(jnp.int32, (rows, width), 1)
        qt = jnp.concatenate([q] * FOX_HEADS, axis=0)
        qbd_scr[...] = jnp.where((r_io // nl) == (c_io // HEAD_DIM), qt, jnp.zeros_like(qt))
        m_scr[...] = jnp.full_like(m_scr, NEG_BIG)
        l_scr[...] = jnp.zeros_like(l_scr)
        acc_scr[...] = jnp.zeros_like(acc_scr)
        fcar_scr[...] = jnp.zeros_like(fcar_scr)

    def head_rows(f):
        return jnp.concatenate(
            [jnp.broadcast_to(f[h:h + 1], (nl, f.shape[1])) for h in range(FOX_HEADS)], axis=0)

    qbd = qbd_scr[...]
    fc = _lane_cumsum(clf_ref[...]) + fcar_scr[:, 0:1]
    fcar_scr[...] = jnp.broadcast_to(fc[:, -1:], fcar_scr.shape)
    s = _dot_nt(qbd, ck_ref[...].astype(BF16)) - head_rows(fc)
    carry = _softmax_step((m_scr[...], l_scr[...], acc_scr[...]), s, cv_ref[...].astype(BF16))
    m_scr[...], l_scr[...], acc_scr[...] = carry

    @pl.when(c == pl.num_programs(1) - 1)
    def _():
        kpad = jnp.zeros((LANES - nl, width), BF16)
        kn = jnp.concatenate([kn_ref[...], kpad], axis=0)
        vn = jnp.concatenate([vn_ref[...], kpad], axis=0)
        fn = _lane_cumsum(lfn_ref[...]) + fcar_scr[:, 0:1]
        sn = _dot_nt(qbd, kn) - head_rows(fn)
        r_io = lax.broadcasted_iota(jnp.int32, sn.shape, 0)
        c_io = lax.broadcasted_iota(jnp.int32, sn.shape, 1)
        sn = jnp.where(c_io <= (r_io % nl), sn, NEG_BIG)
        _, l, acc = _softmax_step((m_scr[...], l_scr[...], acc_scr[...]), sn, vn)
        o = acc / l
        o_ref[...] = jnp.concatenate(
            [o[h * nl:(h + 1) * nl, h * HEAD_DIM:(h + 1) * HEAD_DIM] for h in range(FOX_HEADS)],
            axis=-1).astype(o_ref.dtype)


def _attn_sample(qb, kb, vb, lfn, cache_k, cache_v, cache_lf_t, *, tk):
    b, past, w = cache_k.shape
    nl = qb.shape[0] // b
    nh = w // HEAD_DIM
    rows = nh * nl
    return pl.pallas_call(
        _attn_sample_kernel,
        grid=(b, past // tk),
        in_specs=[pl.BlockSpec((nl, w), lambda s, c: (s, 0)),
                  pl.BlockSpec((None, tk, w), lambda s, c: (s, c, 0)),
                  pl.BlockSpec((None, tk, w), lambda s, c: (s, c, 0)),
                  pl.BlockSpec((None, nh, tk), lambda s, c: (s, 0, c)),
                  pl.BlockSpec((nl, w), lambda s, c: (s, 0)),
                  pl.BlockSpec((nl, w), lambda s, c: (s, 0)),
                  pl.BlockSpec((None, nh, LANES), lambda s, c: (s, 0, 0))],
        out_specs=pl.BlockSpec((nl, w), lambda s, c: (s, 0)),
        out_shape=jax.ShapeDtypeStruct(qb.shape, BF16),
        scratch_shapes=[pltpu.VMEM((rows, w), BF16), pltpu.VMEM((rows, 1), F32),
                        pltpu.VMEM((rows, 1), F32), pltpu.VMEM((rows, w), F32),
                        pltpu.VMEM((nh, LANES), F32)],
        compiler_params=_cp("parallel", "arbitrary"),
        name="attn_sample",
    )(qb, cache_k, cache_v, cache_lf_t, kb, vb, lfn)


def _residual_mod(x, gate, o, g2, sh2, sc2, x_ref, h_ref):
    xn = x + gate * o
    x_ref[...] = xn
    h_ref[...] = _modulate(xn, g2, sh2, sc2).astype(h_ref.dtype)


def _merge_kernel(a_ref, b_ref, w_ref, x_ref, gate_ref, g2_ref, sh2_ref, sc2_ref, xo_ref, h_ref):
    ca = a_ref.shape[1]
    o = _dot(a_ref[...], w_ref[:ca, :]) + _dot(b_ref[...], w_ref[ca:, :])
    _residual_mod(x_ref[...], gate_ref[...], o, g2_ref[...], sh2_ref[...], sc2_ref[...],
                  xo_ref, h_ref)


def _merge(a, b, w_out, x, gate, g2, sh2, sc2, *, tm, per_row):
    t, d = x.shape
    ca = a.shape[1]
    row = lambda i: (i, 0)
    mod = _mod_spec(per_row, tm, d, False)
    return pl.pallas_call(
        _merge_kernel,
        grid=(t // tm,),
        in_specs=[pl.BlockSpec((tm, ca), row), pl.BlockSpec((tm, ca), row),
                  pl.BlockSpec(w_out.shape, lambda i: (0, 0)),
                  pl.BlockSpec((tm, d), row), mod,
                  pl.BlockSpec((1, d), lambda i: (0, 0)), mod, mod],
        out_specs=[pl.BlockSpec((tm, d), row), pl.BlockSpec((tm, d), row)],
        out_shape=[jax.ShapeDtypeStruct((t, d), F32), jax.ShapeDtypeStruct((t, d), BF16)],
        compiler_params=_cp("parallel"),
        name="merge",
    )(a, b, w_out, x, gate, g2, sh2, sc2)


def _uv_kernel(x_ref, g_ref, sh_ref, sc_ref, w_ref, lng_ref, lnb_ref, u_ref, vn_ref, h_scr):
    j = pl.program_id(1)

    @pl.when(j == 0)
    def _():
        h_scr[...] = _modulate(x_ref[...], g_ref[...], sh_ref[...], sc_ref[...]).astype(BF16)

    z = _gelu(_dot(h_scr[...], w_ref[...]))

    @pl.when(j == 0)
    def _():
        u_ref[...] = z.astype(u_ref.dtype)

    @pl.when(j == 1)
    def _():
        mu = jnp.mean(z, axis=-1, keepdims=True)
        zc = z - mu
        var = jnp.mean(zc * zc, axis=-1, keepdims=True)
        vn_ref[...] = (zc * lax.rsqrt(var + EPS) * lng_ref[...] + lnb_ref[...]).astype(vn_ref.dtype)


def _uv(x, g, shift, scale, w_uv, ln_g, ln_b, *, tm, per_row, vn_dtype):
    t, d = x.shape
    sw = w_uv.shape[1] // 2
    row = lambda i, j: (i, 0)
    const = lambda i, j: (0, 0)
    return pl.pallas_call(
        _uv_kernel,
        grid=(t // tm, 2),
        in_specs=[pl.BlockSpec((tm, d), row), pl.BlockSpec((1, d), const),
                  _mod_spec(per_row, tm, d, True), _mod_spec(per_row, tm, d, True),
                  pl.BlockSpec((d, sw), lambda i, j: (0, j)),
                  pl.BlockSpec((1, sw), const), pl.BlockSpec((1, sw), const)],
        out_specs=[pl.BlockSpec((tm, sw), row), pl.BlockSpec((tm, sw), row)],
        out_shape=[jax.ShapeDtypeStruct((t, sw), BF16), jax.ShapeDtypeStruct((t, sw), vn_dtype)],
        scratch_shapes=[pltpu.VMEM((tm, d), BF16)],
        compiler_params=_cp("parallel", "arbitrary"),
        name="sg_uv",
    )(x, g, shift, scale, w_uv, ln_g, ln_b)


def _sgate_kernel(u_ref, vn_ref, ws_ref, bs_ref, w_ref, x_ref, gate_ref, g2_ref, sh2_ref, sc2_ref,
                  xo_ref, h_ref, y_scr):
    tm, sw = u_ref.shape
    gw = sw // SG_GROUPS
    for c in range(tm // SG_CHUNK):
        rs = slice(c * SG_CHUNK, (c + 1) * SG_CHUNK)
        for g in range(SG_GROUPS):
            cs = slice(g * gw, (g + 1) * gw)
            sv = _dot(ws_ref[g], vn_ref[rs, cs].astype(BF16)) + bs_ref[:, g:g + 1]
            y_scr[rs, cs] = (u_ref[rs, cs].astype(F32) * sv).astype(BF16)
    o = _dot(y_scr[...], w_ref[...])
    _residual_mod(x_ref[...], gate_ref[...], o, g2_ref[...], sh2_ref[...], sc2_ref[...],
                  xo_ref, h_ref)


def _sgate(u, vn, ws, bs_t, w_out, x, gate, g2, sh2, sc2, *, tm, per_row):
    t, d = x.shape
    sw = u.shape[1]
    row = lambda i: (i, 0)
    mod = _mod_spec(per_row, tm, d, False)
    return pl.pallas_call(
        _sgate_kernel,
        grid=(t // tm,),
        in_specs=[pl.BlockSpec((tm, sw), row), pl.BlockSpec((tm, sw), row),
                  pl.BlockSpec(ws.shape, lambda i: (0, 0, 0)),
                  pl.BlockSpec(bs_t.shape, lambda i: (0, 0)),
                  pl.BlockSpec(w_out.shape, lambda i: (0, 0)),
                  pl.BlockSpec((tm, d), row), mod,
                  pl.BlockSpec((1, d), lambda i: (0, 0)), mod, mod],
        out_specs=[pl.BlockSpec((tm, d), row), pl.BlockSpec((tm, d), row)],
        out_shape=[jax.ShapeDtypeStruct((t, d), F32), jax.ShapeDtypeStruct((t, d), BF16)],
        scratch_shapes=[pltpu.VMEM((tm, sw), BF16)],
        compiler_params=_cp("parallel"),
        name="sg_gate",
    )(u, vn, ws, bs_t, w_out, x, gate, g2, sh2, sc2)


def _topk_rows(s, key_io, vals_ref):
    def body(r, carry):
        s, rank = carry
        m = jnp.max(s, axis=0, keepdims=True)
        first = jnp.min(jnp.where(s == m, key_io, float(PEER_KEYS)), axis=0, keepdims=True)
        hit = key_io == first
        vals_ref[pl.ds(r, 1), :] = m
        return jnp.where(hit, -jnp.inf, s), jnp.where(hit, r.astype(F32), rank)

    _, rank = lax.fori_loop(0, PEER_TOPK, body, (s, jnp.full(s.shape, RANK_NONE, F32)))
    return rank


def _pair_candidates(v1, v2):
    groups = [v1[0:1] + v2[0:16]]
    groups += [v1[r1:r1 + 1] + v2[0:8] for r1 in range(1, 8)]
    groups.append(v1[8:16] + v2[0:1])
    return jnp.concatenate(groups, axis=0)


def _pair_tables():
    r1, r2 = [], []
    for i in range(16):
        r1.append(0), r2.append(i)
    for a in range(1, 8):
        for i in range(8):
            r1.append(a), r2.append(i)
    for a in range(8, 16):
        r1.append(a), r2.append(0)
    return r1, r2


def _peer_route_kernel(h_ref, wq_ref, keys_ref, flat_ref, grp_ref,
                       r2_ref, e2_ref, c_ref, e1_ref, qt_scr, v1_scr, v2_scr):
    tb = h_ref.shape[0]
    qt_scr[...] = _dot_nt(wq_ref[...], h_ref[...]).astype(BF16)
    key_io = lax.broadcasted_iota(jnp.int32, (PEER_KEYS, LANES), 0).astype(F32)
    flat = jnp.broadcast_to(flat_ref[...], (flat_ref.shape[0], LANES))
    ncand = flat.shape[0]

    def per_head_block(idx, _):
        h = idx // (tb // LANES)
        lb = idx % (tb // LANES)
        col = pl.multiple_of(lb * LANES, LANES)
        row = pl.multiple_of(h * 2 * PEER_KEYS, 2 * PEER_KEYS)
        s1 = _dot(keys_ref[h, 0], qt_scr[pl.ds(row, PEER_KEYS), pl.ds(col, LANES)])
        s2 = _dot(keys_ref[h, 1], qt_scr[pl.ds(row + PEER_KEYS, PEER_KEYS), pl.ds(col, LANES)])
        rank1 = _topk_rows(s1, key_io, v1_scr)
        rank2 = _topk_rows(s2, key_io, v2_scr)
        v1 = v1_scr[...]
        v2 = v2_scr[...]
        cand = _pair_candidates(v1, v2)
        cur = cand
        sel = jnp.zeros_like(cand)
        for _ in range(PEER_TOPK):
            m = jnp.max(cur, axis=0, keepdims=True)
            first = jnp.min(jnp.where(cur == m, flat, 1e9), axis=0, keepdims=True)
            hit = flat == first
            sel = jnp.where(hit, 1.0, sel)
            cur = jnp.where(hit, -jnp.inf, cur)
        top = v1[0:1] + v2[0:1]
        z = jnp.sum(sel * jnp.exp(cand - top), axis=0, keepdims=True)
        selp = jnp.concatenate([sel, jnp.zeros((LANES - ncand, LANES), F32)], axis=0)
        cnt = _dot(grp_ref[...], selp.astype(BF16))
        cj = jnp.zeros((PEER_KEYS, LANES), F32)
        for r in range(PEER_TOPK):
            cj = jnp.where(rank1 == float(r), cnt[r:r + 1], cj)
        r2_ref[h, :, pl.ds(col, LANES)] = rank2.astype(BF16)
        e2_ref[h, :, pl.ds(col, LANES)] = jnp.exp(s2 - v2[0:1]).astype(BF16)
        c_ref[h, :, pl.ds(col, LANES)] = cj
        e1_ref[h, :, pl.ds(col, LANES)] = jnp.exp(s1 - v1[0:1]) / z
        return 0

    lax.fori_loop(0, PEER_HEADS * (tb // LANES), per_head_block, 0)


def _peer_route(h, wq_t, keys, *, tb):
    t, d = h.shape
    r1, r2 = _pair_tables()
    flat = jnp.asarray([a * 16 + b for a, b in zip(r1, r2)], F32).reshape(-1, 1)
    grp = jnp.asarray([[1.0 if (c < len(r1) and r1[c] == a) else 0.0 for c in range(LANES)]
                       for a in range(PEER_TOPK)], BF16)
    tab = lambda dt: jax.ShapeDtypeStruct((PEER_HEADS, PEER_KEYS, t), dt)
    tab_spec = pl.BlockSpec((PEER_HEADS, PEER_KEYS, tb), lambda i: (0, 0, i))
    return pl.pallas_call(
        _peer_route_kernel,
        grid=(t // tb,),
        in_specs=[pl.BlockSpec((tb, d), lambda i: (i, 0)),
                  pl.BlockSpec(wq_t.shape, lambda i: (0, 0)),
                  pl.BlockSpec(keys.shape, lambda i: (0, 0, 0, 0)),
                  pl.BlockSpec(flat.shape, lambda i: (0, 0)),
                  pl.BlockSpec(grp.shape, lambda i: (0, 0))],
        out_specs=[tab_spec] * 4,
        out_shape=[tab(BF16), tab(BF16), tab(F32), tab(F32)],
        scratch_shapes=[pltpu.VMEM((wq_t.shape[0], tb), BF16),
                        pltpu.VMEM((PEER_TOPK, LANES), F32), pltpu.VMEM((PEER_TOPK, LANES), F32)],
        compiler_params=_cp("parallel"),
        name="peer_route",
    )(h, wq_t, keys, flat, grp)


def _peer_dense_kernel(h_ref, u_ref, vt_ref, r2_ref, e2_ref, c_ref, e1_ref, o_ref, g_scr):
    e = pl.program_id(1)
    tb = h_ref.shape[0]
    nchunk = u_ref.shape[0] // PEER_KEYS
    sub = r2_ref.shape[2]
    at = _dot_nt(u_ref[...], h_ref[...])
    for jj in range(nchunk):
        w = jnp.zeros((PEER_KEYS // sub, sub, tb), BF16)
        for h in range(PEER_HEADS):
            cb = jnp.broadcast_to(c_ref[h, jj:jj + 1, :], (sub, tb)).astype(BF16)[None]
            eb = jnp.broadcast_to(e1_ref[h, jj:jj + 1, :], (sub, tb)).astype(BF16)[None]
            w = w + jnp.where(r2_ref[h] < cb, e2_ref[h] * eb, jnp.zeros_like(w))
        a = _gelu(at[jj * PEER_KEYS:(jj + 1) * PEER_KEYS, :]).astype(BF16)
        g_scr[jj * PEER_KEYS:(jj + 1) * PEER_KEYS, :] = a * w.reshape(PEER_KEYS, tb)
    contrib = _dot(vt_ref[...], g_scr[...])

    @pl.when(e == 0)
    def _():
        o_ref[...] = contrib

    @pl.when(e > 0)
    def _():
        o_ref[...] += contrib


def _peer_dense(h, u_tab, v_tab_t, r2, e2, c, e1, *, tb, te):
    t, d = h.shape
    ne = u_tab.shape[0]
    sub = 16
    r2 = r2.reshape(PEER_HEADS, PEER_KEYS // sub, sub, t)
    e2 = e2.reshape(PEER_HEADS, PEER_KEYS // sub, sub, t)
    nchunk = te // PEER_KEYS
    full = pl.BlockSpec((PEER_HEADS, PEER_KEYS // sub, sub, tb), lambda i, e: (0, 0, 0, i))
    rows = pl.BlockSpec((PEER_HEADS, nchunk, tb), lambda i, e: (0, e, i))
    return pl.pallas_call(
        _peer_dense_kernel,
        grid=(t // tb, ne // te),
        in_specs=[pl.BlockSpec((tb, d), lambda i, e: (i, 0)),
                  pl.BlockSpec((te, d), lambda i, e: (e, 0)),
                  pl.BlockSpec((d, te), lambda i, e: (0, e)),
                  full, full, rows, rows],
        out_specs=pl.BlockSpec((d, tb), lambda i, e: (0, i)),
        out_shape=jax.ShapeDtypeStruct((d, t), F32),
        scratch_shapes=[pltpu.VMEM((te, tb), BF16)],
        compiler_params=_cp("parallel", "arbitrary"),
        name="peer_dense",
    )(h, u_tab, v_tab_t, r2, e2, c, e1)


def _peer_out_kernel(ot_ref, x_ref, gate_ref, xo_ref):
    xo_ref[...] = x_ref[...] + gate_ref[...] * ot_ref[...].T


def _peer_out(ot, x, gate, *, tm, per_row):
    t, d = x.shape
    row = lambda i: (i, 0)
    return pl.pallas_call(
        _peer_out_kernel,
        grid=(t // tm,),
        in_specs=[pl.BlockSpec((d, tm), lambda i: (0, i)), pl.BlockSpec((tm, d), row),
                  _mod_spec(per_row, tm, d, False)],
        out_specs=pl.BlockSpec((tm, d), row),
        out_shape=jax.ShapeDtypeStruct((t, d), F32),
        compiler_params=_cp("parallel"),
        name="peer_out",
    )(ot, x, gate)


def _peer(h, x, gate, wq_t, keys, u_tab, v_tab_t, *, tb, tm, per_row):
    r2, e2, c, e1 = _peer_route(h, wq_t, keys, tb=min(tb, 256))
    ot = _peer_dense(h, u_tab, v_tab_t, r2, e2, c, e1, tb=tb, te=1024)
    return _peer_out(ot, x, gate, tm=tm, per_row=per_row)


def kernel(x_prompt, x_sample, c_prompt, c_sample, cache_pool, cache_k, cache_v, cache_logf, w_ada, b_ada, norm1, norm2, w_in_ab, b_f, w_pool, pool_scale, q_gain, k_gain, w_out_ab, w_uv, sg_ln_g, sg_ln_b, w_s, b_s, w_out_sg, peer_wq, peer_keys, peer_u, peer_v):
    nb, seq, d = x_prompt.shape
    db, dl, _ = x_sample.shape
    depth = w_ada.shape[0]
    past = cache_k.shape[2]
    pool_hist = cache_pool.shape[2]
    assert nb == 1 and db * dl == LANES and dl == 16 and pool_hist == dl - 1
    fox_w = FOX_HEADS * HEAD_DIM
    pool_w = w_pool.shape[1] * w_pool.shape[2]
    ts = db * dl

    c_rows = jnp.concatenate(
        [c_prompt, c_sample, jnp.zeros((16 - nb - db, d), F32)], axis=0)
    mods = _ada(c_rows, w_ada, b_ada)

    def mod_p(layer, k):
        return mods[layer, 0:1, k * d:(k + 1) * d]

    def mod_s(layer, k):
        return jnp.repeat(mods[layer, 1:1 + db, k * d:(k + 1) * d], dl, axis=0)

    xp = x_prompt.reshape(seq, d)
    xs = x_sample.reshape(ts, d)
    tm_p = 512 if seq % 512 == 0 else 256
    outs = {k: [] for k in ("pool_p", "pool_s", "kp", "ks", "vp", "vs", "lfp", "lfs", "sgv")}

    for layer in range(depth):
        j = layer // 2
        g1 = norm1[layer].reshape(1, d)
        g2 = norm2[layer].reshape(1, d)
        if layer % 2 == 0:
            w_in = w_in_ab[j]
            w_main = w_in[:, :pool_w + 3 * fox_w].astype(BF16)
            wf_t = w_in[:, pool_w + 3 * fox_w:].T.astype(BF16)
            bfc = b_f[j].reshape(FOX_HEADS, 1)
            qg = q_gain[j].reshape(1, fox_w)
            kg = k_gain[j].reshape(1, fox_w)
            wp = w_pool[j].astype(BF16)
            psc = pool_scale[j].reshape(1, pool_w)
            wo = w_out_ab[j].astype(BF16)

            p, qb, k, kb, v, vb, lf, fc = _proj(
                xp, g1, mod_p(layer, 0), mod_p(layer, 1), w_main, wf_t, bfc, qg, kg,
                tm=tm_p, per_row=False, do_cumsum=True)
            a_out = _pool_prompt(p, wp, psc, tm=256)
            b_out = _attn_prompt(qb, kb, vb, fc, tq=tm_p)
            xp, hp = _merge(a_out, b_out, wo, xp, mod_p(layer, 2), g2, mod_p(layer, 3),
                            mod_p(layer, 4), tm=256, per_row=False)
            outs["pool_p"].append(p[-pool_hist:].reshape(nb, pool_hist, pool_w))
            outs["kp"].append(k.reshape(nb, seq, FOX_HEADS, HEAD_DIM))
            outs["vp"].append(v.reshape(nb, seq, FOX_HEADS, HEAD_DIM))
            outs["lfp"].append(lf.T.reshape(nb, seq, FOX_HEADS))

            p, qb, k, kb, v, vb, lf, _ = _proj(
                xs, g1, mod_s(layer, 0), mod_s(layer, 1), w_main, wf_t, bfc, qg, kg,
                tm=ts, per_row=True, do_cumsum=False)
            p3 = p.reshape(db, dl, pool_w)
            buf = jnp.concatenate([jnp.zeros((db, 1, pool_w), F32), cache_pool[j], p3], axis=1)
            a_out = _pool_sample(buf.reshape(db * 2 * dl, pool_w), p, wp, psc, pos0=past)
            lfn = jnp.transpose(lf.reshape(FOX_HEADS, db, dl), (1, 0, 2))
            lfn = jnp.pad(lfn, ((0, 0), (0, 0), (0, LANES - dl)))
            b_out = _attn_sample(
                qb, kb, vb, lfn, cache_k[j].reshape(db, past, fox_w),
                cache_v[j].reshape(db, past, fox_w), jnp.transpose(cache_logf[j], (0, 2, 1)),
                tk=min(past, 1024))
            xs, hs = _merge(a_out, b_out, wo, xs, mod_s(layer, 2), g2, mod_s(layer, 3),
                            mod_s(layer, 4), tm=ts, per_row=True)
            outs["pool_s"].append(jnp.concatenate([cache_pool[j], p3], axis=1)[:, -pool_hist:])
            outs["ks"].append(k.reshape(db, dl, FOX_HEADS, HEAD_DIM))
            outs["vs"].append(v.reshape(db, dl, FOX_HEADS, HEAD_DIM))
            outs["lfs"].append(lf.T.reshape(db, dl, FOX_HEADS))
        else:
            wuv = w_uv[j].astype(BF16)
            lng = sg_ln_g[j].reshape(1, -1)
            lnb = sg_ln_b[j].reshape(1, -1)
            wo = w_out_sg[j].astype(BF16)
            tri = jnp.tril(jnp.ones((SG_CHUNK, SG_CHUNK), bool))
            ws_p = jnp.where(tri, w_s[j], 0.0).astype(BF16)
            bs_p = b_s[j].T
            ws16 = jnp.where(tri[:dl, :dl], w_s[j][:, :dl, :dl], 0.0)
            ws_s = jnp.einsum("ab,gts->gatbs", jnp.eye(db, dtype=F32), ws16)
            ws_s = ws_s.reshape(SG_GROUPS, ts, ts).astype(BF16)
            bs_s = jnp.tile(b_s[j][:, :dl], (1, db)).T

            u, vn = _uv(xp, g1, mod_p(layer, 0), mod_p(layer, 1), wuv, lng, lnb,
                        tm=tm_p, per_row=False, vn_dtype=BF16)
            xp, hp = _sgate(u, vn, ws_p, bs_p, wo, xp, mod_p(layer, 2), g2, mod_p(layer, 3),
                            mod_p(layer, 4), tm=256, per_row=False)
            u, vn = _uv(xs, g1, mod_s(layer, 0), mod_s(layer, 1), wuv, lng, lnb,
                        tm=ts, per_row=True, vn_dtype=F32)
            xs, hs = _sgate(u, vn, ws_s, bs_s, wo, xs, mod_s(layer, 2), g2, mod_s(layer, 3),
                            mod_s(layer, 4), tm=ts, per_row=True)
            outs["sgv"].append(vn.reshape(db, dl, -1))

        wq_t = peer_wq[layer].T.astype(BF16)
        keys = peer_keys[layer].astype(BF16)
        u_tab = peer_u[layer].astype(BF16)
        v_tab_t = peer_v[layer].T.astype(BF16)
        xp = _peer(hp, xp, mod_p(layer, 5), wq_t, keys, u_tab, v_tab_t,
                   tb=512 if seq % 512 == 0 else 256, tm=256, per_row=False)
        xs = _peer(hs, xs, mod_s(layer, 5), wq_t, keys, u_tab, v_tab_t,
                   tb=ts, tm=ts, per_row=True)

    st = lambda key: jnp.stack(outs[key])
    return (xp.reshape(nb, seq, d), xs.reshape(db, dl, d), st("pool_p"), st("pool_s"),
            st("kp"), st("ks"), st("vp"), st("vs"), st("lfp"), st("lfs"), st("sgv"))
```

```python
import functools

import jax
import jax.numpy as jnp
from jax import lax
from jax.experimental import pallas as pl
from jax.experimental.pallas import tpu as pltpu

F32 = jnp.float32
BF16 = jnp.bfloat16
EPS = 1e-6

LANES = 128
POOL_WINDOWS = (2, 4, 8, 16)
POOL_HALO = 128
FOX_HEADS = 8
HEAD_DIM = 128
SG_GROUPS = 8
SG_CHUNK = 128
PEER_HEADS = 8
PEER_KEYS = 128
PEER_TOPK = 16
PEER_EXPERT_BLOCK = 1024
PEER_SUB = 256
NEG_BIG = -1e30
RANK_NONE = 64.0
VMEM_LIMIT = 56 * 1024 * 1024

NT_DIMS = (((1,), (1,)), ((), ()))


def _cp(*sem):
    return pltpu.CompilerParams(dimension_semantics=sem, vmem_limit_bytes=VMEM_LIMIT)


def _dot(a, b):
    return jnp.dot(a, b, preferred_element_type=F32)


def _dot_nt(a, b):
    return lax.dot_general(a, b, NT_DIMS, preferred_element_type=F32)


def _modulate(x, g, shift, scale):
    ms = jnp.mean(x * x, axis=-1, keepdims=True)
    return x * lax.rsqrt(ms + EPS) * g * (1.0 + scale) + shift


def _gelu(x):
    cdf = 0.5 * (1.0 + jnp.tanh(0.7978845608028654 * (x + 0.044715 * (x * x * x))))
    return x * cdf


def _log_sigmoid(x):
    return -(jnp.maximum(-x, 0.0) + jnp.log(1.0 + jnp.exp(-jnp.abs(x))))


def _lane_cumsum(x):
    n = x.shape[-1]
    lane = lax.broadcasted_iota(jnp.int32, x.shape, x.ndim - 1)
    s = 1
    while s < n:
        x = x + jnp.where(lane >= s, pltpu.roll(x, s, x.ndim - 1), 0.0)
        s *= 2
    return x


def _ada_kernel(c_ref, w_ref, b_ref, o_ref):
    c = c_ref[...]
    a = (c * (1.0 / (1.0 + jnp.exp(-c)))).astype(BF16)
    o_ref[0] = _dot(a, w_ref[0].astype(BF16)) + b_ref[0]


def _ada(c_rows, w_ada, b_ada):
    nl, d, n = w_ada.shape
    r = c_rows.shape[0]
    tn = 1024
    return pl.pallas_call(
        _ada_kernel,
        grid=(nl, n // tn),
        in_specs=[pl.BlockSpec((r, d), lambda l, j: (0, 0)),
                  pl.BlockSpec((1, d, tn), lambda l, j: (l, 0, j)),
                  pl.BlockSpec((1, 1, tn), lambda l, j: (l, 0, j))],
        out_specs=pl.BlockSpec((1, r, tn), lambda l, j: (l, 0, j)),
        out_shape=jax.ShapeDtypeStruct((nl, r, n), F32),
        compiler_params=_cp("parallel", "parallel"),
        name="ada",
    )(c_rows, w_ada, b_ada.reshape(nl, 1, n))


def _mod_spec(per_row, tm, d, nj):
    if nj:
        return (pl.BlockSpec((tm, d), lambda i, j: (i, 0)) if per_row
                else pl.BlockSpec((1, d), lambda i, j: (0, 0)))
    return (pl.BlockSpec((tm, d), lambda i: (i, 0)) if per_row
            else pl.BlockSpec((1, d), lambda i: (0, 0)))


def _head_rms(z, gain):
    outs = []
    for h in range(FOX_HEADS):
        zh = z[:, h * HEAD_DIM:(h + 1) * HEAD_DIM]
        ms = jnp.mean(zh * zh, axis=-1, keepdims=True)
        outs.append(zh * lax.rsqrt(ms + EPS))
    return jnp.concatenate(outs, axis=-1) * gain


def _proj_kernel(x_ref, g_ref, sh_ref, sc_ref, w_ref, wf_ref, bf_ref, qg_ref, kg_ref,
                 p_ref, q_ref, k_ref, kb_ref, v_ref, vb_ref, lf_ref, fc_ref,
                 h_scr, carry_scr, *, do_cumsum):
    i = pl.program_id(0)
    j = pl.program_id(1)

    @pl.when(j == 0)
    def _():
        hb = _modulate(x_ref[...], g_ref[...], sh_ref[...], sc_ref[...]).astype(BF16)
        h_scr[...] = hb
        lf = _log_sigmoid(_dot_nt(wf_ref[...], hb) + bf_ref[...])
        lf_ref[...] = lf
        if do_cumsum:
            @pl.when(i == 0)
            def _():
                carry_scr[...] = jnp.zeros_like(carry_scr)
            fc = _lane_cumsum(lf) + carry_scr[:, 0:1]
            fc_ref[...] = fc
            carry_scr[...] = jnp.broadcast_to(fc[:, -1:], carry_scr.shape)
        else:
            fc_ref[...] = lf

    z = _dot(h_scr[...], w_ref[...])

    @pl.when(j == 0)
    def _():
        p_ref[...] = z

    @pl.when(j == 1)
    def _():
        q_ref[...] = (_head_rms(z, qg_ref[...]) * (HEAD_DIM ** -0.5)).astype(BF16)

    @pl.when(j == 2)
    def _():
        kn = _head_rms(z, kg_ref[...])
        k_ref[...] = kn
        kb_ref[...] = kn.astype(BF16)

    @pl.when(j == 3)
    def _():
        v_ref[...] = z
        vb_ref[...] = z.astype(BF16)


def _proj(x, g, shift, scale, w_main, wf_t, b_f, q_gain, k_gain, *, tm, per_row, do_cumsum):
    t, d = x.shape
    w4 = w_main.shape[1] // 4
    nh = wf_t.shape[0]
    row = lambda i, j: (i, 0)
    const = lambda i, j: (0, 0)
    tok_f32 = jax.ShapeDtypeStruct((t, w4), F32)
    tok_b16 = jax.ShapeDtypeStruct((t, w4), BF16)
    head_t = jax.ShapeDtypeStruct((nh, t), F32)
    return pl.pallas_call(
        functools.partial(_proj_kernel, do_cumsum=do_cumsum),
        grid=(t // tm, 4),
        in_specs=[pl.BlockSpec((tm, d), row),
                  pl.BlockSpec((1, d), const),
                  _mod_spec(per_row, tm, d, True),
                  _mod_spec(per_row, tm, d, True),
                  pl.BlockSpec((d, w4), lambda i, j: (0, j)),
                  pl.BlockSpec((nh, d), const),
                  pl.BlockSpec((nh, 1), const),
                  pl.BlockSpec((1, w4), const),
                  pl.BlockSpec((1, w4), const)],
        out_specs=[pl.BlockSpec((tm, w4), row)] * 6
                  + [pl.BlockSpec((nh, tm), lambda i, j: (0, i))] * 2,
        out_shape=[tok_f32, tok_b16, tok_f32, tok_b16, tok_f32, tok_b16, head_t, head_t],
        scratch_shapes=[pltpu.VMEM((tm, d), BF16), pltpu.VMEM((nh, LANES), F32)],
        compiler_params=_cp("arbitrary", "arbitrary"),
        name="proj",
    )(x, g, shift, scale, w_main, wf_t, b_f, q_gain, k_gain)


def _pool_kernel(ext_ref, tok_ref, w_ref, ps_ref, o_ref, *, prompt, pos0):
    i = pl.program_id(0)
    tok = tok_ref[...]
    tm = tok.shape[0]
    if prompt:
        halo = jnp.where(i == 0, 0.0, ext_ref[...])
        ext = jnp.concatenate([halo, tok], axis=0)
    else:
        ext = ext_ref[...]
    ke = ext.shape[0]
    hi = ext.astype(BF16)
    lo = (ext - hi.astype(F32)).astype(BF16)
    m_io = lax.broadcasted_iota(jnp.int32, (tm, ke), 0)
    c_io = lax.broadcasted_iota(jnp.int32, (tm, ke), 1)
    r_io = lax.broadcasted_iota(jnp.int32, (tm, 1), 0)
    if prompt:
        tgt = m_io + POOL_HALO
        pos = pos0 + i * tm + r_io
    else:
        tgt = m_io + ((m_io >> 4) << 4) + 16
        pos = pos0 + (r_io & 15)
    dlt = tgt - c_io
    gw = w_ref.shape[1]
    for gi, win in enumerate(POOL_WINDOWS):
        sl = slice(gi * gw, (gi + 1) * gw)
        band = jnp.where(dlt >= 0, jnp.where(dlt < win, 1.0, 0.0), 0.0).astype(BF16)
        wsum = _dot(band, hi[:, sl]) + _dot(band, lo[:, sl])
        cnt = jnp.minimum(win, pos + 1).astype(F32)
        dd = wsum / cnt - tok[:, sl]
        o = _dot(dd.astype(BF16), w_ref[gi]) * ps_ref[:, sl]
        o_ref[:, sl] = o.astype(o_ref.dtype)


def _pool_prompt(p, w_pool, pool_scale, *, tm):
    t, c = p.shape
    per = tm // POOL_HALO
    return pl.pallas_call(
        functools.partial(_pool_kernel, prompt=True, pos0=0),
        grid=(t // tm,),
        in_specs=[pl.BlockSpec((POOL_HALO, c), lambda i: (jnp.maximum(i * per - 1, 0), 0)),
                  pl.BlockSpec((tm, c), lambda i: (i, 0)),
                  pl.BlockSpec(w_pool.shape, lambda i: (0, 0, 0)),
                  pl.BlockSpec((1, c), lambda i: (0, 0))],
        out_specs=pl.BlockSpec((tm, c), lambda i: (i, 0)),
        out_shape=jax.ShapeDtypeStruct((t, c), BF16),
        compiler_params=_cp("parallel"),
        name="pool_prompt",
    )(p, p, w_pool, pool_scale)


def _pool_sample(buf, p, w_pool, pool_scale, *, pos0):
    t, c = p.shape
    return pl.pallas_call(
        functools.partial(_pool_kernel, prompt=False, pos0=pos0),
        grid=(1,),
        in_specs=[pl.BlockSpec(buf.shape, lambda i: (0, 0)),
                  pl.BlockSpec((t, c), lambda i: (0, 0)),
                  pl.BlockSpec(w_pool.shape, lambda i: (0, 0, 0)),
                  pl.BlockSpec((1, c), lambda i: (0, 0))],
        out_specs=pl.BlockSpec((t, c), lambda i: (0, 0)),
        out_shape=jax.ShapeDtypeStruct((t, c), BF16),
        compiler_params=_cp("arbitrary"),
        name="pool_sample",
    )(buf, p, w_pool, pool_scale)


def _softmax_step(carry, s, v):
    m, l, acc = carry
    m_new = jnp.maximum(m, jnp.max(s, axis=-1, keepdims=True))
    alpha = jnp.exp(m - m_new)
    p = jnp.exp(s - m_new)
    l = alpha * l + jnp.sum(p, axis=-1, keepdims=True)
    acc = alpha * acc + _dot(p.astype(BF16), v)
    return m_new, l, acc


def _attn_prompt_kernel(q_ref, k_ref, v_ref, fk_ref, o_ref, *, tq):
    i = pl.program_id(1)
    q = q_ref[...]

    def chunk(kj, carry, masked):
        off = pl.multiple_of(kj * tq, tq)
        s = _dot_nt(q, k_ref[pl.ds(off, tq), :]) - fk_ref[:, pl.ds(off, tq)]
        if masked:
            r_io = lax.broadcasted_iota(jnp.int32, (tq, tq), 0)
            c_io = lax.broadcasted_iota(jnp.int32, (tq, tq), 1)
            s = jnp.where(c_io <= r_io, s, NEG_BIG)
        return _softmax_step(carry, s, v_ref[pl.ds(off, tq), :])

    init = (jnp.full((tq, 1), NEG_BIG, F32), jnp.zeros((tq, 1), F32),
            jnp.zeros((tq, HEAD_DIM), F32))
    carry = lax.fori_loop(0, i, lambda kj, c: chunk(kj, c, False), init)
    _, l, acc = chunk(i, carry, True)
    o_ref[...] = (acc / l).astype(o_ref.dtype)


def _attn_prompt(qb, kb, vb, fcum, *, tq):
    t, w = qb.shape
    nh = w // HEAD_DIM
    return pl.pallas_call(
        functools.partial(_attn_prompt_kernel, tq=tq),
        grid=(nh, t // tq),
        in_specs=[pl.BlockSpec((tq, HEAD_DIM), lambda h, i: (i, h)),
                  pl.BlockSpec((t, HEAD_DIM), lambda h, i: (0, h)),
                  pl.BlockSpec((t, HEAD_DIM), lambda h, i: (0, h)),
                  pl.BlockSpec((None, 1, t), lambda h, i: (h, 0, 0))],
        out_specs=pl.BlockSpec((tq, HEAD_DIM), lambda h, i: (i, h)),
        out_shape=jax.ShapeDtypeStruct((t, w), BF16),
        compiler_params=_cp("parallel", "parallel"),
        name="attn_prompt",
    )(qb, kb, vb, fcum.reshape(nh, 1, t))


def _attn_sample_kernel(q_ref, ck_ref, cv_ref, clf_ref, kn_ref, vn_ref, lfn_ref, o_ref,
                        qbd_scr, m_scr, l_scr, acc_scr, fcar_scr):
    c = pl.program_id(1)
    nl = q_ref.shape[0]
    rows, width = qbd_scr.shape

    @pl.when(c == 0)
    def _():
        q = q_ref[...]
        r_io = lax.broadcasted_iota(jnp.int32, (rows, width), 0)
        c_io = lax.broadcasted_iota(jnp.int32, (rows, width), 1)
        qt = jnp.concatenate([q] * FOX_HEADS, axis=0)
        qbd_scr[...] = jnp.where((r_io // nl) == (c_io // HEAD_DIM), qt, jnp.zeros_like(qt))
        m_scr[...] = jnp.full_like(m_scr, NEG_BIG)
        l_scr[...] = jnp.zeros_like(l_scr)
        acc_scr[...] = jnp.zeros_like(acc_scr)
        fcar_scr[...] = jnp.zeros_like(fcar_scr)

    def head_rows(f):
        return jnp.concatenate(
            [jnp.broadcast_to(f[h:h + 1], (nl, f.shape[1])) for h in range(FOX_HEADS)], axis=0)

    qbd = qbd_scr[...]
    fc = _lane_cumsum(clf_ref[...]) + fcar_scr[:, 0:1]
    fcar_scr[...] = jnp.broadcast_to(fc[:, -1:], fcar_scr.shape)
    s = _dot_nt(qbd, ck_ref[...].astype(BF16)) - head_rows(fc)
    carry = _softmax_step((m_scr[...], l_scr[...], acc_scr[...]), s, cv_ref[...].astype(BF16))
    m_scr[...], l_scr[...], acc_scr[...] = carry

    @pl.when(c == pl.num_programs(1) - 1)
    def _():
        kpad = jnp.zeros((LANES - nl, width), BF16)
        kn = jnp.concatenate([kn_ref[...], kpad], axis=0)
        vn = jnp.concatenate([vn_ref[...], kpad], axis=0)
        fn = _lane_cumsum(lfn_ref[...]) + fcar_scr[:, 0:1]
        sn = _dot_nt(qbd, kn) - head_rows(fn)
        r_io = lax.broadcasted_iota(jnp.int32, sn.shape, 0)
        c_io = lax.broadcasted_iota(jnp.int32, sn.shape, 1)
        sn = jnp.where(c_io <= (r_io % nl), sn, NEG_BIG)
        _, l, acc = _softmax_step((m_scr[...], l_scr[...], acc_scr[...]), sn, vn)
        o = acc / l
        o_ref[...] = jnp.concatenate(
            [o[h * nl:(h + 1) * nl, h * HEAD_DIM:(h + 1) * HEAD_DIM] for h in range(FOX_HEADS)],
            axis=-1).astype(o_ref.dtype)


def _attn_sample(qb, kb, vb, lfn, cache_k, cache_v, cache_lf_t, *, tk):
    b, past, w = cache_k.shape
    nl = qb.shape[0] // b
    nh = w // HEAD_DIM
    rows = nh * nl
    return pl.pallas_call(
        _attn_sample_kernel,
        grid=(b, past // tk),
        in_specs=[pl.BlockSpec((nl, w), lambda s, c: (s, 0)),
                  pl.BlockSpec((None, tk, w), lambda s, c: (s, c, 0)),
                  pl.BlockSpec((None, tk, w), lambda s, c: (s, c, 0)),
                  pl.BlockSpec((None, nh, tk), lambda s, c: (s, 0, c)),
                  pl.BlockSpec((nl, w), lambda s, c: (s, 0)),
                  pl.BlockSpec((nl, w), lambda s, c: (s, 0)),
                  pl.BlockSpec((None, nh, LANES), lambda s, c: (s, 0, 0))],
        out_specs=pl.BlockSpec((nl, w), lambda s, c: (s, 0)),
        out_shape=jax.ShapeDtypeStruct(qb.shape, BF16),
        scratch_shapes=[pltpu.VMEM((rows, w), BF16), pltpu.VMEM((rows, 1), F32),
                        pltpu.VMEM((rows, 1), F32), pltpu.VMEM((rows, w), F32),
                        pltpu.VMEM((nh, LANES), F32)],
        compiler_params=_cp("parallel", "arbitrary"),
        name="attn_sample",
    )(qb, cache_k, cache_v, cache_lf_t, kb, vb, lfn)


def _residual_mod(x, gate, o, g2, sh2, sc2, x_ref, h_ref):
    xn = x + gate * o
    x_ref[...] = xn
    h_ref[...] = _modulate(xn, g2, sh2, sc2).astype(h_ref.dtype)


def _merge_kernel(a_ref, b_ref, w_ref, x_ref, gate_ref, g2_ref, sh2_ref, sc2_ref, xo_ref, h_ref):
    ca = a_ref.shape[1]
    o = _dot(a_ref[...], w_ref[:ca, :]) + _dot(b_ref[...], w_ref[ca:, :])
    _residual_mod(x_ref[...], gate_ref[...], o, g2_ref[...], sh2_ref[...], sc2_ref[...],
                  xo_ref, h_ref)


def _merge(a, b, w_out, x, gate, g2, sh2, sc2, *, tm, per_row):
    t, d = x.shape
    ca = a.shape[1]
    row = lambda i: (i, 0)
    mod = _mod_spec(per_row, tm, d, False)
    return pl.pallas_call(
        _merge_kernel,
        grid=(t // tm,),
        in_specs=[pl.BlockSpec((tm, ca), row), pl.BlockSpec((tm, ca), row),
                  pl.BlockSpec(w_out.shape, lambda i: (0, 0)),
                  pl.BlockSpec((tm, d), row), mod,
                  pl.BlockSpec((1, d), lambda i: (0, 0)), mod, mod],
        out_specs=[pl.BlockSpec((tm, d), row), pl.BlockSpec((tm, d), row)],
        out_shape=[jax.ShapeDtypeStruct((t, d), F32), jax.ShapeDtypeStruct((t, d), BF16)],
        compiler_params=_cp("parallel"),
        name="merge",
    )(a, b, w_out, x, gate, g2, sh2, sc2)


def _uv_kernel(x_ref, g_ref, sh_ref, sc_ref, w_ref, lng_ref, lnb_ref, u_ref, vn_ref, h_scr):
    j = pl.program_id(1)

    @pl.when(j == 0)
    def _():
        h_scr[...] = _modulate(x_ref[...], g_ref[...], sh_ref[...], sc_ref[...]).astype(BF16)

    z = _gelu(_dot(h_scr[...], w_ref[...]))

    @pl.when(j == 0)
    def _():
        u_ref[...] = z.astype(u_ref.dtype)

    @pl.when(j == 1)
    def _():
        mu = jnp.mean(z, axis=-1, keepdims=True)
        zc = z - mu
        var = jnp.mean(zc * zc, axis=-1, keepdims=True)
        vn_ref[...] = (zc * lax.rsqrt(var + EPS) * lng_ref[...] + lnb_ref[...]).astype(vn_ref.dtype)


def _uv(x, g, shift, scale, w_uv, ln_g, ln_b, *, tm, per_row, vn_dtype):
    t, d = x.shape
    sw = w_uv.shape[1] // 2
    row = lambda i, j: (i, 0)
    const = lambda i, j: (0, 0)
    return pl.pallas_call(
        _uv_kernel,
        grid=(t // tm, 2),
        in_specs=[pl.BlockSpec((tm, d), row), pl.BlockSpec((1, d), const),
                  _mod_spec(per_row, tm, d, True), _mod_spec(per_row, tm, d, True),
                  pl.BlockSpec((d, sw), lambda i, j: (0, j)),
                  pl.BlockSpec((1, sw), const), pl.BlockSpec((1, sw), const)],
        out_specs=[pl.BlockSpec((tm, sw), row), pl.BlockSpec((tm, sw), row)],
        out_shape=[jax.ShapeDtypeStruct((t, sw), BF16), jax.ShapeDtypeStruct((t, sw), vn_dtype)],
        scratch_shapes=[pltpu.VMEM((tm, d), BF16)],
        compiler_params=_cp("parallel", "arbitrary"),
        name="sg_uv",
    )(x, g, shift, scale, w_uv, ln_g, ln_b)


def _sgate_kernel(u_ref, vn_ref, ws_ref, bs_ref, w_ref, x_ref, gate_ref, g2_ref, sh2_ref, sc2_ref,
                  xo_ref, h_ref, y_scr):
    tm, sw = u_ref.shape
    gw = sw // SG_GROUPS
    for c in range(tm // SG_CHUNK):
        rs = slice(c * SG_CHUNK, (c + 1) * SG_CHUNK)
        for g in range(SG_GROUPS):
            cs = slice(g * gw, (g + 1) * gw)
            sv = _dot(ws_ref[g], vn_ref[rs, cs].astype(BF16)) + bs_ref[:, g:g + 1]
            y_scr[rs, cs] = (u_ref[rs, cs].astype(F32) * sv).astype(BF16)
    o = _dot(y_scr[...], w_ref[...])
    _residual_mod(x_ref[...], gate_ref[...], o, g2_ref[...], sh2_ref[...], sc2_ref[...],
                  xo_ref, h_ref)


def _sgate(u, vn, ws, bs_t, w_out, x, gate, g2, sh2, sc2, *, tm, per_row):
    t, d = x.shape
    sw = u.shape[1]
    row = lambda i: (i, 0)
    mod = _mod_spec(per_row, tm, d, False)
    return pl.pallas_call(
        _sgate_kernel,
        grid=(t // tm,),
        in_specs=[pl.BlockSpec((tm, sw), row), pl.BlockSpec((tm, sw), row),
                  pl.BlockSpec(ws.shape, lambda i: (0, 0, 0)),
                  pl.BlockSpec(bs_t.shape, lambda i: (0, 0)),
                  pl.BlockSpec(w_out.shape, lambda i: (0, 0)),
                  pl.BlockSpec((tm, d), row), mod,
                  pl.BlockSpec((1, d), lambda i: (0, 0)), mod, mod],
        out_specs=[pl.BlockSpec((tm, d), row), pl.BlockSpec((tm, d), row)],
        out_shape=[jax.ShapeDtypeStruct((t, d), F32), jax.ShapeDtypeStruct((t, d), BF16)],
        scratch_shapes=[pltpu.VMEM((tm, sw), BF16)],
        compiler_params=_cp("parallel"),
        name="sg_gate",
    )(u, vn, ws, bs_t, w_out, x, gate, g2, sh2, sc2)


GROUPS = 8
ROUTE_TOKENS = GROUPS * LANES
PAIR_R1 = [0] * 16 + [a for a in range(1, 8) for _ in range(8)] + list(range(8, 16))
PAIR_R2 = list(range(16)) + [i for _ in range(1, 8) for i in range(8)] + [0] * 8
PAIR_FLAT = [a * PEER_TOPK + b for a, b in zip(PAIR_R1, PAIR_R2)]


def _extract_topk(problems, vals_refs, idx_refs):
    chains = 4

    def one_round(r, prev):
        picked = []
        for p, (ref, ids) in enumerate(problems):
            n = len(ids)
            per = -(-n // chains)
            best = []
            for c0 in range(0, n, per):
                m = ix = None
                for e in range(c0, min(c0 + per, n)):
                    x = jnp.where(prev[p] == float(ids[e]), -jnp.inf, ref[e])
                    ref[e] = x
                    if m is None:
                        m, ix = x, jnp.full(x.shape, float(ids[e]), F32)
                    else:
                        ix = jnp.where(x > m, float(ids[e]), ix)
                        m = jnp.maximum(m, x)
                best.append((m, ix))
            m, ix = best[0]
            for mc, ic in best[1:]:
                ix = jnp.where(mc > m, ic, ix)
                m = jnp.maximum(m, mc)
            vals_refs[p][r] = m
            idx_refs[p][r] = ix
            picked.append(ix)
        return tuple(picked)

    none = jnp.full((GROUPS, LANES), -1.0, F32)
    return lax.fori_loop(0, PEER_TOPK, one_round, tuple(none for _ in problems))


def _peer_route_kernel(h_ref, wq_ref, kexp_ref, r2_ref, e2_ref, c_ref, e1_ref,
                       qt_scr, s_scr, so_scr, vals_scr, idx_scr, cand_scr, cval_scr, cidx_scr,
                       t2_scr):
    hd = pl.program_id(1)

    @pl.when(hd == 0)
    def _():
        qt_scr[...] = _dot_nt(wq_ref[...], h_ref[...]).astype(BF16)

    row = pl.multiple_of(hd * 2 * PEER_KEYS, 2 * PEER_KEYS)
    for side in range(2):
        q2 = jnp.concatenate(
            [qt_scr[pl.ds(row + side * PEER_KEYS, PEER_KEYS), g * LANES:(g + 1) * LANES]
             for g in range(GROUPS)], axis=0)
        s = _dot(kexp_ref[side], q2).reshape(PEER_KEYS, GROUPS, LANES)
        s_scr[side] = s
        so_scr[side] = s

    keys = list(range(PEER_KEYS))
    _extract_topk([(s_scr.at[0], keys), (s_scr.at[1], keys)],
                  [vals_scr.at[0], vals_scr.at[1]], [idx_scr.at[0], idx_scr.at[1]])
    v1 = [vals_scr[0, r] for r in range(PEER_TOPK)]
    v2 = [vals_scr[1, r] for r in range(PEER_TOPK)]
    ncand = len(PAIR_FLAT)
    for c in range(ncand):
        cand_scr[c] = v1[PAIR_R1[c]] + v2[PAIR_R2[c]]
    (last,) = _extract_topk([(cand_scr, PAIR_FLAT)], [cval_scr], [cidx_scr])

    top = v1[0] + v2[0]
    z = jnp.zeros((GROUPS, LANES), F32)
    for r in range(PEER_TOPK):
        z = z + jnp.exp(cval_scr[r] - top)
    inv_z = 1.0 / z
    cnt = [jnp.zeros((GROUPS, LANES), F32) for _ in range(PEER_TOPK)]
    for c in range(ncand):
        picked = jnp.where(cand_scr[c] == -jnp.inf, 1.0,
                           jnp.where(last == float(PAIR_FLAT[c]), 1.0, 0.0))
        cnt[PAIR_R1[c]] = cnt[PAIR_R1[c]] + picked
    idx1 = [idx_scr[0, r] for r in range(PEER_TOPK)]
    idx2 = [idx_scr[1, r] for r in range(PEER_TOPK)]

    def per_key(k, _):
        kf = k.astype(F32)
        rank2 = jnp.full((GROUPS, LANES), RANK_NONE, F32)
        ck = jnp.zeros((GROUPS, LANES), F32)
        for r in range(PEER_TOPK):
            rank2 = jnp.where(idx2[r] == kf, float(r), rank2)
            ck = jnp.where(idx1[r] == kf, cnt[r], ck)
        base = pl.multiple_of(k * GROUPS, GROUPS)
        t2_scr[0, pl.ds(base, GROUPS), :] = rank2
        t2_scr[1, pl.ds(base, GROUPS), :] = jnp.exp(so_scr[1, k] - v2[0])
        c_ref[k] = ck
        e1_ref[k] = jnp.exp(so_scr[0, k] - v1[0]) * inv_z
        return 0

    lax.fori_loop(0, PEER_KEYS, per_key, 0)
    for g in range(GROUPS):
        cols = slice(g * LANES, (g + 1) * LANES)
        r2_ref[:, cols] = t2_scr[0, pl.ds(g, PEER_KEYS, stride=GROUPS), :].astype(BF16)
        e2_ref[:, cols] = t2_scr[1, pl.ds(g, PEER_KEYS, stride=GROUPS), :].astype(BF16)


def _peer_route(h, wq_t, kexp):
    t, d = h.shape
    nblk = t // ROUTE_TOKENS
    nq = wq_t.shape[0]
    std = jax.ShapeDtypeStruct((PEER_HEADS, PEER_KEYS, t), BF16)
    grp = jax.ShapeDtypeStruct((PEER_HEADS, PEER_KEYS, t // LANES, LANES), F32)
    std_spec = pl.BlockSpec((None, PEER_KEYS, ROUTE_TOKENS), lambda i, hd: (hd, 0, i))
    grp_spec = pl.BlockSpec((None, PEER_KEYS, GROUPS, LANES), lambda i, hd: (hd, 0, i, 0))
    tile = (GROUPS, LANES)
    return pl.pallas_call(
        _peer_route_kernel,
        grid=(nblk, PEER_HEADS),
        in_specs=[pl.BlockSpec((ROUTE_TOKENS, d), lambda i, hd: (i, 0)),
                  pl.BlockSpec(wq_t.shape, lambda i, hd: (0, 0)),
                  pl.BlockSpec((None,) + kexp.shape[1:], lambda i, hd: (hd, 0, 0, 0))],
        out_specs=[std_spec, std_spec, grp_spec, grp_spec],
        out_shape=[std, std, grp, grp],
        scratch_shapes=[pltpu.VMEM((nq, ROUTE_TOKENS), BF16),
                        pltpu.VMEM((2, PEER_KEYS) + tile, F32),
                        pltpu.VMEM((2, PEER_KEYS) + tile, F32),
                        pltpu.VMEM((2, PEER_TOPK) + tile, F32),
                        pltpu.VMEM((2, PEER_TOPK) + tile, F32),
                        pltpu.VMEM((len(PAIR_FLAT),) + tile, F32),
                        pltpu.VMEM((PEER_TOPK,) + tile, F32),
                        pltpu.VMEM((PEER_TOPK,) + tile, F32),
                        pltpu.VMEM((2, PEER_KEYS * GROUPS, LANES), F32)],
        compiler_params=_cp("parallel", "arbitrary"),
        name="peer_route",
    )(h, wq_t, kexp)


def _peer_dense_kernel(h_ref, u_ref, vt_ref, r2_ref, e2_ref, c_ref, e1_ref, o_ref):
    e = pl.program_id(1)
    tb = h_ref.shape[0]
    sub = r2_ref.shape[2]

    @pl.when(e == 0)
    def _():
        o_ref[...] = jnp.zeros_like(o_ref)

    hb = h_ref[...]
    acc = o_ref[...]
    per = PEER_SUB // PEER_KEYS
    nsub = u_ref.shape[0] // PEER_SUB
    ngrp = tb // LANES
    goff = (pl.program_id(0) % (ROUTE_TOKENS // tb)) * ngrp

    def token_row(ref, h, jj):
        return jnp.concatenate([ref[h, jj, pl.ds(goff + g, 1), :] for g in range(ngrp)], axis=1)

    def scores(q):
        return _dot_nt(u_ref[q * PEER_SUB:(q + 1) * PEER_SUB, :], hb)

    at_next = scores(0)
    for q in range(nsub):
        rows = slice(q * PEER_SUB, (q + 1) * PEER_SUB)
        at = at_next
        if q + 1 < nsub:
            at_next = scores(q + 1)
        gs = []
        for jj in range(q * per, (q + 1) * per):
            w = jnp.zeros((PEER_KEYS // sub, sub, tb), BF16)
            for h in range(PEER_HEADS):
                cb = jnp.broadcast_to(token_row(c_ref, h, jj), (sub, tb)).astype(BF16)[None]
                eb = jnp.broadcast_to(token_row(e1_ref, h, jj), (sub, tb)).astype(BF16)[None]
                w = w + jnp.where(r2_ref[h] < cb, e2_ref[h] * eb, jnp.zeros_like(w))
            lo = (jj - q * per) * PEER_KEYS
            a = _gelu(at[lo:lo + PEER_KEYS, :]).astype(BF16)
            gs.append(a * w.reshape(PEER_KEYS, tb))
        acc = acc + _dot(vt_ref[:, rows], jnp.concatenate(gs, axis=0))
    o_ref[...] = acc


def _peer_dense(h, u_all, vt_all, layer, r2, e2, c, e1, *, tb, te):
    t, d = h.shape
    ne = u_all.shape[1]
    sub = 16
    tr = r2.shape[2]
    r2 = r2.reshape(PEER_HEADS, PEER_KEYS // sub, sub, tr)
    e2 = e2.reshape(PEER_HEADS, PEER_KEYS // sub, sub, tr)
    nchunk = te // PEER_KEYS
    per_route = ROUTE_TOKENS // tb
    full = pl.BlockSpec((PEER_HEADS, PEER_KEYS // sub, sub, tb), lambda i, e: (0, 0, 0, i))
    rows = pl.BlockSpec((PEER_HEADS, nchunk, GROUPS, LANES),
                        lambda i, e: (0, e, i // per_route, 0))
    return pl.pallas_call(
        _peer_dense_kernel,
        grid=(t // tb, ne // te),
        in_specs=[pl.BlockSpec((tb, d), lambda i, e: (i, 0)),
                  pl.BlockSpec((None, te, d), lambda i, e: (layer, e, 0)),
                  pl.BlockSpec((None, d, te), lambda i, e: (layer, 0, e)),
                  full, full, rows, rows],
        out_specs=pl.BlockSpec((d, tb), lambda i, e: (0, i)),
        out_shape=jax.ShapeDtypeStruct((d, t), F32),
        compiler_params=_cp("parallel", "arbitrary"),
        name="peer_dense",
    )(h, u_all, vt_all, r2, e2, c, e1)


def _peer_out_kernel(ot_ref, x_ref, gate_ref, xo_ref):
    xo_ref[...] = x_ref[...] + gate_ref[...] * ot_ref[...].T


def _peer_out(ot, x, gate, *, tm, per_row):
    t, d = x.shape
    row = lambda i: (i, 0)
    return pl.pallas_call(
        _peer_out_kernel,
        grid=(t // tm,),
        in_specs=[pl.BlockSpec((d, tm), lambda i: (0, i)), pl.BlockSpec((tm, d), row),
                  _mod_spec(per_row, tm, d, False)],
        out_specs=pl.BlockSpec((tm, d), row),
        out_shape=jax.ShapeDtypeStruct((t, d), F32),
        compiler_params=_cp("parallel"),
        name="peer_out",
    )(ot, x, gate)


def _peer(h, x, gate, wq_t, kexp, u_all, vt_all, layer, *, tb, tm, per_row):
    t = h.shape[0]
    h_route = jnp.pad(h, ((0, -t % ROUTE_TOKENS), (0, 0)))
    r2, e2, c, e1 = _peer_route(h_route, wq_t, kexp)
    ot = _peer_dense(h, u_all, vt_all, layer, r2, e2, c, e1, tb=tb, te=PEER_EXPERT_BLOCK)
    return _peer_out(ot, x, gate, tm=tm, per_row=per_row)


def kernel(x_prompt, x_sample, c_prompt, c_sample, cache_pool, cache_k, cache_v, cache_logf, w_ada, b_ada, norm1, norm2, w_in_ab, b_f, w_pool, pool_scale, q_gain, k_gain, w_out_ab, w_uv, sg_ln_g, sg_ln_b, w_s, b_s, w_out_sg, peer_wq, peer_keys, peer_u, peer_v):
    nb, seq, d = x_prompt.shape
    db, dl, _ = x_sample.shape
    depth = w_ada.shape[0]
    past = cache_k.shape[2]
    pool_hist = cache_pool.shape[2]
    assert nb == 1 and db * dl == LANES and dl == 16 and pool_hist == dl - 1
    fox_w = FOX_HEADS * HEAD_DIM
    pool_w = w_pool.shape[1] * w_pool.shape[2]
    ts = db * dl

    c_rows = jnp.concatenate(
        [c_prompt, c_sample, jnp.zeros((16 - nb - db, d), F32)], axis=0)
    mods = _ada(c_rows, w_ada, b_ada)

    def mod_p(layer, k):
        return mods[layer, 0:1, k * d:(k + 1) * d]

    def mod_s(layer, k):
        return jnp.repeat(mods[layer, 1:1 + db, k * d:(k + 1) * d], dl, axis=0)

    xp = x_prompt.reshape(seq, d)
    xs = x_sample.reshape(ts, d)
    tm_p = 512 if seq % 512 == 0 else 256
    outs = {k: [] for k in ("pool_p", "pool_s", "kp", "ks", "vp", "vs", "lfp", "lfs", "sgv")}
    u_all = peer_u.astype(BF16)
    vt_all = jnp.swapaxes(peer_v, 1, 2).astype(BF16)

    for layer in range(depth):
        j = layer // 2
        g1 = norm1[layer].reshape(1, d)
        g2 = norm2[layer].reshape(1, d)
        if layer % 2 == 0:
            w_in = w_in_ab[j]
            w_main = w_in[:, :pool_w + 3 * fox_w].astype(BF16)
            wf_t = w_in[:, pool_w + 3 * fox_w:].T.astype(BF16)
            bfc = b_f[j].reshape(FOX_HEADS, 1)
            qg = q_gain[j].reshape(1, fox_w)
            kg = k_gain[j].reshape(1, fox_w)
            wp = w_pool[j].astype(BF16)
            psc = pool_scale[j].reshape(1, pool_w)
            wo = w_out_ab[j].astype(BF16)

            p, qb, k, kb, v, vb, lf, fc = _proj(
                xp, g1, mod_p(layer, 0), mod_p(layer, 1), w_main, wf_t, bfc, qg, kg,
                tm=tm_p, per_row=False, do_cumsum=True)
            a_out = _pool_prompt(p, wp, psc, tm=256)
            b_out = _attn_prompt(qb, kb, vb, fc, tq=tm_p)
            xp, hp = _merge(a_out, b_out, wo, xp, mod_p(layer, 2), g2, mod_p(layer, 3),
                            mod_p(layer, 4), tm=256, per_row=False)
            outs["pool_p"].append(p[-pool_hist:].reshape(nb, pool_hist, pool_w))
            outs["kp"].append(k.reshape(nb, seq, FOX_HEADS, HEAD_DIM))
            outs["vp"].append(v.reshape(nb, seq, FOX_HEADS, HEAD_DIM))
            outs["lfp"].append(lf.T.reshape(nb, seq, FOX_HEADS))

            p, qb, k, kb, v, vb, lf, _ = _proj(
                xs, g1, mod_s(layer, 0), mod_s(layer, 1), w_main, wf_t, bfc, qg, kg,
                tm=ts, per_row=True, do_cumsum=False)
            p3 = p.reshape(db, dl, pool_w)
            buf = jnp.concatenate([jnp.zeros((db, 1, pool_w), F32), cache_pool[j], p3], axis=1)
            a_out = _pool_sample(buf.reshape(db * 2 * dl, pool_w), p, wp, psc, pos0=past)
            lfn = jnp.transpose(lf.reshape(FOX_HEADS, db, dl), (1, 0, 2))
            lfn = jnp.pad(lfn, ((0, 0), (0, 0), (0, LANES - dl)))
            b_out = _attn_sample(
                qb, kb, vb, lfn, cache_k[j].reshape(db, past, fox_w),
                cache_v[j].reshape(db, past, fox_w), jnp.transpose(cache_logf[j], (0, 2, 1)),
                tk=min(past, 1024))
            xs, hs = _merge(a_out, b_out, wo, xs, mod_s(layer, 2), g2, mod_s(layer, 3),
                            mod_s(layer, 4), tm=ts, per_row=True)
            outs["pool_s"].append(jnp.concatenate([cache_pool[j], p3], axis=1)[:, -pool_hist:])
            outs["ks"].append(k.reshape(db, dl, FOX_HEADS, HEAD_DIM))
            outs["vs"].append(v.reshape(db, dl, FOX_HEADS, HEAD_DIM))
            outs["lfs"].append(lf.T.reshape(db, dl, FOX_HEADS))
        else:
            wuv = w_uv[j].astype(BF16)
            lng = sg_ln_g[j].reshape(1, -1)
            lnb = sg_ln_b[j].reshape(1, -1)
            wo = w_out_sg[j].astype(BF16)
            tri = jnp.tril(jnp.ones((SG_CHUNK, SG_CHUNK), bool))
            ws_p = jnp.where(tri, w_s[j], 0.0).astype(BF16)
            bs_p = b_s[j].T
            ws16 = jnp.where(tri[:dl, :dl], w_s[j][:, :dl, :dl], 0.0)
            ws_s = jnp.einsum("ab,gts->gatbs", jnp.eye(db, dtype=F32), ws16)
            ws_s = ws_s.reshape(SG_GROUPS, ts, ts).astype(BF16)
            bs_s = jnp.tile(b_s[j][:, :dl], (1, db)).T

            u, vn = _uv(xp, g1, mod_p(layer, 0), mod_p(layer, 1), wuv, lng, lnb,
                        tm=tm_p, per_row=False, vn_dtype=BF16)
            xp, hp = _sgate(u, vn, ws_p, bs_p, wo, xp, mod_p(layer, 2), g2, mod_p(layer, 3),
                            mod_p(layer, 4), tm=256, per_row=False)
            u, vn = _uv(xs, g1, mod_s(layer, 0), mod_s(layer, 1), wuv, lng, lnb,
                        tm=ts, per_row=True, vn_dtype=F32)
            xs, hs = _sgate(u, vn, ws_s, bs_s, wo, xs, mod_s(layer, 2), g2, mod_s(layer, 3),
                            mod_s(layer, 4), tm=ts, per_row=True)
            outs["sgv"].append(vn.reshape(db, dl, -1))

        wq_t = peer_wq[layer].T.astype(BF16)
        kexp = jnp.einsum("hskd,gj->hskgjd", peer_keys[layer].astype(BF16),
                          jnp.eye(GROUPS, dtype=BF16))
        kexp = kexp.reshape(PEER_HEADS, 2, PEER_KEYS * GROUPS, GROUPS * kexp.shape[-1])
        xp = _peer(hp, xp, mod_p(layer, 5), wq_t, kexp, u_all, vt_all, layer,
                   tb=512 if seq % 512 == 0 else 256, tm=256, per_row=False)
        xs = _peer(hs, xs, mod_s(layer, 5), wq_t, kexp, u_all, vt_all, layer,
                   tb=ts, tm=ts, per_row=True)

    st = lambda key: jnp.stack(outs[key])
    return (xp.reshape(nb, seq, d), xs.reshape(db, dl, d), st("pool_p"), st("pool_s"),
            st("kp"), st("ks"), st("vp"), st("vs"), st("lfp"), st("lfs"), st("sgv"))
```

```python
import functools

import jax
import jax.numpy as jnp
from jax import lax
from jax.experimental import pallas as pl
from jax.experimental.pallas import tpu as pltpu

F32 = jnp.float32
BF16 = jnp.bfloat16
EPS = 1e-6

LANES = 128
POOL_WINDOWS = (2, 4, 8, 16)
POOL_HALO = 128
FOX_HEADS = 8
HEAD_DIM = 128
SG_GROUPS = 8
SG_CHUNK = 128
PEER_HEADS = 8
PEER_KEYS = 128
PEER_TOPK = 16
PEER_EXPERT_BLOCK = 1024
PEER_SUB = 256
PEER_OUT_K = 256
NEG_BIG = -1e30
EXP_ZERO_BELOW = 110.0
RANK_NONE = 64.0
VMEM_LIMIT = 56 * 1024 * 1024

NT_DIMS = (((1,), (1,)), ((), ()))


def _cp(*sem):
    return pltpu.CompilerParams(dimension_semantics=sem, vmem_limit_bytes=VMEM_LIMIT)


def _dot(a, b):
    return jnp.dot(a, b, preferred_element_type=F32)


def _dot_nt(a, b):
    return lax.dot_general(a, b, NT_DIMS, preferred_element_type=F32)


def _modulate(x, g, shift, scale):
    ms = jnp.mean(x * x, axis=-1, keepdims=True)
    return x * lax.rsqrt(ms + EPS) * g * (1.0 + scale) + shift


def _gelu(x):
    cdf = 0.5 * (1.0 + jnp.tanh(0.7978845608028654 * (x + 0.044715 * (x * x * x))))
    return x * cdf


def _log_sigmoid(x):
    return -(jnp.maximum(-x, 0.0) + jnp.log(1.0 + jnp.exp(-jnp.abs(x))))


def _lane_cumsum(x):
    n = x.shape[-1]
    lane = lax.broadcasted_iota(jnp.int32, x.shape, x.ndim - 1)
    s = 1
    while s < n:
        x = x + jnp.where(lane >= s, pltpu.roll(x, s, x.ndim - 1), 0.0)
        s *= 2
    return x


def _ada_kernel(c_ref, w_ref, b_ref, o_ref):
    c = c_ref[...]
    a = (c * (1.0 / (1.0 + jnp.exp(-c)))).astype(BF16)
    o_ref[0] = _dot(a, w_ref[0].astype(BF16)) + b_ref[0]


def _ada(c_rows, w_ada, b_ada):
    nl, d, n = w_ada.shape
    r = c_rows.shape[0]
    tn = 1024
    return pl.pallas_call(
        _ada_kernel,
        grid=(nl, n // tn),
        in_specs=[pl.BlockSpec((r, d), lambda l, j: (0, 0)),
                  pl.BlockSpec((1, d, tn), lambda l, j: (l, 0, j)),
                  pl.BlockSpec((1, 1, tn), lambda l, j: (l, 0, j))],
        out_specs=pl.BlockSpec((1, r, tn), lambda l, j: (l, 0, j)),
        out_shape=jax.ShapeDtypeStruct((nl, r, n), F32),
        compiler_params=_cp("parallel", "parallel"),
        name="ada",
    )(c_rows, w_ada, b_ada.reshape(nl, 1, n))


def _mod_spec(per_row, tm, d, nj):
    if nj:
        return (pl.BlockSpec((tm, d), lambda i, j: (i, 0)) if per_row
                else pl.BlockSpec((1, d), lambda i, j: (0, 0)))
    return (pl.BlockSpec((tm, d), lambda i: (i, 0)) if per_row
            else pl.BlockSpec((1, d), lambda i: (0, 0)))


def _head_rms(z, gain):
    outs = []
    for h in range(FOX_HEADS):
        zh = z[:, h * HEAD_DIM:(h + 1) * HEAD_DIM]
        ms = jnp.mean(zh * zh, axis=-1, keepdims=True)
        outs.append(zh * lax.rsqrt(ms + EPS))
    return jnp.concatenate(outs, axis=-1) * gain


def _proj_kernel(x_ref, g_ref, sh_ref, sc_ref, w_ref, wf_ref, bf_ref, qg_ref, kg_ref,
                 p_ref, q_ref, k_ref, kb_ref, v_ref, vb_ref, lf_ref, fc_ref,
                 h_scr, carry_scr, *, do_cumsum):
    i = pl.program_id(0)
    j = pl.program_id(1)

    @pl.when(j == 0)
    def _():
        hb = _modulate(x_ref[...], g_ref[...], sh_ref[...], sc_ref[...]).astype(BF16)
        h_scr[...] = hb
        lf = _log_sigmoid(_dot_nt(wf_ref[...], hb) + bf_ref[...])
        lf_ref[...] = lf
        if do_cumsum:
            @pl.when(i == 0)
            def _():
                carry_scr[...] = jnp.zeros_like(carry_scr)
            fc = _lane_cumsum(lf) + carry_scr[:, 0:1]
            fc_ref[...] = fc
            carry_scr[...] = jnp.broadcast_to(fc[:, -1:], carry_scr.shape)
        else:
            fc_ref[...] = lf

    z = _dot(h_scr[...], w_ref[...])

    @pl.when(j == 0)
    def _():
        p_ref[...] = z

    @pl.when(j == 1)
    def _():
        q_ref[...] = (_head_rms(z, qg_ref[...]) * (HEAD_DIM ** -0.5)).astype(BF16)

    @pl.when(j == 2)
    def _():
        kn = _head_rms(z, kg_ref[...])
        k_ref[...] = kn
        kb_ref[...] = kn.astype(BF16)

    @pl.when(j == 3)
    def _():
        v_ref[...] = z
        vb_ref[...] = z.astype(BF16)


def _proj(x, g, shift, scale, w_main, wf_t, b_f, q_gain, k_gain, *, tm, per_row, do_cumsum):
    t, d = x.shape
    w4 = w_main.shape[1] // 4
    nh = wf_t.shape[0]
    row = lambda i, j: (i, 0)
    const = lambda i, j: (0, 0)
    tok_f32 = jax.ShapeDtypeStruct((t, w4), F32)
    tok_b16 = jax.ShapeDtypeStruct((t, w4), BF16)
    head_t = jax.ShapeDtypeStruct((nh, t), F32)
    return pl.pallas_call(
        functools.partial(_proj_kernel, do_cumsum=do_cumsum),
        grid=(t // tm, 4),
        in_specs=[pl.BlockSpec((tm, d), row),
                  pl.BlockSpec((1, d), const),
                  _mod_spec(per_row, tm, d, True),
                  _mod_spec(per_row, tm, d, True),
                  pl.BlockSpec((d, w4), lambda i, j: (0, j)),
                  pl.BlockSpec((nh, d), const),
                  pl.BlockSpec((nh, 1), const),
                  pl.BlockSpec((1, w4), const),
                  pl.BlockSpec((1, w4), const)],
        out_specs=[pl.BlockSpec((tm, w4), row)] * 6
                  + [pl.BlockSpec((nh, tm), lambda i, j: (0, i))] * 2,
        out_shape=[tok_f32, tok_b16, tok_f32, tok_b16, tok_f32, tok_b16, head_t, head_t],
        scratch_shapes=[pltpu.VMEM((tm, d), BF16), pltpu.VMEM((nh, LANES), F32)],
        compiler_params=_cp("arbitrary", "arbitrary"),
        name="proj",
    )(x, g, shift, scale, w_main, wf_t, b_f, q_gain, k_gain)


def _pool_kernel(ext_ref, tok_ref, w_ref, ps_ref, o_ref, *, prompt, pos0):
    i = pl.program_id(0)
    tok = tok_ref[...]
    tm = tok.shape[0]
    if prompt:
        halo = jnp.where(i == 0, 0.0, ext_ref[...])
        ext = jnp.concatenate([halo, tok], axis=0)
    else:
        ext = ext_ref[...]
    ke = ext.shape[0]
    hi = ext.astype(BF16)
    lo = (ext - hi.astype(F32)).astype(BF16)
    m_io = lax.broadcasted_iota(jnp.int32, (tm, ke), 0)
    c_io = lax.broadcasted_iota(jnp.int32, (tm, ke), 1)
    r_io = lax.broadcasted_iota(jnp.int32, (tm, 1), 0)
    if prompt:
        tgt = m_io + POOL_HALO
        pos = pos0 + i * tm + r_io
    else:
        tgt = m_io + ((m_io >> 4) << 4) + 16
        pos = pos0 + (r_io & 15)
    dlt = tgt - c_io
    gw = w_ref.shape[1]
    for gi, win in enumerate(POOL_WINDOWS):
        sl = slice(gi * gw, (gi + 1) * gw)
        band = jnp.where(dlt >= 0, jnp.where(dlt < win, 1.0, 0.0), 0.0).astype(BF16)
        wsum = _dot(band, hi[:, sl]) + _dot(band, lo[:, sl])
        cnt = jnp.minimum(win, pos + 1).astype(F32)
        dd = wsum / cnt - tok[:, sl]
        o = _dot(dd.astype(BF16), w_ref[gi]) * ps_ref[:, sl]
        o_ref[:, sl] = o.astype(o_ref.dtype)


def _pool_prompt(p, w_pool, pool_scale, *, tm):
    t, c = p.shape
    per = tm // POOL_HALO
    return pl.pallas_call(
        functools.partial(_pool_kernel, prompt=True, pos0=0),
        grid=(t // tm,),
        in_specs=[pl.BlockSpec((POOL_HALO, c), lambda i: (jnp.maximum(i * per - 1, 0), 0)),
                  pl.BlockSpec((tm, c), lambda i: (i, 0)),
                  pl.BlockSpec(w_pool.shape, lambda i: (0, 0, 0)),
                  pl.BlockSpec((1, c), lambda i: (0, 0))],
        out_specs=pl.BlockSpec((tm, c), lambda i: (i, 0)),
        out_shape=jax.ShapeDtypeStruct((t, c), BF16),
        compiler_params=_cp("parallel"),
        name="pool_prompt",
    )(p, p, w_pool, pool_scale)


def _pool_sample(buf, p, w_pool, pool_scale, *, pos0):
    t, c = p.shape
    return pl.pallas_call(
        functools.partial(_pool_kernel, prompt=False, pos0=pos0),
        grid=(1,),
        in_specs=[pl.BlockSpec(buf.shape, lambda i: (0, 0)),
                  pl.BlockSpec((t, c), lambda i: (0, 0)),
                  pl.BlockSpec(w_pool.shape, lambda i: (0, 0, 0)),
                  pl.BlockSpec((1, c), lambda i: (0, 0))],
        out_specs=pl.BlockSpec((t, c), lambda i: (0, 0)),
        out_shape=jax.ShapeDtypeStruct((t, c), BF16),
        compiler_params=_cp("arbitrary"),
        name="pool_sample",
    )(buf, p, w_pool, pool_scale)


def _softmax_step(carry, s, v):
    m, l, acc = carry
    m_new = jnp.maximum(m, jnp.max(s, axis=-1, keepdims=True))
    alpha = jnp.exp(m - m_new)
    p = jnp.exp(s - m_new)
    l = alpha * l + jnp.sum(p, axis=-1, keepdims=True)
    acc = alpha * acc + _dot(p.astype(BF16), v)
    return m_new, l, acc


def _attn_prompt_kernel(fend_ref, fstart_ref, bound_ref, q_ref, k_ref, v_ref, fk_ref, o_ref,
                        *, tq):
    h = pl.program_id(0)
    i = pl.program_id(1)
    q = q_ref[...]

    def chunk(kj, carry, masked):
        off = pl.multiple_of(kj * tq, tq)
        s = _dot_nt(q, k_ref[pl.ds(off, tq), :]) - fk_ref[:, pl.ds(off, tq)]
        if masked:
            r_io = lax.broadcasted_iota(jnp.int32, (tq, tq), 0)
            c_io = lax.broadcasted_iota(jnp.int32, (tq, tq), 1)
            s = jnp.where(c_io <= r_io, s, NEG_BIG)
        return _softmax_step(carry, s, v_ref[pl.ds(off, tq), :])

    limit = fstart_ref[h, i] + bound_ref[h]
    dead = lax.fori_loop(
        0, i, lambda kj, n: n + jnp.where(fend_ref[h, kj] > limit, 1, 0), jnp.int32(0))
    init = (jnp.full((tq, 1), NEG_BIG, F32), jnp.zeros((tq, 1), F32),
            jnp.zeros((tq, HEAD_DIM), F32))
    carry = lax.fori_loop(dead, i, lambda kj, c: chunk(kj, c, False), init)
    _, l, acc = chunk(i, carry, True)
    o_ref[...] = (acc / l).astype(o_ref.dtype)


def _attn_prompt(qb, kb, vb, fcum, score_bound, *, tq):
    t, w = qb.shape
    nh = w // HEAD_DIM
    fend = fcum[:, tq - 1::tq]
    fstart = fcum[:, ::tq]
    bound = 2.0 * score_bound + EXP_ZERO_BELOW
    smem = pl.BlockSpec(memory_space=pltpu.SMEM)
    return pl.pallas_call(
        functools.partial(_attn_prompt_kernel, tq=tq),
        grid=(nh, t // tq),
        in_specs=[smem, smem, smem,
                  pl.BlockSpec((tq, HEAD_DIM), lambda h, i: (i, h)),
                  pl.BlockSpec((t, HEAD_DIM), lambda h, i: (0, h)),
                  pl.BlockSpec((t, HEAD_DIM), lambda h, i: (0, h)),
                  pl.BlockSpec((None, 1, t), lambda h, i: (h, 0, 0))],
        out_specs=pl.BlockSpec((tq, HEAD_DIM), lambda h, i: (i, h)),
        out_shape=jax.ShapeDtypeStruct((t, w), BF16),
        compiler_params=_cp("parallel", "parallel"),
        name="attn_prompt",
    )(fend, fstart, bound, qb, kb, vb, fcum.reshape(nh, 1, t))


def _attn_sample_kernel(q_ref, ck_ref, cv_ref, clf_ref, kn_ref, vn_ref, lfn_ref, o_ref,
                        qbd_scr, m_scr, l_scr, acc_scr, fcar_scr):
    c = pl.program_id(1)
    nl = q_ref.shape[0]
    rows, width = qbd_scr.shape

    @pl.when(c == 0)
    def _():
        q = q_ref[...]
        r_io = lax.broadcasted_iota(jnp.int32, (rows, width), 0)
        c_io = lax.broadcasted_iota(jnp.int32, (rows, width), 1)
        qt = jnp.concatenate([q] * FOX_HEADS, axis=0)
        qbd_scr[...] = jnp.where((r_io // nl) == (c_io // HEAD_DIM), qt, jnp.zeros_like(qt))
        m_scr[...] = jnp.full_like(m_scr, NEG_BIG)
        l_scr[...] = jnp.zeros_like(l_scr)
        acc_scr[...] = jnp.zeros_like(acc_scr)
        fcar_scr[...] = jnp.zeros_like(fcar_scr)

    def head_rows(f):
        return jnp.concatenate(
            [jnp.broadcast_to(f[h:h + 1], (nl, f.shape[1])) for h in range(FOX_HEADS)], axis=0)

    qbd = qbd_scr[...]
    fc = _lane_cumsum(clf_ref[...]) + fcar_scr[:, 0:1]
    fcar_scr[...] = jnp.broadcast_to(fc[:, -1:], fcar_scr.shape)
    s = _dot_nt(qbd, ck_ref[...].astype(BF16)) - head_rows(fc)
    carry = _softmax_step((m_scr[...], l_scr[...], acc_scr[...]), s, cv_ref[...].astype(BF16))
    m_scr[...], l_scr[...], acc_scr[...] = carry

    @pl.when(c == pl.num_programs(1) - 1)
    def _():
        kpad = jnp.zeros((LANES - nl, width), BF16)
        kn = jnp.concatenate([kn_ref[...], kpad], axis=0)
        vn = jnp.concatenate([vn_ref[...], kpad], axis=0)
        fn = _lane_cumsum(lfn_ref[...]) + fcar_scr[:, 0:1]
        sn = _dot_nt(qbd, kn) - head_rows(fn)
        r_io = lax.broadcasted_iota(jnp.int32, sn.shape, 0)
        c_io = lax.broadcasted_iota(jnp.int32, sn.shape, 1)
        sn = jnp.where(c_io <= (r_io % nl), sn, NEG_BIG)
        _, l, acc = _softmax_step((m_scr[...], l_scr[...], acc_scr[...]), sn, vn)
        o = acc / l
        o_ref[...] = jnp.concatenate(
            [o[h * nl:(h + 1) * nl, h * HEAD_DIM:(h + 1) * HEAD_DIM] for h in range(FOX_HEADS)],
            axis=-1).astype(o_ref.dtype)


def _attn_sample(qb, kb, vb, lfn, cache_k, cache_v, cache_lf_t, *, tk):
    b, past, w = cache_k.shape
    nl = qb.shape[0] // b
    nh = w // HEAD_DIM
    rows = nh * nl
    return pl.pallas_call(
        _attn_sample_kernel,
        grid=(b, past // tk),
        in_specs=[pl.BlockSpec((nl, w), lambda s, c: (s, 0)),
                  pl.BlockSpec((None, tk, w), lambda s, c: (s, c, 0)),
                  pl.BlockSpec((None, tk, w), lambda s, c: (s, c, 0)),
                  pl.BlockSpec((None, nh, tk), lambda s, c: (s, 0, c)),
                  pl.BlockSpec((nl, w), lambda s, c: (s, 0)),
                  pl.BlockSpec((nl, w), lambda s, c: (s, 0)),
                  pl.BlockSpec((None, nh, LANES), lambda s, c: (s, 0, 0))],
        out_specs=pl.BlockSpec((nl, w), lambda s, c: (s, 0)),
        out_shape=jax.ShapeDtypeStruct(qb.shape, BF16),
        scratch_shapes=[pltpu.VMEM((rows, w), BF16), pltpu.VMEM((rows, 1), F32),
                        pltpu.VMEM((rows, 1), F32), pltpu.VMEM((rows, w), F32),
                        pltpu.VMEM((nh, LANES), F32)],
        compiler_params=_cp("parallel", "arbitrary"),
        name="attn_sample",
    )(qb, cache_k, cache_v, cache_lf_t, kb, vb, lfn)


def _residual_mod(x, gate, o, g2, sh2, sc2, x_ref, h_ref):
    xn = x + gate * o
    x_ref[...] = xn
    h_ref[...] = _modulate(xn, g2, sh2, sc2).astype(h_ref.dtype)


def _merge_kernel(a_ref, b_ref, w_ref, x_ref, gate_ref, g2_ref, sh2_ref, sc2_ref, xo_ref, h_ref):
    ca = a_ref.shape[1]
    o = _dot(a_ref[...], w_ref[:ca, :]) + _dot(b_ref[...], w_ref[ca:, :])
    _residual_mod(x_ref[...], gate_ref[...], o, g2_ref[...], sh2_ref[...], sc2_ref[...],
                  xo_ref, h_ref)


def _merge(a, b, w_out, x, gate, g2, sh2, sc2, *, tm, per_row):
    t, d = x.shape
    ca = a.shape[1]
    row = lambda i: (i, 0)
    mod = _mod_spec(per_row, tm, d, False)
    return pl.pallas_call(
        _merge_kernel,
        grid=(t // tm,),
        in_specs=[pl.BlockSpec((tm, ca), row), pl.BlockSpec((tm, ca), row),
                  pl.BlockSpec(w_out.shape, lambda i: (0, 0)),
                  pl.BlockSpec((tm, d), row), mod,
                  pl.BlockSpec((1, d), lambda i: (0, 0)), mod, mod],
        out_specs=[pl.BlockSpec((tm, d), row), pl.BlockSpec((tm, d), row)],
        out_shape=[jax.ShapeDtypeStruct((t, d), F32), jax.ShapeDtypeStruct((t, d), BF16)],
        compiler_params=_cp("parallel"),
        name="merge",
    )(a, b, w_out, x, gate, g2, sh2, sc2)


def _uv_kernel(x_ref, g_ref, sh_ref, sc_ref, w_ref, lng_ref, lnb_ref, u_ref, vn_ref, h_scr):
    j = pl.program_id(1)

    @pl.when(j == 0)
    def _():
        h_scr[...] = _modulate(x_ref[...], g_ref[...], sh_ref[...], sc_ref[...]).astype(BF16)

    z = _gelu(_dot(h_scr[...], w_ref[...]))

    @pl.when(j == 0)
    def _():
        u_ref[...] = z.astype(u_ref.dtype)

    @pl.when(j == 1)
    def _():
        mu = jnp.mean(z, axis=-1, keepdims=True)
        zc = z - mu
        var = jnp.mean(zc * zc, axis=-1, keepdims=True)
        vn_ref[...] = (zc * lax.rsqrt(var + EPS) * lng_ref[...] + lnb_ref[...]).astype(vn_ref.dtype)


def _uv(x, g, shift, scale, w_uv, ln_g, ln_b, *, tm, per_row, vn_dtype):
    t, d = x.shape
    sw = w_uv.shape[1] // 2
    row = lambda i, j: (i, 0)
    const = lambda i, j: (0, 0)
    return pl.pallas_call(
        _uv_kernel,
        grid=(t // tm, 2),
        in_specs=[pl.BlockSpec((tm, d), row), pl.BlockSpec((1, d), const),
                  _mod_spec(per_row, tm, d, True), _mod_spec(per_row, tm, d, True),
                  pl.BlockSpec((d, sw), lambda i, j: (0, j)),
                  pl.BlockSpec((1, sw), const), pl.BlockSpec((1, sw), const)],
        out_specs=[pl.BlockSpec((tm, sw), row), pl.BlockSpec((tm, sw), row)],
        out_shape=[jax.ShapeDtypeStruct((t, sw), BF16), jax.ShapeDtypeStruct((t, sw), vn_dtype)],
        scratch_shapes=[pltpu.VMEM((tm, d), BF16)],
        compiler_params=_cp("parallel", "arbitrary"),
        name="sg_uv",
    )(x, g, shift, scale, w_uv, ln_g, ln_b)


def _sgate_kernel(u_ref, vn_ref, ws_ref, bs_ref, w_ref, x_ref, gate_ref, g2_ref, sh2_ref, sc2_ref,
                  xo_ref, h_ref, y_scr):
    tm, sw = u_ref.shape
    gw = sw // SG_GROUPS
    for c in range(tm // SG_CHUNK):
        rs = slice(c * SG_CHUNK, (c + 1) * SG_CHUNK)
        for g in range(SG_GROUPS):
            cs = slice(g * gw, (g + 1) * gw)
            sv = _dot(ws_ref[g], vn_ref[rs, cs].astype(BF16)) + bs_ref[:, g:g + 1]
            y_scr[rs, cs] = (u_ref[rs, cs].astype(F32) * sv).astype(BF16)
    o = _dot(y_scr[...], w_ref[...])
    _residual_mod(x_ref[...], gate_ref[...], o, g2_ref[...], sh2_ref[...], sc2_ref[...],
                  xo_ref, h_ref)


def _sgate(u, vn, ws, bs_t, w_out, x, gate, g2, sh2, sc2, *, tm, per_row):
    t, d = x.shape
    sw = u.shape[1]
    row = lambda i: (i, 0)
    mod = _mod_spec(per_row, tm, d, False)
    return pl.pallas_call(
        _sgate_kernel,
        grid=(t // tm,),
        in_specs=[pl.BlockSpec((tm, sw), row), pl.BlockSpec((tm, sw), row),
                  pl.BlockSpec(ws.shape, lambda i: (0, 0, 0)),
                  pl.BlockSpec(bs_t.shape, lambda i: (0, 0)),
                  pl.BlockSpec(w_out.shape, lambda i: (0, 0)),
                  pl.BlockSpec((tm, d), row), mod,
                  pl.BlockSpec((1, d), lambda i: (0, 0)), mod, mod],
        out_specs=[pl.BlockSpec((tm, d), row), pl.BlockSpec((tm, d), row)],
        out_shape=[jax.ShapeDtypeStruct((t, d), F32), jax.ShapeDtypeStruct((t, d), BF16)],
        scratch_shapes=[pltpu.VMEM((tm, sw), BF16)],
        compiler_params=_cp("parallel"),
        name="sg_gate",
    )(u, vn, ws, bs_t, w_out, x, gate, g2, sh2, sc2)


GROUPS = 8
ROUTE_TOKENS = GROUPS * LANES
PAIR_R1 = [0] * 16 + [a for a in range(1, 8) for _ in range(8)] + list(range(8, 16))
PAIR_R2 = list(range(16)) + [i for _ in range(1, 8) for i in range(8)] + [0] * 8
PAIR_FLAT = [a * PEER_TOPK + b for a, b in zip(PAIR_R1, PAIR_R2)]


def _extract_topk(problems, vals_refs, idx_refs):
    chains = 4

    def one_round(r, prev):
        picked = []
        for p, (ref, ids) in enumerate(problems):
            n = len(ids)
            per = -(-n // chains)
            best = []
            for c0 in range(0, n, per):
                m = ix = None
                for e in range(c0, min(c0 + per, n)):
                    x = jnp.where(prev[p] == float(ids[e]), -jnp.inf, ref[e])
                    ref[e] = x
                    if m is None:
                        m, ix = x, jnp.full(x.shape, float(ids[e]), F32)
                    else:
                        ix = jnp.where(x > m, float(ids[e]), ix)
                        m = jnp.maximum(m, x)
                best.append((m, ix))
            m, ix = best[0]
            for mc, ic in best[1:]:
                ix = jnp.where(mc > m, ic, ix)
                m = jnp.maximum(m, mc)
            vals_refs[p][r] = m
            idx_refs[p][r] = ix
            picked.append(ix)
        return tuple(picked)

    none = jnp.full((GROUPS, LANES), -1.0, F32)
    return lax.fori_loop(0, PEER_TOPK, one_round, tuple(none for _ in problems))


def _peer_route_kernel(h_ref, wq_ref, kexp_ref, r2_ref, e2_ref, c_ref, e1_ref,
                       qt_scr, s_scr, so_scr, vals_scr, idx_scr, cand_scr, cval_scr, cidx_scr,
                       t2_scr):
    hd = pl.program_id(1)

    @pl.when(hd == 0)
    def _():
        qt_scr[...] = _dot_nt(wq_ref[...], h_ref[...]).astype(BF16)

    row = pl.multiple_of(hd * 2 * PEER_KEYS, 2 * PEER_KEYS)
    for side in range(2):
        q2 = jnp.concatenate(
            [qt_scr[pl.ds(row + side * PEER_KEYS, PEER_KEYS), g * LANES:(g + 1) * LANES]
             for g in range(GROUPS)], axis=0)
        s = _dot(kexp_ref[side], q2).reshape(PEER_KEYS, GROUPS, LANES)
        s_scr[side] = s
        so_scr[side] = s

    keys = list(range(PEER_KEYS))
    _extract_topk([(s_scr.at[0], keys), (s_scr.at[1], keys)],
                  [vals_scr.at[0], vals_scr.at[1]], [idx_scr.at[0], idx_scr.at[1]])
    v1 = [vals_scr[0, r] for r in range(PEER_TOPK)]
    v2 = [vals_scr[1, r] for r in range(PEER_TOPK)]
    ncand = len(PAIR_FLAT)
    for c in range(ncand):
        cand_scr[c] = v1[PAIR_R1[c]] + v2[PAIR_R2[c]]
    (last,) = _extract_topk([(cand_scr, PAIR_FLAT)], [cval_scr], [cidx_scr])

    top = v1[0] + v2[0]
    z = jnp.zeros((GROUPS, LANES), F32)
    for r in range(PEER_TOPK):
        z = z + jnp.exp(cval_scr[r] - top)
    inv_z = 1.0 / z
    cnt = [jnp.zeros((GROUPS, LANES), F32) for _ in range(PEER_TOPK)]
    for c in range(ncand):
        picked = jnp.where(cand_scr[c] == -jnp.inf, 1.0,
                           jnp.where(last == float(PAIR_FLAT[c]), 1.0, 0.0))
        cnt[PAIR_R1[c]] = cnt[PAIR_R1[c]] + picked
    idx1 = [idx_scr[0, r] for r in range(PEER_TOPK)]
    idx2 = [idx_scr[1, r] for r in range(PEER_TOPK)]

    def per_key(k, _):
        kf = lax.convert_element_type(k, F32)
        rank2 = jnp.full((GROUPS, LANES), RANK_NONE, F32)
        ck = jnp.zeros((GROUPS, LANES), F32)
        for r in range(PEER_TOPK):
            rank2 = jnp.where(idx2[r] == kf, float(r), rank2)
            ck = jnp.where(idx1[r] == kf, cnt[r], ck)
        base = pl.multiple_of(k * GROUPS, GROUPS)
        t2_scr[0, pl.ds(base, GROUPS), :] = rank2
        t2_scr[1, pl.ds(base, GROUPS), :] = jnp.exp(so_scr[1, k] - v2[0])
        c_ref[k] = ck
        e1_ref[k] = jnp.exp(so_scr[0, k] - v1[0]) * inv_z
        return 0

    lax.fori_loop(0, PEER_KEYS, per_key, 0)
    for g in range(GROUPS):
        cols = slice(g * LANES, (g + 1) * LANES)
        r2_ref[:, cols] = t2_scr[0, pl.ds(g, PEER_KEYS, stride=GROUPS), :].astype(BF16)
        e2_ref[:, cols] = t2_scr[1, pl.ds(g, PEER_KEYS, stride=GROUPS), :].astype(BF16)


def _peer_route(h, wq_t, kexp):
    t, d = h.shape
    nblk = t // ROUTE_TOKENS
    nq = wq_t.shape[0]
    std = jax.ShapeDtypeStruct((PEER_HEADS, PEER_KEYS, t), BF16)
    grp = jax.ShapeDtypeStruct((PEER_HEADS, PEER_KEYS, t // LANES, LANES), F32)
    std_spec = pl.BlockSpec((None, PEER_KEYS, ROUTE_TOKENS), lambda i, hd: (hd, 0, i))
    grp_spec = pl.BlockSpec((None, PEER_KEYS, GROUPS, LANES), lambda i, hd: (hd, 0, i, 0))
    tile = (GROUPS, LANES)
    return pl.pallas_call(
        _peer_route_kernel,
        grid=(nblk, PEER_HEADS),
        in_specs=[pl.BlockSpec((ROUTE_TOKENS, d), lambda i, hd: (i, 0)),
                  pl.BlockSpec(wq_t.shape, lambda i, hd: (0, 0)),
                  pl.BlockSpec((None,) + kexp.shape[1:], lambda i, hd: (hd, 0, 0, 0))],
        out_specs=[std_spec, std_spec, grp_spec, grp_spec],
        out_shape=[std, std, grp, grp],
        scratch_shapes=[pltpu.VMEM((nq, ROUTE_TOKENS), BF16),
                        pltpu.VMEM((2, PEER_KEYS) + tile, F32),
                        pltpu.VMEM((2, PEER_KEYS) + tile, F32),
                        pltpu.VMEM((2, PEER_TOPK) + tile, F32),
                        pltpu.VMEM((2, PEER_TOPK) + tile, F32),
                        pltpu.VMEM((len(PAIR_FLAT),) + tile, F32),
                        pltpu.VMEM((PEER_TOPK,) + tile, F32),
                        pltpu.VMEM((PEER_TOPK,) + tile, F32),
                        pltpu.VMEM((2, PEER_KEYS * GROUPS, LANES), F32)],
        compiler_params=_cp("parallel", "arbitrary"),
        name="peer_route",
    )(h, wq_t, kexp)


def _peer_dense_kernel(h_ref, u_ref, vt_ref, r2_ref, e2_ref, c_ref, e1_ref, o_ref):
    e = pl.program_id(1)
    tb = h_ref.shape[0]
    sub = r2_ref.shape[2]

    @pl.when(e == 0)
    def _():
        o_ref[...] = jnp.zeros_like(o_ref)

    hb = h_ref[...]
    acc = o_ref[...]
    per = PEER_SUB // PEER_KEYS
    nsub = u_ref.shape[0] // PEER_SUB
    ngrp = tb // LANES
    goff = (pl.program_id(0) % (ROUTE_TOKENS // tb)) * ngrp

    def token_rows(ref, h, jj):
        row = jnp.concatenate(
            [ref[h, jj, pl.ds(goff + g, 1), :] for g in range(ngrp)], axis=1)
        return jnp.broadcast_to(row, (sub, tb)).astype(BF16)[None]

    def scores(q):
        return _dot_nt(u_ref[q * PEER_SUB:(q + 1) * PEER_SUB, :], hb)

    at_next = scores(0)
    gs = []
    for q in range(nsub):
        at = at_next
        if q + 1 < nsub:
            at_next = scores(q + 1)
        for jj in range(q * per, (q + 1) * per):
            w = jnp.zeros((PEER_KEYS // sub, sub, tb), BF16)
            for h in range(PEER_HEADS):
                picked = jnp.where(r2_ref[h] < token_rows(c_ref, h, jj), e2_ref[h],
                                   jnp.zeros_like(w))
                w = w + picked * token_rows(e1_ref, h, jj)
            lo = (jj - q * per) * PEER_KEYS
            a = _gelu(at[lo:lo + PEER_KEYS, :]).astype(BF16)
            gs.append(a * w.reshape(PEER_KEYS, tb))
        if len(gs) * PEER_KEYS == PEER_OUT_K:
            lo = (q + 1) * PEER_SUB - PEER_OUT_K
            acc = acc + _dot(vt_ref[:, lo:lo + PEER_OUT_K], jnp.concatenate(gs, axis=0))
            gs = []
    o_ref[...] = acc


def _peer_dense(h, u_all, vt_all, layer, r2, e2, c, e1, *, tb, te):
    t, d = h.shape
    ne = u_all.shape[1]
    sub = 16
    tr = r2.shape[2]
    r2 = r2.reshape(PEER_HEADS, PEER_KEYS // sub, sub, tr)
    e2 = e2.reshape(PEER_HEADS, PEER_KEYS // sub, sub, tr)
    nchunk = te // PEER_KEYS
    per_route = ROUTE_TOKENS // tb
    full = pl.BlockSpec((PEER_HEADS, PEER_KEYS // sub, sub, tb), lambda i, e: (0, 0, 0, i))
    rows = pl.BlockSpec((PEER_HEADS, nchunk, GROUPS, LANES),
                        lambda i, e: (0, e, i // per_route, 0))
    return pl.pallas_call(
        _peer_dense_kernel,
        grid=(t // tb, ne // te),
        in_specs=[pl.BlockSpec((tb, d), lambda i, e: (i, 0)),
                  pl.BlockSpec((None, te, d), lambda i, e: (layer, e, 0)),
                  pl.BlockSpec((None, d, te), lambda i, e: (layer, 0, e)),
                  full, full, rows, rows],
        out_specs=pl.BlockSpec((d, tb), lambda i, e: (0, i)),
        out_shape=jax.ShapeDtypeStruct((d, t), F32),
        compiler_params=_cp("parallel", "arbitrary"),
        name="peer_dense",
    )(h, u_all, vt_all, r2, e2, c, e1)


def _peer_out_kernel(ot_ref, x_ref, gate_ref, xo_ref):
    xo_ref[...] = x_ref[...] + gate_ref[...] * ot_ref[...].T


def _peer_out(ot, x, gate, *, tm, per_row):
    t, d = x.shape
    row = lambda i: (i, 0)
    return pl.pallas_call(
        _peer_out_kernel,
        grid=(t // tm,),
        in_specs=[pl.BlockSpec((d, tm), lambda i: (0, i)), pl.BlockSpec((tm, d), row),
                  _mod_spec(per_row, tm, d, False)],
        out_specs=pl.BlockSpec((tm, d), row),
        out_shape=jax.ShapeDtypeStruct((t, d), F32),
        compiler_params=_cp("parallel"),
        name="peer_out",
    )(ot, x, gate)


def _peer(h, x, gate, wq_t, kexp, u_all, vt_all, layer, *, tb, tm, per_row):
    t = h.shape[0]
    h_route = jnp.pad(h, ((0, -t % ROUTE_TOKENS), (0, 0)))
    r2, e2, c, e1 = _peer_route(h_route, wq_t, kexp)
    ot = _peer_dense(h, u_all, vt_all, layer, r2, e2, c, e1, tb=tb, te=PEER_EXPERT_BLOCK)
    return _peer_out(ot, x, gate, tm=tm, per_row=per_row)


def kernel(x_prompt, x_sample, c_prompt, c_sample, cache_pool, cache_k, cache_v, cache_logf, w_ada, b_ada, norm1, norm2, w_in_ab, b_f, w_pool, pool_scale, q_gain, k_gain, w_out_ab, w_uv, sg_ln_g, sg_ln_b, w_s, b_s, w_out_sg, peer_wq, peer_keys, peer_u, peer_v):
    nb, seq, d = x_prompt.shape
    db, dl, _ = x_sample.shape
    depth = w_ada.shape[0]
    past = cache_k.shape[2]
    pool_hist = cache_pool.shape[2]
    assert nb == 1 and db * dl == LANES and dl == 16 and pool_hist == dl - 1
    fox_w = FOX_HEADS * HEAD_DIM
    pool_w = w_pool.shape[1] * w_pool.shape[2]
    ts = db * dl

    c_rows = jnp.concatenate(
        [c_prompt, c_sample, jnp.zeros((16 - nb - db, d), F32)], axis=0)
    mods = _ada(c_rows, w_ada, b_ada)

    def mod_p(layer, k):
        return mods[layer, 0:1, k * d:(k + 1) * d]

    def mod_s(layer, k):
        return jnp.repeat(mods[layer, 1:1 + db, k * d:(k + 1) * d], dl, axis=0)

    xp = x_prompt.reshape(seq, d)
    xs = x_sample.reshape(ts, d)
    tm_p = 512 if seq % 512 == 0 else 256
    outs = {k: [] for k in ("pool_p", "pool_s", "kp", "ks", "vp", "vs", "lfp", "lfs", "sgv")}
    u_all = peer_u.astype(BF16)
    vt_all = jnp.swapaxes(peer_v, 1, 2).astype(BF16)

    for layer in range(depth):
        j = layer // 2
        g1 = norm1[layer].reshape(1, d)
        g2 = norm2[layer].reshape(1, d)
        if layer % 2 == 0:
            w_in = w_in_ab[j]
            w_main = w_in[:, :pool_w + 3 * fox_w].astype(BF16)
            wf_t = w_in[:, pool_w + 3 * fox_w:].T.astype(BF16)
            bfc = b_f[j].reshape(FOX_HEADS, 1)
            qg = q_gain[j].reshape(1, fox_w)
            kg = k_gain[j].reshape(1, fox_w)
            wp = w_pool[j].astype(BF16)
            psc = pool_scale[j].reshape(1, pool_w)
            wo = w_out_ab[j].astype(BF16)

            p, qb, k, kb, v, vb, lf, fc = _proj(
                xp, g1, mod_p(layer, 0), mod_p(layer, 1), w_main, wf_t, bfc, qg, kg,
                tm=tm_p, per_row=False, do_cumsum=True)
            a_out = _pool_prompt(p, wp, psc, tm=256)
            score_bound = 1.02 * (HEAD_DIM ** 0.5) * (jnp.max(jnp.abs(q_gain[j]), axis=-1)
                                                      * jnp.max(jnp.abs(k_gain[j]), axis=-1))
            b_out = _attn_prompt(qb, kb, vb, fc, score_bound, tq=tm_p)
            xp, hp = _merge(a_out, b_out, wo, xp, mod_p(layer, 2), g2, mod_p(layer, 3),
                            mod_p(layer, 4), tm=256, per_row=False)
            outs["pool_p"].append(p[-pool_hist:].reshape(nb, pool_hist, pool_w))
            outs["kp"].append(k.reshape(nb, seq, FOX_HEADS, HEAD_DIM))
            outs["vp"].append(v.reshape(nb, seq, FOX_HEADS, HEAD_DIM))
            outs["lfp"].append(lf.T.reshape(nb, seq, FOX_HEADS))

            p, qb, k, kb, v, vb, lf, _ = _proj(
                xs, g1, mod_s(layer, 0), mod_s(layer, 1), w_main, wf_t, bfc, qg, kg,
                tm=ts, per_row=True, do_cumsum=False)
            p3 = p.reshape(db, dl, pool_w)
            buf = jnp.concatenate([jnp.zeros((db, 1, pool_w), F32), cache_pool[j], p3], axis=1)
            a_out = _pool_sample(buf.reshape(db * 2 * dl, pool_w), p, wp, psc, pos0=past)
            lfn = jnp.transpose(lf.reshape(FOX_HEADS, db, dl), (1, 0, 2))
            lfn = jnp.pad(lfn, ((0, 0), (0, 0), (0, LANES - dl)))
            b_out = _attn_sample(
                qb, kb, vb, lfn, cache_k[j].reshape(db, past, fox_w),
                cache_v[j].reshape(db, past, fox_w), jnp.transpose(cache_logf[j], (0, 2, 1)),
                tk=min(past, 1024))
            xs, hs = _merge(a_out, b_out, wo, xs, mod_s(layer, 2), g2, mod_s(layer, 3),
                            mod_s(layer, 4), tm=ts, per_row=True)
            outs["pool_s"].append(jnp.concatenate([cache_pool[j], p3], axis=1)[:, -pool_hist:])
            outs["ks"].append(k.reshape(db, dl, FOX_HEADS, HEAD_DIM))
            outs["vs"].append(v.reshape(db, dl, FOX_HEADS, HEAD_DIM))
            outs["lfs"].append(lf.T.reshape(db, dl, FOX_HEADS))
        else:
            wuv = w_uv[j].astype(BF16)
            lng = sg_ln_g[j].reshape(1, -1)
            lnb = sg_ln_b[j].reshape(1, -1)
            wo = w_out_sg[j].astype(BF16)
            tri = jnp.tril(jnp.ones((SG_CHUNK, SG_CHUNK), bool))
            ws_p = jnp.where(tri, w_s[j], 0.0).astype(BF16)
            bs_p = b_s[j].T
            ws16 = jnp.where(tri[:dl, :dl], w_s[j][:, :dl, :dl], 0.0)
            ws_s = jnp.einsum("ab,gts->gatbs", jnp.eye(db, dtype=F32), ws16)
            ws_s = ws_s.reshape(SG_GROUPS, ts, ts).astype(BF16)
            bs_s = jnp.tile(b_s[j][:, :dl], (1, db)).T

            u, vn = _uv(xp, g1, mod_p(layer, 0), mod_p(layer, 1), wuv, lng, lnb,
                        tm=tm_p, per_row=False, vn_dtype=BF16)
            xp, hp = _sgate(u, vn, ws_p, bs_p, wo, xp, mod_p(layer, 2), g2, mod_p(layer, 3),
                            mod_p(layer, 4), tm=256, per_row=False)
            u, vn = _uv(xs, g1, mod_s(layer, 0), mod_s(layer, 1), wuv, lng, lnb,
                        tm=ts, per_row=True, vn_dtype=F32)
            xs, hs = _sgate(u, vn, ws_s, bs_s, wo, xs, mod_s(layer, 2), g2, mod_s(layer, 3),
                            mod_s(layer, 4), tm=ts, per_row=True)
            outs["sgv"].append(vn.reshape(db, dl, -1))

        wq_t = peer_wq[layer].T.astype(BF16)
        kexp = jnp.einsum("hskd,gj->hskgjd", peer_keys[layer].astype(BF16),
                          jnp.eye(GROUPS, dtype=BF16))
        kexp = kexp.reshape(PEER_HEADS, 2, PEER_KEYS * GROUPS, GROUPS * kexp.shape[-1])
        xp = _peer(hp, xp, mod_p(layer, 5), wq_t, kexp, u_all, vt_all, layer,
                   tb=512 if seq % 512 == 0 else 256, tm=256, per_row=False)
        xs = _peer(hs, xs, mod_s(layer, 5), wq_t, kexp, u_all, vt_all, layer,
                   tb=ts, tm=ts, per_row=True)

    st = lambda key: jnp.stack(outs[key])
    return (xp.reshape(nb, seq, d), xs.reshape(db, dl, d), st("pool_p"), st("pool_s"),
            st("kp"), st("ks"), st("vp"), st("vs"), st("lfp"), st("lfs"), st("sgv"))
```

```python
import functools

import jax
import jax.numpy as jnp
from jax import lax
from jax.experimental import pallas as pl
from jax.experimental.pallas import tpu as pltpu

F32 = jnp.float32
BF16 = jnp.bfloat16
EPS = 1e-6

LANES = 128
POOL_WINDOWS = (2, 4, 8, 16)
POOL_HALO = 128
FOX_HEADS = 8
HEAD_DIM = 128
SG_GROUPS = 8
SG_CHUNK = 128
PEER_HEADS = 8
PEER_KEYS = 128
PEER_TOPK = 16
PEER_TOKEN_BLOCK = 1024
PEER_EXPERT_BLOCK = 1024
PEER_SUB = 256
PEER_UNIT_TOKENS = 1024
NEG_BIG = -1e30
EXP_ZERO_BELOW = 110.0
RANK_NONE = 64.0
VMEM_LIMIT = 56 * 1024 * 1024

NT_DIMS = (((1,), (1,)), ((), ()))


def _cp(*sem):
    return pltpu.CompilerParams(dimension_semantics=sem, vmem_limit_bytes=VMEM_LIMIT)


def _dot(a, b):
    return jnp.dot(a, b, preferred_element_type=F32)


def _dot_nt(a, b):
    return lax.dot_general(a, b, NT_DIMS, preferred_element_type=F32)


def _modulate(x, g, shift, scale):
    ms = jnp.mean(x * x, axis=-1, keepdims=True)
    return x * lax.rsqrt(ms + EPS) * g * (1.0 + scale) + shift


def _gelu(x):
    c = 0.7978845608028654
    half = 0.5 * x
    return half + half * jnp.tanh(x * (c + (c * 0.044715) * (x * x)))


def _log_sigmoid(x):
    return -(jnp.maximum(-x, 0.0) + jnp.log(1.0 + jnp.exp(-jnp.abs(x))))


def _lane_cumsum(x):
    n = x.shape[-1]
    lane = lax.broadcasted_iota(jnp.int32, x.shape, x.ndim - 1)
    s = 1
    while s < n:
        x = x + jnp.where(lane >= s, pltpu.roll(x, s, x.ndim - 1), 0.0)
        s *= 2
    return x


def _ada_kernel(c_ref, w_ref, b_ref, o_ref):
    c = c_ref[...]
    a = (c * (1.0 / (1.0 + jnp.exp(-c)))).astype(BF16)
    o_ref[0] = _dot(a, w_ref[0].astype(BF16)) + b_ref[0]


def _ada(c_rows, w_ada, b_ada):
    nl, d, n = w_ada.shape
    r = c_rows.shape[0]
    tn = 1024
    return pl.pallas_call(
        _ada_kernel,
        grid=(nl, n // tn),
        in_specs=[pl.BlockSpec((r, d), lambda l, j: (0, 0)),
                  pl.BlockSpec((1, d, tn), lambda l, j: (l, 0, j)),
                  pl.BlockSpec((1, 1, tn), lambda l, j: (l, 0, j))],
        out_specs=pl.BlockSpec((1, r, tn), lambda l, j: (l, 0, j)),
        out_shape=jax.ShapeDtypeStruct((nl, r, n), F32),
        compiler_params=_cp("parallel", "parallel"),
        name="ada",
    )(c_rows, w_ada, b_ada.reshape(nl, 1, n))


def _mod_spec(per_row, tm, d, nj):
    if nj:
        return (pl.BlockSpec((tm, d), lambda i, j: (i, 0)) if per_row
                else pl.BlockSpec((1, d), lambda i, j: (0, 0)))
    return (pl.BlockSpec((tm, d), lambda i: (i, 0)) if per_row
            else pl.BlockSpec((1, d), lambda i: (0, 0)))


def _head_rms(z, gain):
    outs = []
    for h in range(FOX_HEADS):
        zh = z[:, h * HEAD_DIM:(h + 1) * HEAD_DIM]
        ms = jnp.mean(zh * zh, axis=-1, keepdims=True)
        outs.append(zh * lax.rsqrt(ms + EPS))
    return jnp.concatenate(outs, axis=-1) * gain


def _proj_kernel(x_ref, g_ref, sh_ref, sc_ref, w_ref, wf_ref, bf_ref, qg_ref, kg_ref,
                 p_ref, q_ref, k_ref, kb_ref, v_ref, vb_ref, lf_ref, fc_ref,
                 h_scr, carry_scr, *, do_cumsum):
    i = pl.program_id(0)
    j = pl.program_id(1)

    @pl.when(j == 0)
    def _():
        hb = _modulate(x_ref[...], g_ref[...], sh_ref[...], sc_ref[...]).astype(BF16)
        h_scr[...] = hb
        lf = _log_sigmoid(_dot_nt(wf_ref[...], hb) + bf_ref[...])
        lf_ref[...] = lf
        if do_cumsum:
            @pl.when(i == 0)
            def _():
                carry_scr[...] = jnp.zeros_like(carry_scr)
            fc = _lane_cumsum(lf) + carry_scr[:, 0:1]
            fc_ref[...] = fc
            carry_scr[...] = jnp.broadcast_to(fc[:, -1:], carry_scr.shape)
        else:
            fc_ref[...] = lf

    z = _dot(h_scr[...], w_ref[...])

    @pl.when(j == 0)
    def _():
        p_ref[...] = z

    @pl.when(j == 1)
    def _():
        q_ref[...] = (_head_rms(z, qg_ref[...]) * (HEAD_DIM ** -0.5)).astype(BF16)

    @pl.when(j == 2)
    def _():
        kn = _head_rms(z, kg_ref[...])
        k_ref[...] = kn
        kb_ref[...] = kn.astype(BF16)

    @pl.when(j == 3)
    def _():
        v_ref[...] = z
        vb_ref[...] = z.astype(BF16)


def _proj(x, g, shift, scale, w_main, wf_t, b_f, q_gain, k_gain, *, tm, per_row, do_cumsum):
    t, d = x.shape
    w4 = w_main.shape[1] // 4
    nh = wf_t.shape[0]
    row = lambda i, j: (i, 0)
    const = lambda i, j: (0, 0)
    tok_f32 = jax.ShapeDtypeStruct((t, w4), F32)
    tok_b16 = jax.ShapeDtypeStruct((t, w4), BF16)
    head_t = jax.ShapeDtypeStruct((nh, t), F32)
    return pl.pallas_call(
        functools.partial(_proj_kernel, do_cumsum=do_cumsum),
        grid=(t // tm, 4),
        in_specs=[pl.BlockSpec((tm, d), row),
                  pl.BlockSpec((1, d), const),
                  _mod_spec(per_row, tm, d, True),
                  _mod_spec(per_row, tm, d, True),
                  pl.BlockSpec((d, w4), lambda i, j: (0, j)),
                  pl.BlockSpec((nh, d), const),
                  pl.BlockSpec((nh, 1), const),
                  pl.BlockSpec((1, w4), const),
                  pl.BlockSpec((1, w4), const)],
        out_specs=[pl.BlockSpec((tm, w4), row)] * 6
                  + [pl.BlockSpec((nh, tm), lambda i, j: (0, i))] * 2,
        out_shape=[tok_f32, tok_b16, tok_f32, tok_b16, tok_f32, tok_b16, head_t, head_t],
        scratch_shapes=[pltpu.VMEM((tm, d), BF16), pltpu.VMEM((nh, LANES), F32)],
        compiler_params=_cp("arbitrary", "arbitrary"),
        name="proj",
    )(x, g, shift, scale, w_main, wf_t, b_f, q_gain, k_gain)


def _pool_kernel(ext_ref, tok_ref, w_ref, ps_ref, o_ref, *, prompt, pos0):
    i = pl.program_id(0)
    tok = tok_ref[...]
    tm = tok.shape[0]
    if prompt:
        halo = jnp.where(i == 0, 0.0, ext_ref[...])
        ext = jnp.concatenate([halo, tok], axis=0)
    else:
        ext = ext_ref[...]
    ke = ext.shape[0]
    hi = ext.astype(BF16)
    lo = (ext - hi.astype(F32)).astype(BF16)
    m_io = lax.broadcasted_iota(jnp.int32, (tm, ke), 0)
    c_io = lax.broadcasted_iota(jnp.int32, (tm, ke), 1)
    r_io = lax.broadcasted_iota(jnp.int32, (tm, 1), 0)
    if prompt:
        tgt = m_io + POOL_HALO
        pos = pos0 + i * tm + r_io
    else:
        tgt = m_io + ((m_io >> 4) << 4) + 16
        pos = pos0 + (r_io & 15)
    dlt = tgt - c_io
    gw = w_ref.shape[1]
    for gi, win in enumerate(POOL_WINDOWS):
        sl = slice(gi * gw, (gi + 1) * gw)
        band = jnp.where(dlt >= 0, jnp.where(dlt < win, 1.0, 0.0), 0.0).astype(BF16)
        wsum = _dot(band, hi[:, sl]) + _dot(band, lo[:, sl])
        cnt = jnp.minimum(win, pos + 1).astype(F32)
        dd = wsum / cnt - tok[:, sl]
        o = _dot(dd.astype(BF16), w_ref[gi]) * ps_ref[:, sl]
        o_ref[:, sl] = o.astype(o_ref.dtype)


def _pool_prompt(p, w_pool, pool_scale, *, tm):
    t, c = p.shape
    per = tm // POOL_HALO
    return pl.pallas_call(
        functools.partial(_pool_kernel, prompt=True, pos0=0),
        grid=(t // tm,),
        in_specs=[pl.BlockSpec((POOL_HALO, c), lambda i: (jnp.maximum(i * per - 1, 0), 0)),
                  pl.BlockSpec((tm, c), lambda i: (i, 0)),
                  pl.BlockSpec(w_pool.shape, lambda i: (0, 0, 0)),
                  pl.BlockSpec((1, c), lambda i: (0, 0))],
        out_specs=pl.BlockSpec((tm, c), lambda i: (i, 0)),
        out_shape=jax.ShapeDtypeStruct((t, c), BF16),
        compiler_params=_cp("parallel"),
        name="pool_prompt",
    )(p, p, w_pool, pool_scale)


def _pool_sample(buf, p, w_pool, pool_scale, *, pos0):
    t, c = p.shape
    return pl.pallas_call(
        functools.partial(_pool_kernel, prompt=False, pos0=pos0),
        grid=(1,),
        in_specs=[pl.BlockSpec(buf.shape, lambda i: (0, 0)),
                  pl.BlockSpec((t, c), lambda i: (0, 0)),
                  pl.BlockSpec(w_pool.shape, lambda i: (0, 0, 0)),
                  pl.BlockSpec((1, c), lambda i: (0, 0))],
        out_specs=pl.BlockSpec((t, c), lambda i: (0, 0)),
        out_shape=jax.ShapeDtypeStruct((t, c), BF16),
        compiler_params=_cp("arbitrary"),
        name="pool_sample",
    )(buf, p, w_pool, pool_scale)


def _softmax_step(carry, s, v):
    m, l, acc = carry
    m_new = jnp.maximum(m, jnp.max(s, axis=-1, keepdims=True))
    alpha = jnp.exp(m - m_new)
    p = jnp.exp(s - m_new)
    l = alpha * l + jnp.sum(p, axis=-1, keepdims=True)
    acc = alpha * acc + _dot(p.astype(BF16), v)
    return m_new, l, acc


def _attn_prompt_kernel(fend_ref, fstart_ref, bound_ref, q_ref, k_ref, v_ref, fk_ref, o_ref,
                        *, tq):
    h = pl.program_id(0)
    i = pl.program_id(1)
    q = q_ref[...]

    def chunk(kj, carry, masked):
        off = pl.multiple_of(kj * tq, tq)
        s = _dot_nt(q, k_ref[pl.ds(off, tq), :]) - fk_ref[:, pl.ds(off, tq)]
        if masked:
            r_io = lax.broadcasted_iota(jnp.int32, (tq, tq), 0)
            c_io = lax.broadcasted_iota(jnp.int32, (tq, tq), 1)
            s = jnp.where(c_io <= r_io, s, NEG_BIG)
        return _softmax_step(carry, s, v_ref[pl.ds(off, tq), :])

    limit = fstart_ref[h, i] + bound_ref[h]
    dead = lax.fori_loop(
        0, i, lambda kj, n: n + jnp.where(fend_ref[h, kj] > limit, 1, 0), jnp.int32(0))
    init = (jnp.full((tq, 1), NEG_BIG, F32), jnp.zeros((tq, 1), F32),
            jnp.zeros((tq, HEAD_DIM), F32))
    carry = lax.fori_loop(dead, i, lambda kj, c: chunk(kj, c, False), init)
    _, l, acc = chunk(i, carry, True)
    o_ref[...] = (acc / l).astype(o_ref.dtype)


def _attn_prompt(qb, kb, vb, fcum, score_bound, *, tq):
    t, w = qb.shape
    nh = w // HEAD_DIM
    fend = fcum[:, tq - 1::tq]
    fstart = fcum[:, ::tq]
    bound = 2.0 * score_bound + EXP_ZERO_BELOW
    smem = pl.BlockSpec(memory_space=pltpu.SMEM)
    return pl.pallas_call(
        functools.partial(_attn_prompt_kernel, tq=tq),
        grid=(nh, t // tq),
        in_specs=[smem, smem, smem,
                  pl.BlockSpec((tq, HEAD_DIM), lambda h, i: (i, h)),
                  pl.BlockSpec((t, HEAD_DIM), lambda h, i: (0, h)),
                  pl.BlockSpec((t, HEAD_DIM), lambda h, i: (0, h)),
                  pl.BlockSpec((None, 1, t), lambda h, i: (h, 0, 0))],
        out_specs=pl.BlockSpec((tq, HEAD_DIM), lambda h, i: (i, h)),
        out_shape=jax.ShapeDtypeStruct((t, w), BF16),
        compiler_params=_cp("parallel", "parallel"),
        name="attn_prompt",
    )(fend, fstart, bound, qb, kb, vb, fcum.reshape(nh, 1, t))


def _attn_sample_kernel(q_ref, ck_ref, cv_ref, clf_ref, kn_ref, vn_ref, lfn_ref, o_ref,
                        qbd_scr, m_scr, l_scr, acc_scr, fcar_scr):
    c = pl.program_id(1)
    nl = q_ref.shape[0]
    rows, width = qbd_scr.shape

    @pl.when(c == 0)
    def _():
        q = q_ref[...]
        r_io = lax.broadcasted_iota(jnp.int32, (rows, width), 0)
        c_io = lax.broadcasted_iota(jnp.int32, (rows, width), 1)
        qt = jnp.concatenate([q] * FOX_HEADS, axis=0)
        qbd_scr[...] = jnp.where((r_io // nl) == (c_io // HEAD_DIM), qt, jnp.zeros_like(qt))
        m_scr[...] = jnp.full_like(m_scr, NEG_BIG)
        l_scr[...] = jnp.zeros_like(l_scr)
        acc_scr[...] = jnp.zeros_like(acc_scr)
        fcar_scr[...] = jnp.zeros_like(fcar_scr)

    def head_rows(f):
        return jnp.concatenate(
            [jnp.broadcast_to(f[h:h + 1], (nl, f.shape[1])) for h in range(FOX_HEADS)], axis=0)

    qbd = qbd_scr[...]
    fc = _lane_cumsum(clf_ref[...]) + fcar_scr[:, 0:1]
    fcar_scr[...] = jnp.broadcast_to(fc[:, -1:], fcar_scr.shape)
    s = _dot_nt(qbd, ck_ref[...].astype(BF16)) - head_rows(fc)
    carry = _softmax_step((m_scr[...], l_scr[...], acc_scr[...]), s, cv_ref[...].astype(BF16))
    m_scr[...], l_scr[...], acc_scr[...] = carry

    @pl.when(c == pl.num_programs(1) - 1)
    def _():
        kpad = jnp.zeros((LANES - nl, width), BF16)
        kn = jnp.concatenate([kn_ref[...], kpad], axis=0)
        vn = jnp.concatenate([vn_ref[...], kpad], axis=0)
        fn = _lane_cumsum(lfn_ref[...]) + fcar_scr[:, 0:1]
        sn = _dot_nt(qbd, kn) - head_rows(fn)
        r_io = lax.broadcasted_iota(jnp.int32, sn.shape, 0)
        c_io = lax.broadcasted_iota(jnp.int32, sn.shape, 1)
        sn = jnp.where(c_io <= (r_io % nl), sn, NEG_BIG)
        _, l, acc = _softmax_step((m_scr[...], l_scr[...], acc_scr[...]), sn, vn)
        o = acc / l
        o_ref[...] = jnp.concatenate(
            [o[h * nl:(h + 1) * nl, h * HEAD_DIM:(h + 1) * HEAD_DIM] for h in range(FOX_HEADS)],
            axis=-1).astype(o_ref.dtype)


def _attn_sample(qb, kb, vb, lfn, cache_k, cache_v, cache_lf_t, *, tk):
    b, past, w = cache_k.shape
    nl = qb.shape[0] // b
    nh = w // HEAD_DIM
    rows = nh * nl
    return pl.pallas_call(
        _attn_sample_kernel,
        grid=(b, past // tk),
        in_specs=[pl.BlockSpec((nl, w), lambda s, c: (s, 0)),
                  pl.BlockSpec((None, tk, w), lambda s, c: (s, c, 0)),
                  pl.BlockSpec((None, tk, w), lambda s, c: (s, c, 0)),
                  pl.BlockSpec((None, nh, tk), lambda s, c: (s, 0, c)),
                  pl.BlockSpec((nl, w), lambda s, c: (s, 0)),
                  pl.BlockSpec((nl, w), lambda s, c: (s, 0)),
                  pl.BlockSpec((None, nh, LANES), lambda s, c: (s, 0, 0))],
        out_specs=pl.BlockSpec((nl, w), lambda s, c: (s, 0)),
        out_shape=jax.ShapeDtypeStruct(qb.shape, BF16),
        scratch_shapes=[pltpu.VMEM((rows, w), BF16), pltpu.VMEM((rows, 1), F32),
                        pltpu.VMEM((rows, 1), F32), pltpu.VMEM((rows, w), F32),
                        pltpu.VMEM((nh, LANES), F32)],
        compiler_params=_cp("parallel", "arbitrary"),
        name="attn_sample",
    )(qb, cache_k, cache_v, cache_lf_t, kb, vb, lfn)


def _residual_mod(x, gate, o, g2, sh2, sc2, x_ref, h_ref):
    xn = x + gate * o
    x_ref[...] = xn
    h_ref[...] = _modulate(xn, g2, sh2, sc2).astype(h_ref.dtype)


def _merge_kernel(a_ref, b_ref, w_ref, x_ref, gate_ref, g2_ref, sh2_ref, sc2_ref, xo_ref, h_ref):
    ca = a_ref.shape[1]
    o = _dot(a_ref[...], w_ref[:ca, :]) + _dot(b_ref[...], w_ref[ca:, :])
    _residual_mod(x_ref[...], gate_ref[...], o, g2_ref[...], sh2_ref[...], sc2_ref[...],
                  xo_ref, h_ref)


def _merge(a, b, w_out, x, gate, g2, sh2, sc2, *, tm, per_row):
    t, d = x.shape
    ca = a.shape[1]
    row = lambda i: (i, 0)
    mod = _mod_spec(per_row, tm, d, False)
    return pl.pallas_call(
        _merge_kernel,
        grid=(t // tm,),
        in_specs=[pl.BlockSpec((tm, ca), row), pl.BlockSpec((tm, ca), row),
                  pl.BlockSpec(w_out.shape, lambda i: (0, 0)),
                  pl.BlockSpec((tm, d), row), mod,
                  pl.BlockSpec((1, d), lambda i: (0, 0)), mod, mod],
        out_specs=[pl.BlockSpec((tm, d), row), pl.BlockSpec((tm, d), row)],
        out_shape=[jax.ShapeDtypeStruct((t, d), F32), jax.ShapeDtypeStruct((t, d), BF16)],
        compiler_params=_cp("parallel"),
        name="merge",
    )(a, b, w_out, x, gate, g2, sh2, sc2)


def _uv_kernel(x_ref, g_ref, sh_ref, sc_ref, w_ref, lng_ref, lnb_ref, u_ref, vn_ref, h_scr):
    j = pl.program_id(1)

    @pl.when(j == 0)
    def _():
        h_scr[...] = _modulate(x_ref[...], g_ref[...], sh_ref[...], sc_ref[...]).astype(BF16)

    z = _gelu(_dot(h_scr[...], w_ref[...]))

    @pl.when(j == 0)
    def _():
        u_ref[...] = z.astype(u_ref.dtype)

    @pl.when(j == 1)
    def _():
        mu = jnp.mean(z, axis=-1, keepdims=True)
        zc = z - mu
        var = jnp.mean(zc * zc, axis=-1, keepdims=True)
        vn_ref[...] = (zc * lax.rsqrt(var + EPS) * lng_ref[...] + lnb_ref[...]).astype(vn_ref.dtype)


def _uv(x, g, shift, scale, w_uv, ln_g, ln_b, *, tm, per_row, vn_dtype):
    t, d = x.shape
    sw = w_uv.shape[1] // 2
    row = lambda i, j: (i, 0)
    const = lambda i, j: (0, 0)
    return pl.pallas_call(
        _uv_kernel,
        grid=(t // tm, 2),
        in_specs=[pl.BlockSpec((tm, d), row), pl.BlockSpec((1, d), const),
                  _mod_spec(per_row, tm, d, True), _mod_spec(per_row, tm, d, True),
                  pl.BlockSpec((d, sw), lambda i, j: (0, j)),
                  pl.BlockSpec((1, sw), const), pl.BlockSpec((1, sw), const)],
        out_specs=[pl.BlockSpec((tm, sw), row), pl.BlockSpec((tm, sw), row)],
        out_shape=[jax.ShapeDtypeStruct((t, sw), BF16), jax.ShapeDtypeStruct((t, sw), vn_dtype)],
        scratch_shapes=[pltpu.VMEM((tm, d), BF16)],
        compiler_params=_cp("parallel", "arbitrary"),
        name="sg_uv",
    )(x, g, shift, scale, w_uv, ln_g, ln_b)


def _sgate_kernel(u_ref, vn_ref, ws_ref, bs_ref, w_ref, x_ref, gate_ref, g2_ref, sh2_ref, sc2_ref,
                  xo_ref, h_ref, y_scr):
    tm, sw = u_ref.shape
    gw = sw // SG_GROUPS
    for c in range(tm // SG_CHUNK):
        rs = slice(c * SG_CHUNK, (c + 1) * SG_CHUNK)
        for g in range(SG_GROUPS):
            cs = slice(g * gw, (g + 1) * gw)
            sv = _dot(ws_ref[g], vn_ref[rs, cs].astype(BF16)) + bs_ref[:, g:g + 1]
            y_scr[rs, cs] = (u_ref[rs, cs].astype(F32) * sv).astype(BF16)
    o = _dot(y_scr[...], w_ref[...])
    _residual_mod(x_ref[...], gate_ref[...], o, g2_ref[...], sh2_ref[...], sc2_ref[...],
                  xo_ref, h_ref)


def _sgate(u, vn, ws, bs_t, w_out, x, gate, g2, sh2, sc2, *, tm, per_row):
    t, d = x.shape
    sw = u.shape[1]
    row = lambda i: (i, 0)
    mod = _mod_spec(per_row, tm, d, False)
    return pl.pallas_call(
        _sgate_kernel,
        grid=(t // tm,),
        in_specs=[pl.BlockSpec((tm, sw), row), pl.BlockSpec((tm, sw), row),
                  pl.BlockSpec(ws.shape, lambda i: (0, 0, 0)),
                  pl.BlockSpec(bs_t.shape, lambda i: (0, 0)),
                  pl.BlockSpec(w_out.shape, lambda i: (0, 0)),
                  pl.BlockSpec((tm, d), row), mod,
                  pl.BlockSpec((1, d), lambda i: (0, 0)), mod, mod],
        out_specs=[pl.BlockSpec((tm, d), row), pl.BlockSpec((tm, d), row)],
        out_shape=[jax.ShapeDtypeStruct((t, d), F32), jax.ShapeDtypeStruct((t, d), BF16)],
        scratch_shapes=[pltpu.VMEM((tm, sw), BF16)],
        compiler_params=_cp("parallel"),
        name="sg_gate",
    )(u, vn, ws, bs_t, w_out, x, gate, g2, sh2, sc2)


GROUPS = 8
ROUTE_TOKENS = GROUPS * LANES
PAIR_R1 = [0] * 16 + [a for a in range(1, 8) for _ in range(8)] + list(range(8, 16))
PAIR_R2 = list(range(16)) + [i for _ in range(1, 8) for i in range(8)] + [0] * 8
PAIR_FLAT = [a * PEER_TOPK + b for a, b in zip(PAIR_R1, PAIR_R2)]


def _extract_topk(problems, vals_refs, idx_refs):
    chains = 4

    def one_round(r, prev):
        picked = []
        for p, (ref, ids) in enumerate(problems):
            n = len(ids)
            per = -(-n // chains)
            best = []
            for c0 in range(0, n, per):
                m = ix = None
                for e in range(c0, min(c0 + per, n)):
                    x = jnp.where(prev[p] == float(ids[e]), -jnp.inf, ref[e])
                    ref[e] = x
                    if m is None:
                        m, ix = x, jnp.full(x.shape, float(ids[e]), F32)
                    else:
                        ix = jnp.where(x > m, float(ids[e]), ix)
                        m = jnp.maximum(m, x)
                best.append((m, ix))
            m, ix = best[0]
            for mc, ic in best[1:]:
                ix = jnp.where(mc > m, ic, ix)
                m = jnp.maximum(m, mc)
            vals_refs[p][r] = m
            idx_refs[p][r] = ix
            picked.append(ix)
        return tuple(picked)

    none = jnp.full((GROUPS, LANES), -1.0, F32)
    return lax.fori_loop(0, PEER_TOPK, one_round, tuple(none for _ in problems))


def _peer_route_kernel(h_ref, wq_ref, kexp_ref, r2_ref, e2_ref, c_ref, e1_ref,
                       qt_scr, s_scr, so_scr, vals_scr, idx_scr, cand_scr, cval_scr, cidx_scr,
                       t2_scr):
    hd = pl.program_id(1)

    @pl.when(hd == 0)
    def _():
        qt_scr[...] = _dot_nt(wq_ref[...], h_ref[...]).astype(BF16)

    row = pl.multiple_of(hd * 2 * PEER_KEYS, 2 * PEER_KEYS)
    for side in range(2):
        q2 = jnp.concatenate(
            [qt_scr[pl.ds(row + side * PEER_KEYS, PEER_KEYS), g * LANES:(g + 1) * LANES]
             for g in range(GROUPS)], axis=0)
        s = _dot(kexp_ref[side], q2).reshape(PEER_KEYS, GROUPS, LANES)
        s_scr[side] = s
        so_scr[side] = s

    keys = list(range(PEER_KEYS))
    _extract_topk([(s_scr.at[0], keys), (s_scr.at[1], keys)],
                  [vals_scr.at[0], vals_scr.at[1]], [idx_scr.at[0], idx_scr.at[1]])
    v1 = [vals_scr[0, r] for r in range(PEER_TOPK)]
    v2 = [vals_scr[1, r] for r in range(PEER_TOPK)]
    ncand = len(PAIR_FLAT)
    for c in range(ncand):
        cand_scr[c] = v1[PAIR_R1[c]] + v2[PAIR_R2[c]]
    (last,) = _extract_topk([(cand_scr, PAIR_FLAT)], [cval_scr], [cidx_scr])

    top = v1[0] + v2[0]
    z = jnp.zeros((GROUPS, LANES), F32)
    for r in range(PEER_TOPK):
        z = z + jnp.exp(cval_scr[r] - top)
    inv_z = 1.0 / z
    cnt = [jnp.zeros((GROUPS, LANES), F32) for _ in range(PEER_TOPK)]
    for c in range(ncand):
        picked = jnp.where(cand_scr[c] == -jnp.inf, 1.0,
                           jnp.where(last == float(PAIR_FLAT[c]), 1.0, 0.0))
        cnt[PAIR_R1[c]] = cnt[PAIR_R1[c]] + picked
    idx1 = [idx_scr[0, r] for r in range(PEER_TOPK)]
    idx2 = [idx_scr[1, r] for r in range(PEER_TOPK)]

    def per_key(k, _):
        kf = lax.convert_element_type(k, F32)
        rank2 = jnp.full((GROUPS, LANES), RANK_NONE, F32)
        ck = jnp.zeros((GROUPS, LANES), F32)
        for r in range(PEER_TOPK):
            rank2 = jnp.where(idx2[r] == kf, float(r), rank2)
            ck = jnp.where(idx1[r] == kf, cnt[r], ck)
        base = pl.multiple_of(k * GROUPS, GROUPS)
        t2_scr[0, pl.ds(base, GROUPS), :] = rank2
        t2_scr[1, pl.ds(base, GROUPS), :] = jnp.exp(so_scr[1, k] - v2[0])
        c_ref[k] = ck
        e1_ref[k] = jnp.exp(so_scr[0, k] - v1[0]) * inv_z
        return 0

    lax.fori_loop(0, PEER_KEYS, per_key, 0)
    for g in range(GROUPS):
        cols = slice(g * LANES, (g + 1) * LANES)
        r2_ref[:, cols] = t2_scr[0, pl.ds(g, PEER_KEYS, stride=GROUPS), :].astype(BF16)
        e2_ref[:, cols] = t2_scr[1, pl.ds(g, PEER_KEYS, stride=GROUPS), :].astype(BF16)


def _peer_route(h, wq_t, kexp):
    t, d = h.shape
    nblk = t // ROUTE_TOKENS
    nq = wq_t.shape[0]
    std = jax.ShapeDtypeStruct((PEER_HEADS, PEER_KEYS, t), BF16)
    grp = jax.ShapeDtypeStruct((PEER_HEADS, PEER_KEYS, t // LANES, LANES), F32)
    std_spec = pl.BlockSpec((None, PEER_KEYS, ROUTE_TOKENS), lambda i, hd: (hd, 0, i))
    grp_spec = pl.BlockSpec((None, PEER_KEYS, GROUPS, LANES), lambda i, hd: (hd, 0, i, 0))
    tile = (GROUPS, LANES)
    return pl.pallas_call(
        _peer_route_kernel,
        grid=(nblk, PEER_HEADS),
        in_specs=[pl.BlockSpec((ROUTE_TOKENS, d), lambda i, hd: (i, 0)),
                  pl.BlockSpec(wq_t.shape, lambda i, hd: (0, 0)),
                  pl.BlockSpec((None,) + kexp.shape[1:], lambda i, hd: (hd, 0, 0, 0))],
        out_specs=[std_spec, std_spec, grp_spec, grp_spec],
        out_shape=[std, std, grp, grp],
        scratch_shapes=[pltpu.VMEM((nq, ROUTE_TOKENS), BF16),
                        pltpu.VMEM((2, PEER_KEYS) + tile, F32),
                        pltpu.VMEM((2, PEER_KEYS) + tile, F32),
                        pltpu.VMEM((2, PEER_TOPK) + tile, F32),
                        pltpu.VMEM((2, PEER_TOPK) + tile, F32),
                        pltpu.VMEM((len(PAIR_FLAT),) + tile, F32),
                        pltpu.VMEM((PEER_TOPK,) + tile, F32),
                        pltpu.VMEM((PEER_TOPK,) + tile, F32),
                        pltpu.VMEM((2, PEER_KEYS * GROUPS, LANES), F32)],
        compiler_params=_cp("parallel", "arbitrary"),
        name="peer_route",
    )(h, wq_t, kexp)


def _peer_dense_kernel(h_ref, u_ref, v_ref, r2_ref, e2_ref, c_ref, e1_ref, o_ref):
    e = pl.program_id(1)
    tb = h_ref.shape[0]
    sub = r2_ref.shape[2]

    @pl.when(e == 0)
    def _():
        o_ref[...] = jnp.zeros_like(o_ref)

    tw = min(tb, PEER_UNIT_TOKENS)
    per = PEER_SUB // PEER_KEYS
    ngrp = tw // LANES
    goff = 0 if tb == ROUTE_TOKENS else (pl.program_id(0) % (ROUTE_TOKENS // tb)) * ngrp
    units = [(q, tp) for q in range(u_ref.shape[0] // PEER_SUB) for tp in range(tb // tw)]

    def token_rows(ref, h, jj, tp):
        row = jnp.concatenate(
            [ref[h, jj, pl.ds(goff + tp * ngrp + g, 1), :] for g in range(ngrp)], axis=1)
        return jnp.broadcast_to(row, (sub, tw)).astype(BF16)[None]

    def scores(unit):
        q, tp = unit
        return _dot_nt(u_ref[q * PEER_SUB:(q + 1) * PEER_SUB, :],
                       h_ref[tp * tw:(tp + 1) * tw, :])

    accs = [o_ref[:, tp * tw:(tp + 1) * tw] for tp in range(tb // tw)]
    at_next = scores(units[0])
    for n, (q, tp) in enumerate(units):
        at = at_next
        if n + 1 < len(units):
            at_next = scores(units[n + 1])
        cols = slice(tp * tw, (tp + 1) * tw)
        gs = []
        for jj in range(q * per, (q + 1) * per):
            w = jnp.zeros((PEER_KEYS // sub, sub, tw), BF16)
            for h in range(PEER_HEADS):
                picked = jnp.where(r2_ref[h, :, :, cols] < token_rows(c_ref, h, jj, tp),
                                   e2_ref[h, :, :, cols], jnp.zeros_like(w))
                w = w + picked * token_rows(e1_ref, h, jj, tp)
            lo = (jj - q * per) * PEER_KEYS
            a = _gelu(at[lo:lo + PEER_KEYS, :]).astype(BF16)
            gs.append(a * w.reshape(PEER_KEYS, tw))
        accs[tp] = accs[tp] + lax.dot_general(
            v_ref[q * PEER_SUB:(q + 1) * PEER_SUB, :], jnp.concatenate(gs, axis=0),
            (((0,), (0,)), ((), ())), preferred_element_type=F32)
    for tp, acc in enumerate(accs):
        o_ref[:, tp * tw:(tp + 1) * tw] = acc


def _peer_dense(h, u_all, v_all, layer, r2, e2, c, e1, *, tb, te):
    t, d = h.shape
    ne = u_all.shape[1]
    sub = 16
    tr = r2.shape[2]
    r2 = r2.reshape(PEER_HEADS, PEER_KEYS // sub, sub, tr)
    e2 = e2.reshape(PEER_HEADS, PEER_KEYS // sub, sub, tr)
    nchunk = te // PEER_KEYS
    per_route = ROUTE_TOKENS // tb
    once = pl.Buffered(1)
    full = pl.BlockSpec((PEER_HEADS, PEER_KEYS // sub, sub, tb), lambda i, e: (0, 0, 0, i),
                        pipeline_mode=once)
    rows = pl.BlockSpec((PEER_HEADS, nchunk, GROUPS, LANES),
                        lambda i, e: (0, e, i // per_route, 0))
    return pl.pallas_call(
        _peer_dense_kernel,
        grid=(t // tb, ne // te),
        in_specs=[pl.BlockSpec((tb, d), lambda i, e: (i, 0), pipeline_mode=once),
                  pl.BlockSpec((None, te, d), lambda i, e: (layer, e, 0)),
                  pl.BlockSpec((None, te, d), lambda i, e: (layer, e, 0)),
                  full, full, rows, rows],
        out_specs=pl.BlockSpec((d, tb), lambda i, e: (0, i)),
        out_shape=jax.ShapeDtypeStruct((d, t), F32),
        compiler_params=_cp("parallel", "arbitrary"),
        name="peer_dense",
    )(h, u_all, v_all, r2, e2, c, e1)


def _peer_out_kernel(ot_ref, x_ref, gate_ref, xo_ref):
    xo_ref[...] = x_ref[...] + gate_ref[...] * ot_ref[...].T


def _peer_out(ot, x, gate, *, tm, per_row):
    t, d = x.shape
    row = lambda i: (i, 0)
    return pl.pallas_call(
        _peer_out_kernel,
        grid=(t // tm,),
        in_specs=[pl.BlockSpec((d, tm), lambda i: (0, i)), pl.BlockSpec((tm, d), row),
                  _mod_spec(per_row, tm, d, False)],
        out_specs=pl.BlockSpec((tm, d), row),
        out_shape=jax.ShapeDtypeStruct((t, d), F32),
        compiler_params=_cp("parallel"),
        name="peer_out",
    )(ot, x, gate)


def _peer(h, x, gate, wq_t, kexp, u_all, v_all, layer, *, tb, tm, per_row):
    t = h.shape[0]
    h_route = jnp.pad(h, ((0, -t % ROUTE_TOKENS), (0, 0)))
    r2, e2, c, e1 = _peer_route(h_route, wq_t, kexp)
    ot = _peer_dense(h, u_all, v_all, layer, r2, e2, c, e1, tb=tb, te=PEER_EXPERT_BLOCK)
    return _peer_out(ot, x, gate, tm=tm, per_row=per_row)


def kernel(x_prompt, x_sample, c_prompt, c_sample, cache_pool, cache_k, cache_v, cache_logf, w_ada, b_ada, norm1, norm2, w_in_ab, b_f, w_pool, pool_scale, q_gain, k_gain, w_out_ab, w_uv, sg_ln_g, sg_ln_b, w_s, b_s, w_out_sg, peer_wq, peer_keys, peer_u, peer_v):
    nb, seq, d = x_prompt.shape
    db, dl, _ = x_sample.shape
    depth = w_ada.shape[0]
    past = cache_k.shape[2]
    pool_hist = cache_pool.shape[2]
    assert nb == 1 and db * dl == LANES and dl == 16 and pool_hist == dl - 1
    fox_w = FOX_HEADS * HEAD_DIM
    pool_w = w_pool.shape[1] * w_pool.shape[2]
    ts = db * dl

    c_rows = jnp.concatenate(
        [c_prompt, c_sample, jnp.zeros((16 - nb - db, d), F32)], axis=0)
    mods = _ada(c_rows, w_ada, b_ada)

    def mod_p(layer, k):
        return mods[layer, 0:1, k * d:(k + 1) * d]

    def mod_s(layer, k):
        return jnp.repeat(mods[layer, 1:1 + db, k * d:(k + 1) * d], dl, axis=0)

    xp = x_prompt.reshape(seq, d)
    xs = x_sample.reshape(ts, d)
    tm_p = 512 if seq % 512 == 0 else 256
    outs = {k: [] for k in ("pool_p", "pool_s", "kp", "ks", "vp", "vs", "lfp", "lfs", "sgv")}
    u_all = peer_u.astype(BF16)
    v_all = peer_v.astype(BF16)

    for layer in range(depth):
        j = layer // 2
        g1 = norm1[layer].reshape(1, d)
        g2 = norm2[layer].reshape(1, d)
        if layer % 2 == 0:
            w_in = w_in_ab[j]
            w_main = w_in[:, :pool_w + 3 * fox_w].astype(BF16)
            wf_t = w_in[:, pool_w + 3 * fox_w:].T.astype(BF16)
            bfc = b_f[j].reshape(FOX_HEADS, 1)
            qg = q_gain[j].reshape(1, fox_w)
            kg = k_gain[j].reshape(1, fox_w)
            wp = w_pool[j].astype(BF16)
            psc = pool_scale[j].reshape(1, pool_w)
            wo = w_out_ab[j].astype(BF16)

            p, qb, k, kb, v, vb, lf, fc = _proj(
                xp, g1, mod_p(layer, 0), mod_p(layer, 1), w_main, wf_t, bfc, qg, kg,
                tm=tm_p, per_row=False, do_cumsum=True)
            a_out = _pool_prompt(p, wp, psc, tm=256)
            score_bound = 1.02 * (HEAD_DIM ** 0.5) * (jnp.max(jnp.abs(q_gain[j]), axis=-1)
                                                      * jnp.max(jnp.abs(k_gain[j]), axis=-1))
            b_out = _attn_prompt(qb, kb, vb, fc, score_bound, tq=tm_p)
            xp, hp = _merge(a_out, b_out, wo, xp, mod_p(layer, 2), g2, mod_p(layer, 3),
                            mod_p(layer, 4), tm=256, per_row=False)
            outs["pool_p"].append(p[-pool_hist:].reshape(nb, pool_hist, pool_w))
            outs["kp"].append(k.reshape(nb, seq, FOX_HEADS, HEAD_DIM))
            outs["vp"].append(v.reshape(nb, seq, FOX_HEADS, HEAD_DIM))
            outs["lfp"].append(lf.T.reshape(nb, seq, FOX_HEADS))

            p, qb, k, kb, v, vb, lf, _ = _proj(
                xs, g1, mod_s(layer, 0), mod_s(layer, 1), w_main, wf_t, bfc, qg, kg,
                tm=ts, per_row=True, do_cumsum=False)
            p3 = p.reshape(db, dl, pool_w)
            buf = jnp.concatenate([jnp.zeros((db, 1, pool_w), F32), cache_pool[j], p3], axis=1)
            a_out = _pool_sample(buf.reshape(db * 2 * dl, pool_w), p, wp, psc, pos0=past)
            lfn = jnp.transpose(lf.reshape(FOX_HEADS, db, dl), (1, 0, 2))
            lfn = jnp.pad(lfn, ((0, 0), (0, 0), (0, LANES - dl)))
            b_out = _attn_sample(
                qb, kb, vb, lfn, cache_k[j].reshape(db, past, fox_w),
                cache_v[j].reshape(db, past, fox_w), jnp.transpose(cache_logf[j], (0, 2, 1)),
                tk=min(past, 1024))
            xs, hs = _merge(a_out, b_out, wo, xs, mod_s(layer, 2), g2, mod_s(layer, 3),
                            mod_s(layer, 4), tm=ts, per_row=True)
            outs["pool_s"].append(jnp.concatenate([cache_pool[j], p3], axis=1)[:, -pool_hist:])
            outs["ks"].append(k.reshape(db, dl, FOX_HEADS, HEAD_DIM))
            outs["vs"].append(v.reshape(db, dl, FOX_HEADS, HEAD_DIM))
            outs["lfs"].append(lf.T.reshape(db, dl, FOX_HEADS))
        else:
            wuv = w_uv[j].astype(BF16)
            lng = sg_ln_g[j].reshape(1, -1)
            lnb = sg_ln_b[j].reshape(1, -1)
            wo = w_out_sg[j].astype(BF16)
            tri = jnp.tril(jnp.ones((SG_CHUNK, SG_CHUNK), bool))
            ws_p = jnp.where(tri, w_s[j], 0.0).astype(BF16)
            bs_p = b_s[j].T
            ws16 = jnp.where(tri[:dl, :dl], w_s[j][:, :dl, :dl], 0.0)
            ws_s = jnp.einsum("ab,gts->gatbs", jnp.eye(db, dtype=F32), ws16)
            ws_s = ws_s.reshape(SG_GROUPS, ts, ts).astype(BF16)
            bs_s = jnp.tile(b_s[j][:, :dl], (1, db)).T

            u, vn = _uv(xp, g1, mod_p(layer, 0), mod_p(layer, 1), wuv, lng, lnb,
                        tm=tm_p, per_row=False, vn_dtype=BF16)
            xp, hp = _sgate(u, vn, ws_p, bs_p, wo, xp, mod_p(layer, 2), g2, mod_p(layer, 3),
                            mod_p(layer, 4), tm=256, per_row=False)
            u, vn = _uv(xs, g1, mod_s(layer, 0), mod_s(layer, 1), wuv, lng, lnb,
                        tm=ts, per_row=True, vn_dtype=F32)
            xs, hs = _sgate(u, vn, ws_s, bs_s, wo, xs, mod_s(layer, 2), g2, mod_s(layer, 3),
                            mod_s(layer, 4), tm=ts, per_row=True)
            outs["sgv"].append(vn.reshape(db, dl, -1))

        wq_t = peer_wq[layer].T.astype(BF16)
        kexp = jnp.einsum("hskd,gj->hskgjd", peer_keys[layer].astype(BF16),
                          jnp.eye(GROUPS, dtype=BF16))
        kexp = kexp.reshape(PEER_HEADS, 2, PEER_KEYS * GROUPS, GROUPS * kexp.shape[-1])
        xp = _peer(hp, xp, mod_p(layer, 5), wq_t, kexp, u_all, v_all, layer,
                   tb=PEER_TOKEN_BLOCK if seq % PEER_TOKEN_BLOCK == 0 else 256, tm=256,
                   per_row=False)
        xs = _peer(hs, xs, mod_s(layer, 5), wq_t, kexp, u_all, v_all, layer,
                   tb=ts, tm=ts, per_row=True)

    st = lambda key: jnp.stack(outs[key])
    return (xp.reshape(nb, seq, d), xs.reshape(db, dl, d), st("pool_p"), st("pool_s"),
            st("kp"), st("ks"), st("vp"), st("vs"), st("lfp"), st("lfs"), st("sgv"))
```

```python
import functools

import jax
import jax.numpy as jnp
from jax import lax
from jax.experimental import pallas as pl
from jax.experimental.pallas import tpu as pltpu

F32 = jnp.float32
BF16 = jnp.bfloat16
EPS = 1e-6

LANES = 128
POOL_WINDOWS = (2, 4, 8, 16)
POOL_HALO = 128
FOX_HEADS = 8
HEAD_DIM = 128
SAMPLE_KEY_CHUNK = 512
SG_GROUPS = 8
SG_CHUNK = 128
PEER_HEADS = 8
PEER_KEYS = 128
PEER_TOPK = 16
PEER_TOKEN_BLOCK = 512
PEER_EXPERT_BLOCK = 1024
PEER_SUB = 256
PEER_UNIT_TOKENS = 1024
NEG_BIG = -1e30
EXP_ZERO_BELOW = 110.0
RANK_NONE = 64.0
VMEM_LIMIT = 56 * 1024 * 1024

NT_DIMS = (((1,), (1,)), ((), ()))


def _cp(*sem):
    return pltpu.CompilerParams(dimension_semantics=sem, vmem_limit_bytes=VMEM_LIMIT)


def _dot(a, b):
    return jnp.dot(a, b, preferred_element_type=F32)


def _dot_nt(a, b):
    return lax.dot_general(a, b, NT_DIMS, preferred_element_type=F32)


def _modulate(x, g, shift, scale):
    ms = jnp.mean(x * x, axis=-1, keepdims=True)
    return x * lax.rsqrt(ms + EPS) * g * (1.0 + scale) + shift


def _gelu(x):
    c = 0.7978845608028654
    half = 0.5 * x
    return half + half * jnp.tanh(x * (c + (c * 0.044715) * (x * x)))


def _log_sigmoid(x):
    return -(jnp.maximum(-x, 0.0) + jnp.log(1.0 + jnp.exp(-jnp.abs(x))))


def _lane_cumsum(x):
    n = x.shape[-1]
    lane = lax.broadcasted_iota(jnp.int32, x.shape, x.ndim - 1)
    s = 1
    while s < n:
        x = x + jnp.where(lane >= s, pltpu.roll(x, s, x.ndim - 1), 0.0)
        s *= 2
    return x


def _ada_kernel(c_ref, w_ref, b_ref, o_ref):
    c = c_ref[...]
    a = (c * (1.0 / (1.0 + jnp.exp(-c)))).astype(BF16)
    o_ref[0] = _dot(a, w_ref[0].astype(BF16)) + b_ref[0]


def _ada(c_rows, w_ada, b_ada):
    nl, d, n = w_ada.shape
    r = c_rows.shape[0]
    tn = 1024
    return pl.pallas_call(
        _ada_kernel,
        grid=(nl, n // tn),
        in_specs=[pl.BlockSpec((r, d), lambda l, j: (0, 0)),
                  pl.BlockSpec((1, d, tn), lambda l, j: (l, 0, j)),
                  pl.BlockSpec((1, 1, tn), lambda l, j: (l, 0, j))],
        out_specs=pl.BlockSpec((1, r, tn), lambda l, j: (l, 0, j)),
        out_shape=jax.ShapeDtypeStruct((nl, r, n), F32),
        compiler_params=_cp("parallel", "parallel"),
        name="ada",
    )(c_rows, w_ada, b_ada.reshape(nl, 1, n))


def _mod_spec(per_row, tm, d, nj):
    if nj:
        return (pl.BlockSpec((tm, d), lambda i, j: (i, 0)) if per_row
                else pl.BlockSpec((1, d), lambda i, j: (0, 0)))
    return (pl.BlockSpec((tm, d), lambda i: (i, 0)) if per_row
            else pl.BlockSpec((1, d), lambda i: (0, 0)))


def _head_rms(z, gain):
    outs = []
    for h in range(FOX_HEADS):
        zh = z[:, h * HEAD_DIM:(h + 1) * HEAD_DIM]
        ms = jnp.mean(zh * zh, axis=-1, keepdims=True)
        outs.append(zh * lax.rsqrt(ms + EPS))
    return jnp.concatenate(outs, axis=-1) * gain


def _proj_kernel(x_ref, g_ref, sh_ref, sc_ref, w_ref, wf_ref, bf_ref, qg_ref, kg_ref,
                 p_ref, q_ref, k_ref, kb_ref, v_ref, vb_ref, lf_ref, fc_ref,
                 h_scr, carry_scr, *, do_cumsum):
    i = pl.program_id(0)
    j = pl.program_id(1)

    @pl.when(j == 0)
    def _():
        hb = _modulate(x_ref[...], g_ref[...], sh_ref[...], sc_ref[...]).astype(BF16)
        h_scr[...] = hb
        lf = _log_sigmoid(_dot_nt(wf_ref[...], hb) + bf_ref[...])
        lf_ref[...] = lf
        if do_cumsum:
            @pl.when(i == 0)
            def _():
                carry_scr[...] = jnp.zeros_like(carry_scr)
            fc = _lane_cumsum(lf) + carry_scr[:, 0:1]
            fc_ref[...] = fc
            carry_scr[...] = jnp.broadcast_to(fc[:, -1:], carry_scr.shape)
        else:
            fc_ref[...] = lf

    z = _dot(h_scr[...], w_ref[...])

    @pl.when(j == 0)
    def _():
        p_ref[...] = z

    @pl.when(j == 1)
    def _():
        q_ref[...] = (_head_rms(z, qg_ref[...]) * (HEAD_DIM ** -0.5)).astype(BF16)

    @pl.when(j == 2)
    def _():
        kn = _head_rms(z, kg_ref[...])
        k_ref[...] = kn
        kb_ref[...] = kn.astype(BF16)

    @pl.when(j == 3)
    def _():
        v_ref[...] = z
        vb_ref[...] = z.astype(BF16)


def _proj(x, g, shift, scale, w_main, wf_t, b_f, q_gain, k_gain, *, tm, per_row, do_cumsum):
    t, d = x.shape
    w4 = w_main.shape[1] // 4
    nh = wf_t.shape[0]
    row = lambda i, j: (i, 0)
    const = lambda i, j: (0, 0)
    tok_f32 = jax.ShapeDtypeStruct((t, w4), F32)
    tok_b16 = jax.ShapeDtypeStruct((t, w4), BF16)
    head_t = jax.ShapeDtypeStruct((nh, t), F32)
    return pl.pallas_call(
        functools.partial(_proj_kernel, do_cumsum=do_cumsum),
        grid=(t // tm, 4),
        in_specs=[pl.BlockSpec((tm, d), row),
                  pl.BlockSpec((1, d), const),
                  _mod_spec(per_row, tm, d, True),
                  _mod_spec(per_row, tm, d, True),
                  pl.BlockSpec((d, w4), lambda i, j: (0, j)),
                  pl.BlockSpec((nh, d), const),
                  pl.BlockSpec((nh, 1), const),
                  pl.BlockSpec((1, w4), const),
                  pl.BlockSpec((1, w4), const)],
        out_specs=[pl.BlockSpec((tm, w4), row)] * 6
                  + [pl.BlockSpec((nh, tm), lambda i, j: (0, i))] * 2,
        out_shape=[tok_f32, tok_b16, tok_f32, tok_b16, tok_f32, tok_b16, head_t, head_t],
        scratch_shapes=[pltpu.VMEM((tm, d), BF16), pltpu.VMEM((nh, LANES), F32)],
        compiler_params=_cp("arbitrary", "arbitrary"),
        name="proj",
    )(x, g, shift, scale, w_main, wf_t, b_f, q_gain, k_gain)


def _pool_kernel(ext_ref, tok_ref, w_ref, ps_ref, o_ref, *, prompt, pos0):
    i = pl.program_id(0)
    tok = tok_ref[...]
    tm = tok.shape[0]
    if prompt:
        halo = jnp.where(i == 0, 0.0, ext_ref[...])
        ext = jnp.concatenate([halo, tok], axis=0)
    else:
        ext = ext_ref[...]
    ke = ext.shape[0]
    hi = ext.astype(BF16)
    lo = (ext - hi.astype(F32)).astype(BF16)
    m_io = lax.broadcasted_iota(jnp.int32, (tm, ke), 0)
    c_io = lax.broadcasted_iota(jnp.int32, (tm, ke), 1)
    r_io = lax.broadcasted_iota(jnp.int32, (tm, 1), 0)
    if prompt:
        tgt = m_io + POOL_HALO
        pos = pos0 + i * tm + r_io
    else:
        tgt = m_io + ((m_io >> 4) << 4) + 16
        pos = pos0 + (r_io & 15)
    dlt = tgt - c_io
    gw = w_ref.shape[1]
    for gi, win in enumerate(POOL_WINDOWS):
        sl = slice(gi * gw, (gi + 1) * gw)
        band = jnp.where(dlt >= 0, jnp.where(dlt < win, 1.0, 0.0), 0.0).astype(BF16)
        wsum = _dot(band, hi[:, sl]) + _dot(band, lo[:, sl])
        cnt = jnp.minimum(win, pos + 1).astype(F32)
        dd = wsum / cnt - tok[:, sl]
        o = _dot(dd.astype(BF16), w_ref[gi]) * ps_ref[:, sl]
        o_ref[:, sl] = o.astype(o_ref.dtype)


def _pool_prompt(p, w_pool, pool_scale, *, tm):
    t, c = p.shape
    per = tm // POOL_HALO
    return pl.pallas_call(
        functools.partial(_pool_kernel, prompt=True, pos0=0),
        grid=(t // tm,),
        in_specs=[pl.BlockSpec((POOL_HALO, c), lambda i: (jnp.maximum(i * per - 1, 0), 0)),
                  pl.BlockSpec((tm, c), lambda i: (i, 0)),
                  pl.BlockSpec(w_pool.shape, lambda i: (0, 0, 0)),
                  pl.BlockSpec((1, c), lambda i: (0, 0))],
        out_specs=pl.BlockSpec((tm, c), lambda i: (i, 0)),
        out_shape=jax.ShapeDtypeStruct((t, c), BF16),
        compiler_params=_cp("parallel"),
        name="pool_prompt",
    )(p, p, w_pool, pool_scale)


def _pool_sample(buf, p, w_pool, pool_scale, *, pos0):
    t, c = p.shape
    return pl.pallas_call(
        functools.partial(_pool_kernel, prompt=False, pos0=pos0),
        grid=(1,),
        in_specs=[pl.BlockSpec(buf.shape, lambda i: (0, 0)),
                  pl.BlockSpec((t, c), lambda i: (0, 0)),
                  pl.BlockSpec(w_pool.shape, lambda i: (0, 0, 0)),
                  pl.BlockSpec((1, c), lambda i: (0, 0))],
        out_specs=pl.BlockSpec((t, c), lambda i: (0, 0)),
        out_shape=jax.ShapeDtypeStruct((t, c), BF16),
        compiler_params=_cp("arbitrary"),
        name="pool_sample",
    )(buf, p, w_pool, pool_scale)


def _softmax_step(carry, s, v):
    m, l, acc = carry
    m_new = jnp.maximum(m, jnp.max(s, axis=-1, keepdims=True))
    alpha = jnp.exp(m - m_new)
    p = jnp.exp(s - m_new)
    l = alpha * l + jnp.sum(p, axis=-1, keepdims=True)
    acc = alpha * acc + _dot(p.astype(BF16), v)
    return m_new, l, acc


def _attn_prompt_kernel(fend_ref, fstart_ref, bound_ref, q_ref, k_ref, v_ref, fk_ref, o_ref,
                        *, tq):
    h = pl.program_id(0)
    i = pl.program_id(1)
    q = q_ref[...]

    def chunk(kj, carry, masked):
        off = pl.multiple_of(kj * tq, tq)
        s = _dot_nt(q, k_ref[pl.ds(off, tq), :]) - fk_ref[:, pl.ds(off, tq)]
        if masked:
            r_io = lax.broadcasted_iota(jnp.int32, (tq, tq), 0)
            c_io = lax.broadcasted_iota(jnp.int32, (tq, tq), 1)
            s = jnp.where(c_io <= r_io, s, NEG_BIG)
        return _softmax_step(carry, s, v_ref[pl.ds(off, tq), :])

    limit = fstart_ref[h, i] + bound_ref[h]
    dead = lax.fori_loop(
        0, i, lambda kj, n: n + jnp.where(fend_ref[h, kj] > limit, 1, 0), jnp.int32(0))
    init = (jnp.full((tq, 1), NEG_BIG, F32), jnp.zeros((tq, 1), F32),
            jnp.zeros((tq, HEAD_DIM), F32))
    carry = lax.fori_loop(dead, i, lambda kj, c: chunk(kj, c, False), init)
    _, l, acc = chunk(i, carry, True)
    o_ref[...] = (acc / l).astype(o_ref.dtype)


def _attn_prompt(qb, kb, vb, fcum, score_bound, *, tq):
    t, w = qb.shape
    nh = w // HEAD_DIM
    fend = fcum[:, tq - 1::tq]
    fstart = fcum[:, ::tq]
    bound = 2.0 * score_bound + EXP_ZERO_BELOW
    smem = pl.BlockSpec(memory_space=pltpu.SMEM)
    return pl.pallas_call(
        functools.partial(_attn_prompt_kernel, tq=tq),
        grid=(nh, t // tq),
        in_specs=[smem, smem, smem,
                  pl.BlockSpec((tq, HEAD_DIM), lambda h, i: (i, h)),
                  pl.BlockSpec((t, HEAD_DIM), lambda h, i: (0, h)),
                  pl.BlockSpec((t, HEAD_DIM), lambda h, i: (0, h)),
                  pl.BlockSpec((None, 1, t), lambda h, i: (h, 0, 0))],
        out_specs=pl.BlockSpec((tq, HEAD_DIM), lambda h, i: (i, h)),
        out_shape=jax.ShapeDtypeStruct((t, w), BF16),
        compiler_params=_cp("parallel", "parallel"),
        name="attn_prompt",
    )(fend, fstart, bound, qb, kb, vb, fcum.reshape(nh, 1, t))


def _strided_lane_cumsum(x, stride, seg=None):
    n = seg or x.shape[-1]
    lane = lax.broadcasted_iota(jnp.int32, x.shape, x.ndim - 1)
    pos = lane if seg is None else lane & (seg - 1)
    s = stride
    while s < n:
        x = x + jnp.where(pos >= s, pltpu.roll(x, s, x.ndim - 1), 0.0)
        s *= 2
    return x


def _tile_last_lanes(x, period):
    n = x.shape[-1]
    lane = lax.broadcasted_iota(jnp.int32, x.shape, x.ndim - 1)
    y = jnp.where(lane >= n - period, x, 0.0)
    s = period
    while s < n:
        y = y + pltpu.roll(y, n - s, x.ndim - 1)
        s *= 2
    return y


def _chunk_suffix_kernel(lf_ref, o_ref, *, tk):
    x = lf_ref[...]
    p = x.shape[1]
    pos = lax.broadcasted_iota(jnp.int32, (p, LANES), 0)
    col = lax.broadcasted_iota(jnp.int32, (p, LANES), 1)
    later = jnp.where(pos >= (col + 1) * tk, 1.0, 0.0).astype(BF16)
    hi = x.astype(BF16)
    r1 = x - hi.astype(F32)
    mid = r1.astype(BF16)
    lo = (r1 - mid.astype(F32)).astype(BF16)
    o_ref[...] = _dot(hi, later) + _dot(mid, later) + _dot(lo, later)


def _chunk_suffix(lf_rows, *, tk):
    r, p = lf_rows.shape
    return pl.pallas_call(
        functools.partial(_chunk_suffix_kernel, tk=tk),
        grid=(1,),
        in_specs=[pl.BlockSpec((r, p), lambda i: (0, 0))],
        out_specs=pl.BlockSpec((r, LANES), lambda i: (0, 0)),
        out_shape=jax.ShapeDtypeStruct((r, LANES), F32),
        compiler_params=_cp("arbitrary"),
        name="cache_decay",
    )(lf_rows)


def _attn_sample_kernel(need_ref, q_ref, ck_ref, cv_ref, clf_ref, kn_ref, vn_ref, lfn_ref, o_ref,
                        q_scr, m_scr, l_scr, acc_scr, car_scr):
    b = pl.program_id(0)
    c = pl.program_id(1)
    nl = q_ref.shape[0]
    rows = q_scr.shape[0]

    def stack_heads(x):
        return jnp.concatenate(
            [x[:, h * HEAD_DIM:(h + 1) * HEAD_DIM] for h in range(FOX_HEADS)], axis=0)

    @pl.when(c == 0)
    def _():
        qs = stack_heads(q_ref[...])
        q_scr[...] = qs
        kn = stack_heads(kn_ref[...])
        r_io = lax.broadcasted_iota(jnp.int32, (rows, rows), 0)
        c_io = lax.broadcasted_iota(jnp.int32, (rows, rows), 1)
        ncum = _strided_lane_cumsum(lfn_ref[...], 1, seg=nl)
        sn = jnp.where((r_io // nl) == (c_io // nl),
                       jnp.where((c_io % nl) <= (r_io % nl), _dot_nt(qs, kn) - ncum, NEG_BIG),
                       NEG_BIG)
        init = (jnp.full((rows, 1), NEG_BIG, F32), jnp.zeros((rows, 1), F32),
                jnp.zeros((rows, HEAD_DIM), F32))
        m_scr[...], l_scr[...], acc_scr[...] = _softmax_step(init, sn, stack_heads(vn_ref[...]))
        car_scr[...] = jnp.zeros_like(car_scr)

    @pl.when(c < need_ref[b])
    def _():
        n = clf_ref.shape[1]
        incl = _strided_lane_cumsum(clf_ref[...], FOX_HEADS)
        total = _tile_last_lanes(incl, FOX_HEADS)
        after = car_scr[...] + total - incl
        car_scr[...] = car_scr[...] + total
        r_io = lax.broadcasted_iota(jnp.int32, (rows, n), 0)
        c_io = lax.broadcasted_iota(jnp.int32, (rows, n), 1)
        s = _dot_nt(q_scr[...], ck_ref[...].astype(BF16)) + after
        s = jnp.where((c_io % FOX_HEADS) == (r_io // nl), s, NEG_BIG)
        carry = _softmax_step((m_scr[...], l_scr[...], acc_scr[...]), s, cv_ref[...].astype(BF16))
        m_scr[...], l_scr[...], acc_scr[...] = carry

    @pl.when(c == pl.num_programs(1) - 1)
    def _():
        o = acc_scr[...] / l_scr[...]
        o_ref[...] = jnp.concatenate(
            [o[h * nl:(h + 1) * nl, :] for h in range(FOX_HEADS)], axis=-1).astype(o_ref.dtype)


def _attn_sample(qb, kb, vb, lf_new, cache_k, cache_v, cache_lf, score_bound, *, tk):
    b, past, nh, dh = cache_k.shape
    w = nh * dh
    nl = qb.shape[0] // b
    rows = nh * nl
    nch = past // tk
    after = _chunk_suffix(jnp.transpose(cache_lf, (0, 2, 1)).reshape(b * nh, past), tk=tk)
    alive = (after.reshape(b, nh, LANES)[:, :, :nch]
             + 2.0 * score_bound[None, :, None]) >= -EXP_ZERO_BELOW
    need = jnp.maximum(jnp.max(jnp.sum(alive, axis=2), axis=1), 1).astype(jnp.int32)
    lfn = jnp.transpose(lf_new.reshape(nh, b, nl), (1, 0, 2)).reshape(b, 1, rows)

    def newest_first(c, need_s):
        return nch - 1 - jnp.minimum(c, need_s - 1)

    chunk = lambda s, c, need: (s, newest_first(c, need[s]), 0)
    new = lambda s, c, need: (s, 0)
    return pl.pallas_call(
        _attn_sample_kernel,
        grid_spec=pltpu.PrefetchScalarGridSpec(
            num_scalar_prefetch=1,
            grid=(b, nch),
            in_specs=[pl.BlockSpec((nl, w), new),
                      pl.BlockSpec((None, tk * nh, dh), chunk),
                      pl.BlockSpec((None, tk * nh, dh), chunk),
                      pl.BlockSpec((None, 1, tk * nh),
                                   lambda s, c, need: (s, 0, newest_first(c, need[s]))),
                      pl.BlockSpec((nl, w), new),
                      pl.BlockSpec((nl, w), new),
                      pl.BlockSpec((None, 1, rows), lambda s, c, need: (s, 0, 0))],
            out_specs=pl.BlockSpec((nl, w), new),
            scratch_shapes=[pltpu.VMEM((rows, dh), BF16), pltpu.VMEM((rows, 1), F32),
                            pltpu.VMEM((rows, 1), F32), pltpu.VMEM((rows, dh), F32),
                            pltpu.VMEM((1, tk * nh), F32)]),
        out_shape=jax.ShapeDtypeStruct(qb.shape, BF16),
        compiler_params=_cp("parallel", "arbitrary"),
        name="attn_sample",
    )(need, qb, cache_k.reshape(b, past * nh, dh), cache_v.reshape(b, past * nh, dh),
      cache_lf.reshape(b, 1, past * nh), kb, vb, lfn)


def _residual_mod(x, gate, o, g2, sh2, sc2, x_ref, h_ref):
    xn = x + gate * o
    x_ref[...] = xn
    h_ref[...] = _modulate(xn, g2, sh2, sc2).astype(h_ref.dtype)


def _merge_kernel(a_ref, b_ref, w_ref, x_ref, gate_ref, g2_ref, sh2_ref, sc2_ref, xo_ref, h_ref):
    ca = a_ref.shape[1]
    o = _dot(a_ref[...], w_ref[:ca, :]) + _dot(b_ref[...], w_ref[ca:, :])
    _residual_mod(x_ref[...], gate_ref[...], o, g2_ref[...], sh2_ref[...], sc2_ref[...],
                  xo_ref, h_ref)


def _merge(a, b, w_out, x, gate, g2, sh2, sc2, *, tm, per_row):
    t, d = x.shape
    ca = a.shape[1]
    row = lambda i: (i, 0)
    mod = _mod_spec(per_row, tm, d, False)
    return pl.pallas_call(
        _merge_kernel,
        grid=(t // tm,),
        in_specs=[pl.BlockSpec((tm, ca), row), pl.BlockSpec((tm, ca), row),
                  pl.BlockSpec(w_out.shape, lambda i: (0, 0)),
                  pl.BlockSpec((tm, d), row), mod,
                  pl.BlockSpec((1, d), lambda i: (0, 0)), mod, mod],
        out_specs=[pl.BlockSpec((tm, d), row), pl.BlockSpec((tm, d), row)],
        out_shape=[jax.ShapeDtypeStruct((t, d), F32), jax.ShapeDtypeStruct((t, d), BF16)],
        compiler_params=_cp("parallel"),
        name="merge",
    )(a, b, w_out, x, gate, g2, sh2, sc2)


def _uv_kernel(x_ref, g_ref, sh_ref, sc_ref, w_ref, lng_ref, lnb_ref, u_ref, vn_ref, h_scr):
    j = pl.program_id(1)

    @pl.when(j == 0)
    def _():
        h_scr[...] = _modulate(x_ref[...], g_ref[...], sh_ref[...], sc_ref[...]).astype(BF16)

    z = _gelu(_dot(h_scr[...], w_ref[...]))

    @pl.when(j == 0)
    def _():
        u_ref[...] = z.astype(u_ref.dtype)

    @pl.when(j == 1)
    def _():
        mu = jnp.mean(z, axis=-1, keepdims=True)
        zc = z - mu
        var = jnp.mean(zc * zc, axis=-1, keepdims=True)
        vn_ref[...] = (zc * lax.rsqrt(var + EPS) * lng_ref[...] + lnb_ref[...]).astype(vn_ref.dtype)


def _uv(x, g, shift, scale, w_uv, ln_g, ln_b, *, tm, per_row, vn_dtype):
    t, d = x.shape
    sw = w_uv.shape[1] // 2
    row = lambda i, j: (i, 0)
    const = lambda i, j: (0, 0)
    return pl.pallas_call(
        _uv_kernel,
        grid=(t // tm, 2),
        in_specs=[pl.BlockSpec((tm, d), row), pl.BlockSpec((1, d), const),
                  _mod_spec(per_row, tm, d, True), _mod_spec(per_row, tm, d, True),
                  pl.BlockSpec((d, sw), lambda i, j: (0, j)),
                  pl.BlockSpec((1, sw), const), pl.BlockSpec((1, sw), const)],
        out_specs=[pl.BlockSpec((tm, sw), row), pl.BlockSpec((tm, sw), row)],
        out_shape=[jax.ShapeDtypeStruct((t, sw), BF16), jax.ShapeDtypeStruct((t, sw), vn_dtype)],
        scratch_shapes=[pltpu.VMEM((tm, d), BF16)],
        compiler_params=_cp("parallel", "arbitrary"),
        name="sg_uv",
    )(x, g, shift, scale, w_uv, ln_g, ln_b)


def _sgate_kernel(u_ref, vn_ref, ws_ref, bs_ref, w_ref, x_ref, gate_ref, g2_ref, sh2_ref, sc2_ref,
                  xo_ref, h_ref, y_scr):
    tm, sw = u_ref.shape
    gw = sw // SG_GROUPS
    for c in range(tm // SG_CHUNK):
        rs = slice(c * SG_CHUNK, (c + 1) * SG_CHUNK)
        for g in range(SG_GROUPS):
            cs = slice(g * gw, (g + 1) * gw)
            sv = _dot(ws_ref[g], vn_ref[rs, cs].astype(BF16)) + bs_ref[:, g:g + 1]
            y_scr[rs, cs] = (u_ref[rs, cs].astype(F32) * sv).astype(BF16)
    o = _dot(y_scr[...], w_ref[...])
    _residual_mod(x_ref[...], gate_ref[...], o, g2_ref[...], sh2_ref[...], sc2_ref[...],
                  xo_ref, h_ref)


def _sgate(u, vn, ws, bs_t, w_out, x, gate, g2, sh2, sc2, *, tm, per_row):
    t, d = x.shape
    sw = u.shape[1]
    row = lambda i: (i, 0)
    mod = _mod_spec(per_row, tm, d, False)
    return pl.pallas_call(
        _sgate_kernel,
        grid=(t // tm,),
        in_specs=[pl.BlockSpec((tm, sw), row), pl.BlockSpec((tm, sw), row),
                  pl.BlockSpec(ws.shape, lambda i: (0, 0, 0)),
                  pl.BlockSpec(bs_t.shape, lambda i: (0, 0)),
                  pl.BlockSpec(w_out.shape, lambda i: (0, 0)),
                  pl.BlockSpec((tm, d), row), mod,
                  pl.BlockSpec((1, d), lambda i: (0, 0)), mod, mod],
        out_specs=[pl.BlockSpec((tm, d), row), pl.BlockSpec((tm, d), row)],
        out_shape=[jax.ShapeDtypeStruct((t, d), F32), jax.ShapeDtypeStruct((t, d), BF16)],
        scratch_shapes=[pltpu.VMEM((tm, sw), BF16)],
        compiler_params=_cp("parallel"),
        name="sg_gate",
    )(u, vn, ws, bs_t, w_out, x, gate, g2, sh2, sc2)


GROUPS = 8
ROUTE_TOKENS = GROUPS * LANES
PAIR_R1 = [0] * 16 + [a for a in range(1, 8) for _ in range(8)] + list(range(8, 16))
PAIR_R2 = list(range(16)) + [i for _ in range(1, 8) for i in range(8)] + [0] * 8
PAIR_FLAT = [a * PEER_TOPK + b for a, b in zip(PAIR_R1, PAIR_R2)]


def _extract_topk(problems, vals_refs, idx_refs):
    chains = 4

    def one_round(r, prev):
        picked = []
        for p, (ref, ids) in enumerate(problems):
            n = len(ids)
            per = -(-n // chains)
            best = []
            for c0 in range(0, n, per):
                m = ix = None
                for e in range(c0, min(c0 + per, n)):
                    x = jnp.where(prev[p] == float(ids[e]), -jnp.inf, ref[e])
                    ref[e] = x
                    if m is None:
                        m, ix = x, jnp.full(x.shape, float(ids[e]), F32)
                    else:
                        ix = jnp.where(x > m, float(ids[e]), ix)
                        m = jnp.maximum(m, x)
                best.append((m, ix))
            m, ix = best[0]
            for mc, ic in best[1:]:
                ix = jnp.where(mc > m, ic, ix)
                m = jnp.maximum(m, mc)
            vals_refs[p][r] = m
            idx_refs[p][r] = ix
            picked.append(ix)
        return tuple(picked)

    none = jnp.full((GROUPS, LANES), -1.0, F32)
    return lax.fori_loop(0, PEER_TOPK, one_round, tuple(none for _ in problems))


def _peer_route_kernel(h_ref, wq_ref, kexp_ref, r2_ref, e2_ref, c_ref, e1_ref,
                       qt_scr, s_scr, so_scr, vals_scr, idx_scr, cand_scr, cval_scr, cidx_scr,
                       t2_scr):
    hd = pl.program_id(1)

    @pl.when(hd == 0)
    def _():
        qt_scr[...] = _dot_nt(wq_ref[...], h_ref[...]).astype(BF16)

    row = pl.multiple_of(hd * 2 * PEER_KEYS, 2 * PEER_KEYS)
    for side in range(2):
        q2 = jnp.concatenate(
            [qt_scr[pl.ds(row + side * PEER_KEYS, PEER_KEYS), g * LANES:(g + 1) * LANES]
             for g in range(GROUPS)], axis=0)
        s = _dot(kexp_ref[side], q2).reshape(PEER_KEYS, GROUPS, LANES)
        s_scr[side] = s
        so_scr[side] = s

    keys = list(range(PEER_KEYS))
    _extract_topk([(s_scr.at[0], keys), (s_scr.at[1], keys)],
                  [vals_scr.at[0], vals_scr.at[1]], [idx_scr.at[0], idx_scr.at[1]])
    v1 = [vals_scr[0, r] for r in range(PEER_TOPK)]
    v2 = [vals_scr[1, r] for r in range(PEER_TOPK)]
    ncand = len(PAIR_FLAT)
    for c in range(ncand):
        cand_scr[c] = v1[PAIR_R1[c]] + v2[PAIR_R2[c]]
    (last,) = _extract_topk([(cand_scr, PAIR_FLAT)], [cval_scr], [cidx_scr])

    top = v1[0] + v2[0]
    z = jnp.zeros((GROUPS, LANES), F32)
    for r in range(PEER_TOPK):
        z = z + jnp.exp(cval_scr[r] - top)
    inv_z = 1.0 / z
    cnt = [jnp.zeros((GROUPS, LANES), F32) for _ in range(PEER_TOPK)]
    for c in range(ncand):
        picked = jnp.where(cand_scr[c] == -jnp.inf, 1.0,
                           jnp.where(last == float(PAIR_FLAT[c]), 1.0, 0.0))
        cnt[PAIR_R1[c]] = cnt[PAIR_R1[c]] + picked
    idx1 = [idx_scr[0, r] for r in range(PEER_TOPK)]
    idx2 = [idx_scr[1, r] for r in range(PEER_TOPK)]

    def per_key(k, _):
        kf = lax.convert_element_type(k, F32)
        rank2 = jnp.full((GROUPS, LANES), RANK_NONE, F32)
        ck = jnp.zeros((GROUPS, LANES), F32)
        for r in range(PEER_TOPK):
            rank2 = jnp.where(idx2[r] == kf, float(r), rank2)
            ck = jnp.where(idx1[r] == kf, cnt[r], ck)
        base = pl.multiple_of(k * GROUPS, GROUPS)
        t2_scr[0, pl.ds(base, GROUPS), :] = rank2
        t2_scr[1, pl.ds(base, GROUPS), :] = jnp.exp(so_scr[1, k] - v2[0])
        c_ref[k] = ck
        e1_ref[k] = jnp.exp(so_scr[0, k] - v1[0]) * inv_z
        return 0

    lax.fori_loop(0, PEER_KEYS, per_key, 0)
    for g in range(GROUPS):
        cols = slice(g * LANES, (g + 1) * LANES)
        r2_ref[:, cols] = t2_scr[0, pl.ds(g, PEER_KEYS, stride=GROUPS), :].astype(BF16)
        e2_ref[:, cols] = t2_scr[1, pl.ds(g, PEER_KEYS, stride=GROUPS), :].astype(BF16)


def _peer_route(h, wq_t, kexp):
    t, d = h.shape
    nblk = t // ROUTE_TOKENS
    nq = wq_t.shape[0]
    std = jax.ShapeDtypeStruct((PEER_HEADS, PEER_KEYS, t), BF16)
    grp = jax.ShapeDtypeStruct((PEER_HEADS, PEER_KEYS, t // LANES, LANES), F32)
    std_spec = pl.BlockSpec((None, PEER_KEYS, ROUTE_TOKENS), lambda i, hd: (hd, 0, i))
    grp_spec = pl.BlockSpec((None, PEER_KEYS, GROUPS, LANES), lambda i, hd: (hd, 0, i, 0))
    tile = (GROUPS, LANES)
    return pl.pallas_call(
        _peer_route_kernel,
        grid=(nblk, PEER_HEADS),
        in_specs=[pl.BlockSpec((ROUTE_TOKENS, d), lambda i, hd: (i, 0)),
                  pl.BlockSpec(wq_t.shape, lambda i, hd: (0, 0)),
                  pl.BlockSpec((None,) + kexp.shape[1:], lambda i, hd: (hd, 0, 0, 0))],
        out_specs=[std_spec, std_spec, grp_spec, grp_spec],
        out_shape=[std, std, grp, grp],
        scratch_shapes=[pltpu.VMEM((nq, ROUTE_TOKENS), BF16),
                        pltpu.VMEM((2, PEER_KEYS) + tile, F32),
                        pltpu.VMEM((2, PEER_KEYS) + tile, F32),
                        pltpu.VMEM((2, PEER_TOPK) + tile, F32),
                        pltpu.VMEM((2, PEER_TOPK) + tile, F32),
                        pltpu.VMEM((len(PAIR_FLAT),) + tile, F32),
                        pltpu.VMEM((PEER_TOPK,) + tile, F32),
                        pltpu.VMEM((PEER_TOPK,) + tile, F32),
                        pltpu.VMEM((2, PEER_KEYS * GROUPS, LANES), F32)],
        compiler_params=_cp("parallel", "arbitrary"),
        name="peer_route",
    )(h, wq_t, kexp)


def _peer_dense_kernel(h_ref, u_ref, v_ref, r2_ref, e2_ref, c_ref, e1_ref, o_ref):
    e = pl.program_id(1)
    tb = h_ref.shape[0]
    sub = r2_ref.shape[2]

    @pl.when(e == 0)
    def _():
        o_ref[...] = jnp.zeros_like(o_ref)

    tw = min(tb, PEER_UNIT_TOKENS)
    per = PEER_SUB // PEER_KEYS
    ngrp = tw // LANES
    goff = 0 if tb == ROUTE_TOKENS else (pl.program_id(0) % (ROUTE_TOKENS // tb)) * ngrp
    units = [(q, tp) for q in range(u_ref.shape[0] // PEER_SUB) for tp in range(tb // tw)]

    def token_rows(ref, h, jj, tp):
        row = jnp.concatenate(
            [ref[h, jj, pl.ds(goff + tp * ngrp + g, 1), :] for g in range(ngrp)], axis=1)
        return jnp.broadcast_to(row, (sub, tw)).astype(BF16)[None]

    def scores(unit):
        q, tp = unit
        return _dot_nt(u_ref[q * PEER_SUB:(q + 1) * PEER_SUB, :],
                       h_ref[tp * tw:(tp + 1) * tw, :])

    accs = [o_ref[:, tp * tw:(tp + 1) * tw] for tp in range(tb // tw)]
    at_next = scores(units[0])
    for n, (q, tp) in enumerate(units):
        at = at_next
        if n + 1 < len(units):
            at_next = scores(units[n + 1])
        cols = slice(tp * tw, (tp + 1) * tw)
        gs = []
        for jj in range(q * per, (q + 1) * per):
            w = jnp.zeros((PEER_KEYS // sub, sub, tw), BF16)
            for h in range(PEER_HEADS):
                picked = jnp.where(r2_ref[h, :, :, cols] < token_rows(c_ref, h, jj, tp),
                                   e2_ref[h, :, :, cols], jnp.zeros_like(w))
                w = w + picked * token_rows(e1_ref, h, jj, tp)
            lo = (jj - q * per) * PEER_KEYS
            a = _gelu(at[lo:lo + PEER_KEYS, :]).astype(BF16)
            gs.append(a * w.reshape(PEER_KEYS, tw))
        accs[tp] = accs[tp] + _dot(v_ref[:, q * PEER_SUB:(q + 1) * PEER_SUB],
                                   jnp.concatenate(gs, axis=0))
    for tp, acc in enumerate(accs):
        o_ref[:, tp * tw:(tp + 1) * tw] = acc


def _peer_dense(h, u_all, v_all, layer, r2, e2, c, e1, *, tb, te):
    t, d = h.shape
    ne = u_all.shape[1]
    sub = 16
    tr = r2.shape[2]
    r2 = r2.reshape(PEER_HEADS, PEER_KEYS // sub, sub, tr)
    e2 = e2.reshape(PEER_HEADS, PEER_KEYS // sub, sub, tr)
    nchunk = te // PEER_KEYS
    per_route = ROUTE_TOKENS // tb
    once = pl.Buffered(1)
    full = pl.BlockSpec((PEER_HEADS, PEER_KEYS // sub, sub, tb), lambda i, e: (0, 0, 0, i),
                        pipeline_mode=once)
    rows = pl.BlockSpec((PEER_HEADS, nchunk, GROUPS, LANES),
                        lambda i, e: (0, e, i // per_route, 0))
    return pl.pallas_call(
        _peer_dense_kernel,
        grid=(t // tb, ne // te),
        in_specs=[pl.BlockSpec((tb, d), lambda i, e: (i, 0), pipeline_mode=once),
                  pl.BlockSpec((None, te, d), lambda i, e: (layer, e, 0)),
                  pl.BlockSpec((None, d, te), lambda i, e: (layer, 0, e)),
                  full, full, rows, rows],
        out_specs=pl.BlockSpec((d, tb), lambda i, e: (0, i)),
        out_shape=jax.ShapeDtypeStruct((d, t), F32),
        compiler_params=_cp("parallel", "arbitrary"),
        name="peer_dense",
    )(h, u_all, v_all, r2, e2, c, e1)


def _peer_out_kernel(ot_ref, x_ref, gate_ref, xo_ref):
    xo_ref[...] = x_ref[...] + gate_ref[...] * ot_ref[...].T


def _peer_out(ot, x, gate, *, tm, per_row):
    t, d = x.shape
    row = lambda i: (i, 0)
    return pl.pallas_call(
        _peer_out_kernel,
        grid=(t // tm,),
        in_specs=[pl.BlockSpec((d, tm), lambda i: (0, i)), pl.BlockSpec((tm, d), row),
                  _mod_spec(per_row, tm, d, False)],
        out_specs=pl.BlockSpec((tm, d), row),
        out_shape=jax.ShapeDtypeStruct((t, d), F32),
        compiler_params=_cp("parallel"),
        name="peer_out",
    )(ot, x, gate)


def _peer(h, x, gate, wq_t, kexp, u_all, v_all, layer, *, tb, tm, per_row):
    t = h.shape[0]
    h_route = jnp.pad(h, ((0, -t % ROUTE_TOKENS), (0, 0)))
    r2, e2, c, e1 = _peer_route(h_route, wq_t, kexp)
    ot = _peer_dense(h, u_all, v_all, layer, r2, e2, c, e1, tb=tb, te=PEER_EXPERT_BLOCK)
    return _peer_out(ot, x, gate, tm=tm, per_row=per_row)


def kernel(x_prompt, x_sample, c_prompt, c_sample, cache_pool, cache_k, cache_v, cache_logf, w_ada, b_ada, norm1, norm2, w_in_ab, b_f, w_pool, pool_scale, q_gain, k_gain, w_out_ab, w_uv, sg_ln_g, sg_ln_b, w_s, b_s, w_out_sg, peer_wq, peer_keys, peer_u, peer_v):
    nb, seq, d = x_prompt.shape
    db, dl, _ = x_sample.shape
    depth = w_ada.shape[0]
    past = cache_k.shape[2]
    pool_hist = cache_pool.shape[2]
    assert nb == 1 and db * dl == LANES and dl == 16 and pool_hist == dl - 1
    fox_w = FOX_HEADS * HEAD_DIM
    pool_w = w_pool.shape[1] * w_pool.shape[2]
    ts = db * dl

    c_rows = jnp.concatenate(
        [c_prompt, c_sample, jnp.zeros((16 - nb - db, d), F32)], axis=0)
    mods = _ada(c_rows, w_ada, b_ada)

    def mod_p(layer, k):
        return mods[layer, 0:1, k * d:(k + 1) * d]

    def mod_s(layer, k):
        return jnp.repeat(mods[layer, 1:1 + db, k * d:(k + 1) * d], dl, axis=0)

    xp = x_prompt.reshape(seq, d)
    xs = x_sample.reshape(ts, d)
    tm_p = 512 if seq % 512 == 0 else 256
    outs = {k: [] for k in ("pool_p", "pool_s", "kp", "ks", "vp", "vs", "lfp", "lfs", "sgv")}
    u_all = peer_u.astype(BF16)
    v_all = jnp.swapaxes(peer_v, 1, 2).astype(BF16)

    for layer in range(depth):
        j = layer // 2
        g1 = norm1[layer].reshape(1, d)
        g2 = norm2[layer].reshape(1, d)
        if layer % 2 == 0:
            w_in = w_in_ab[j]
            w_main = w_in[:, :pool_w + 3 * fox_w].astype(BF16)
            wf_t = w_in[:, pool_w + 3 * fox_w:].T.astype(BF16)
            bfc = b_f[j].reshape(FOX_HEADS, 1)
            qg = q_gain[j].reshape(1, fox_w)
            kg = k_gain[j].reshape(1, fox_w)
            wp = w_pool[j].astype(BF16)
            psc = pool_scale[j].reshape(1, pool_w)
            wo = w_out_ab[j].astype(BF16)

            p, qb, k, kb, v, vb, lf, fc = _proj(
                xp, g1, mod_p(layer, 0), mod_p(layer, 1), w_main, wf_t, bfc, qg, kg,
                tm=tm_p, per_row=False, do_cumsum=True)
            a_out = _pool_prompt(p, wp, psc, tm=256)
            score_bound = 1.02 * (HEAD_DIM ** 0.5) * (jnp.max(jnp.abs(q_gain[j]), axis=-1)
                                                      * jnp.max(jnp.abs(k_gain[j]), axis=-1))
            b_out = _attn_prompt(qb, kb, vb, fc, score_bound, tq=tm_p)
            xp, hp = _merge(a_out, b_out, wo, xp, mod_p(layer, 2), g2, mod_p(layer, 3),
                            mod_p(layer, 4), tm=256, per_row=False)
            outs["pool_p"].append(p[-pool_hist:].reshape(nb, pool_hist, pool_w))
            outs["kp"].append(k.reshape(nb, seq, FOX_HEADS, HEAD_DIM))
            outs["vp"].append(v.reshape(nb, seq, FOX_HEADS, HEAD_DIM))
            outs["lfp"].append(lf.T.reshape(nb, seq, FOX_HEADS))

            p, qb, k, kb, v, vb, lf, _ = _proj(
                xs, g1, mod_s(layer, 0), mod_s(layer, 1), w_main, wf_t, bfc, qg, kg,
                tm=ts, per_row=True, do_cumsum=False)
            p3 = p.reshape(db, dl, pool_w)
            buf = jnp.concatenate([jnp.zeros((db, 1, pool_w), F32), cache_pool[j], p3], axis=1)
            a_out = _pool_sample(buf.reshape(db * 2 * dl, pool_w), p, wp, psc, pos0=past)
            b_out = _attn_sample(qb, kb, vb, lf, cache_k[j], cache_v[j], cache_logf[j],
                                 score_bound, tk=min(past, SAMPLE_KEY_CHUNK))
            xs, hs = _merge(a_out, b_out, wo, xs, mod_s(layer, 2), g2, mod_s(layer, 3),
                            mod_s(layer, 4), tm=ts, per_row=True)
            outs["pool_s"].append(jnp.concatenate([cache_pool[j], p3], axis=1)[:, -pool_hist:])
            outs["ks"].append(k.reshape(db, dl, FOX_HEADS, HEAD_DIM))
            outs["vs"].append(v.reshape(db, dl, FOX_HEADS, HEAD_DIM))
            outs["lfs"].append(lf.T.reshape(db, dl, FOX_HEADS))
        else:
            wuv = w_uv[j].astype(BF16)
            lng = sg_ln_g[j].reshape(1, -1)
            lnb = sg_ln_b[j].reshape(1, -1)
            wo = w_out_sg[j].astype(BF16)
            tri = jnp.tril(jnp.ones((SG_CHUNK, SG_CHUNK), bool))
            ws_p = jnp.where(tri, w_s[j], 0.0).astype(BF16)
            bs_p = b_s[j].T
            ws16 = jnp.where(tri[:dl, :dl], w_s[j][:, :dl, :dl], 0.0)
            ws_s = jnp.einsum("ab,gts->gatbs", jnp.eye(db, dtype=F32), ws16)
            ws_s = ws_s.reshape(SG_GROUPS, ts, ts).astype(BF16)
            bs_s = jnp.tile(b_s[j][:, :dl], (1, db)).T

            u, vn = _uv(xp, g1, mod_p(layer, 0), mod_p(layer, 1), wuv, lng, lnb,
                        tm=tm_p, per_row=False, vn_dtype=BF16)
            xp, hp = _sgate(u, vn, ws_p, bs_p, wo, xp, mod_p(layer, 2), g2, mod_p(layer, 3),
                            mod_p(layer, 4), tm=256, per_row=False)
            u, vn = _uv(xs, g1, mod_s(layer, 0), mod_s(layer, 1), wuv, lng, lnb,
                        tm=ts, per_row=True, vn_dtype=F32)
            xs, hs = _sgate(u, vn, ws_s, bs_s, wo, xs, mod_s(layer, 2), g2, mod_s(layer, 3),
                            mod_s(layer, 4), tm=ts, per_row=True)
            outs["sgv"].append(vn.reshape(db, dl, -1))

        wq_t = peer_wq[layer].T.astype(BF16)
        kexp = jnp.einsum("hskd,gj->hskgjd", peer_keys[layer].astype(BF16),
                          jnp.eye(GROUPS, dtype=BF16))
        kexp = kexp.reshape(PEER_HEADS, 2, PEER_KEYS * GROUPS, GROUPS * kexp.shape[-1])
        xp = _peer(hp, xp, mod_p(layer, 5), wq_t, kexp, u_all, v_all, layer,
                   tb=PEER_TOKEN_BLOCK if seq % PEER_TOKEN_BLOCK == 0 else 256, tm=256,
                   per_row=False)
        xs = _peer(hs, xs, mod_s(layer, 5), wq_t, kexp, u_all, v_all, layer,
                   tb=ts, tm=ts, per_row=True)

    st = lambda key: jnp.stack(outs[key])
    return (xp.reshape(nb, seq, d), xs.reshape(db, dl, d), st("pool_p"), st("pool_s"),
            st("kp"), st("ks"), st("vp"), st("vs"), st("lfp"), st("lfs"), st("sgv"))
```

```python
import functools

import jax
import jax.numpy as jnp
from jax import lax
from jax.experimental import pallas as pl
from jax.experimental.pallas import tpu as pltpu

F32 = jnp.float32
BF16 = jnp.bfloat16
EPS = 1e-6

LANES = 128
POOL_WINDOWS = (2, 4, 8, 16)
POOL_HALO = 128
FOX_HEADS = 8
HEAD_DIM = 128
SAMPLE_KEY_CHUNK = 512
SG_GROUPS = 8
SG_CHUNK = 128
PEER_HEADS = 8
PEER_KEYS = 128
PEER_TOPK = 16
PEER_TOKEN_BLOCK = 512
PEER_EXPERT_BLOCK = 1024
PEER_SUB = 256
PEER_UNIT_TOKENS = 1024
NEG_BIG = -1e30
EXP_ZERO_BELOW = 110.0
RANK_NONE = 64.0
VMEM_LIMIT = 56 * 1024 * 1024

NT_DIMS = (((1,), (1,)), ((), ()))


def _cp(*sem):
    return pltpu.CompilerParams(dimension_semantics=sem, vmem_limit_bytes=VMEM_LIMIT)


def _dot(a, b):
    return jnp.dot(a, b, preferred_element_type=F32)


def _dot_nt(a, b):
    return lax.dot_general(a, b, NT_DIMS, preferred_element_type=F32)


def _modulate(x, g, shift, scale):
    ms = jnp.mean(x * x, axis=-1, keepdims=True)
    return x * lax.rsqrt(ms + EPS) * g * (1.0 + scale) + shift


def _gelu(x):
    c = 0.7978845608028654
    half = 0.5 * x
    return half + half * jnp.tanh(x * (c + (c * 0.044715) * (x * x)))


def _log_sigmoid(x):
    return -(jnp.maximum(-x, 0.0) + jnp.log(1.0 + jnp.exp(-jnp.abs(x))))


def _lane_cumsum(x):
    n = x.shape[-1]
    lane = lax.broadcasted_iota(jnp.int32, x.shape, x.ndim - 1)
    s = 1
    while s < n:
        x = x + jnp.where(lane >= s, pltpu.roll(x, s, x.ndim - 1), 0.0)
        s *= 2
    return x


def _ada_kernel(c_ref, w_ref, b_ref, o_ref):
    c = c_ref[...]
    a = (c * (1.0 / (1.0 + jnp.exp(-c)))).astype(BF16)
    o_ref[0] = _dot(a, w_ref[0].astype(BF16)) + b_ref[0]


def _ada(c_rows, w_ada, b_ada):
    nl, d, n = w_ada.shape
    r = c_rows.shape[0]
    tn = 1024
    return pl.pallas_call(
        _ada_kernel,
        grid=(nl, n // tn),
        in_specs=[pl.BlockSpec((r, d), lambda l, j: (0, 0)),
                  pl.BlockSpec((1, d, tn), lambda l, j: (l, 0, j)),
                  pl.BlockSpec((1, 1, tn), lambda l, j: (l, 0, j))],
        out_specs=pl.BlockSpec((1, r, tn), lambda l, j: (l, 0, j)),
        out_shape=jax.ShapeDtypeStruct((nl, r, n), F32),
        compiler_params=_cp("parallel", "parallel"),
        name="ada",
    )(c_rows, w_ada, b_ada.reshape(nl, 1, n))


def _mod_spec(per_row, tm, d, nj):
    if nj:
        return (pl.BlockSpec((tm, d), lambda i, j: (i, 0)) if per_row
                else pl.BlockSpec((1, d), lambda i, j: (0, 0)))
    return (pl.BlockSpec((tm, d), lambda i: (i, 0)) if per_row
            else pl.BlockSpec((1, d), lambda i: (0, 0)))


def _head_rms(z, gain):
    outs = []
    for h in range(FOX_HEADS):
        zh = z[:, h * HEAD_DIM:(h + 1) * HEAD_DIM]
        ms = jnp.mean(zh * zh, axis=-1, keepdims=True)
        outs.append(zh * lax.rsqrt(ms + EPS))
    return jnp.concatenate(outs, axis=-1) * gain


def _proj_kernel(x_ref, g_ref, sh_ref, sc_ref, w_ref, wf_ref, bf_ref, qg_ref, kg_ref,
                 p_ref, q_ref, k_ref, kb_ref, v_ref, vb_ref, lf_ref, fc_ref,
                 h_scr, carry_scr, *, do_cumsum):
    i = pl.program_id(0)
    j = pl.program_id(1)

    @pl.when(j == 0)
    def _():
        hb = _modulate(x_ref[...], g_ref[...], sh_ref[...], sc_ref[...]).astype(BF16)
        h_scr[...] = hb
        lf = _log_sigmoid(_dot_nt(wf_ref[...], hb) + bf_ref[...])
        lf_ref[...] = lf
        if do_cumsum:
            @pl.when(i == 0)
            def _():
                carry_scr[...] = jnp.zeros_like(carry_scr)
            fc = _lane_cumsum(lf) + carry_scr[:, 0:1]
            fc_ref[...] = fc
            carry_scr[...] = jnp.broadcast_to(fc[:, -1:], carry_scr.shape)
        else:
            fc_ref[...] = lf

    z = _dot(h_scr[...], w_ref[...])

    @pl.when(j == 0)
    def _():
        p_ref[...] = z

    @pl.when(j == 1)
    def _():
        q_ref[...] = (_head_rms(z, qg_ref[...]) * (HEAD_DIM ** -0.5)).astype(BF16)

    @pl.when(j == 2)
    def _():
        kn = _head_rms(z, kg_ref[...])
        k_ref[...] = kn
        kb_ref[...] = kn.astype(BF16)

    @pl.when(j == 3)
    def _():
        v_ref[...] = z
        vb_ref[...] = z.astype(BF16)


def _proj(x, g, shift, scale, w_main, wf_t, b_f, q_gain, k_gain, *, tm, per_row, do_cumsum):
    t, d = x.shape
    w4 = w_main.shape[1] // 4
    nh = wf_t.shape[0]
    row = lambda i, j: (i, 0)
    const = lambda i, j: (0, 0)
    tok_f32 = jax.ShapeDtypeStruct((t, w4), F32)
    tok_b16 = jax.ShapeDtypeStruct((t, w4), BF16)
    head_t = jax.ShapeDtypeStruct((nh, t), F32)
    return pl.pallas_call(
        functools.partial(_proj_kernel, do_cumsum=do_cumsum),
        grid=(t // tm, 4),
        in_specs=[pl.BlockSpec((tm, d), row),
                  pl.BlockSpec((1, d), const),
                  _mod_spec(per_row, tm, d, True),
                  _mod_spec(per_row, tm, d, True),
                  pl.BlockSpec((d, w4), lambda i, j: (0, j)),
                  pl.BlockSpec((nh, d), const),
                  pl.BlockSpec((nh, 1), const),
                  pl.BlockSpec((1, w4), const),
                  pl.BlockSpec((1, w4), const)],
        out_specs=[pl.BlockSpec((tm, w4), row)] * 6
                  + [pl.BlockSpec((nh, tm), lambda i, j: (0, i))] * 2,
        out_shape=[tok_f32, tok_b16, tok_f32, tok_b16, tok_f32, tok_b16, head_t, head_t],
        scratch_shapes=[pltpu.VMEM((tm, d), BF16), pltpu.VMEM((nh, LANES), F32)],
        compiler_params=_cp("arbitrary", "arbitrary"),
        name="proj",
    )(x, g, shift, scale, w_main, wf_t, b_f, q_gain, k_gain)


def _pool_kernel(ext_ref, tok_ref, w_ref, ps_ref, o_ref, *, prompt, pos0):
    i = pl.program_id(0)
    tok = tok_ref[...]
    tm = tok.shape[0]
    if prompt:
        halo = jnp.where(i == 0, 0.0, ext_ref[...])
        ext = jnp.concatenate([halo, tok], axis=0)
    else:
        ext = ext_ref[...]
    ke = ext.shape[0]
    hi = ext.astype(BF16)
    lo = (ext - hi.astype(F32)).astype(BF16)
    m_io = lax.broadcasted_iota(jnp.int32, (tm, ke), 0)
    c_io = lax.broadcasted_iota(jnp.int32, (tm, ke), 1)
    r_io = lax.broadcasted_iota(jnp.int32, (tm, 1), 0)
    if prompt:
        tgt = m_io + POOL_HALO
        pos = pos0 + i * tm + r_io
    else:
        tgt = m_io + ((m_io >> 4) << 4) + 16
        pos = pos0 + (r_io & 15)
    dlt = tgt - c_io
    gw = w_ref.shape[1]
    for gi, win in enumerate(POOL_WINDOWS):
        sl = slice(gi * gw, (gi + 1) * gw)
        band = jnp.where(dlt >= 0, jnp.where(dlt < win, 1.0, 0.0), 0.0).astype(BF16)
        wsum = _dot(band, hi[:, sl]) + _dot(band, lo[:, sl])
        cnt = jnp.minimum(win, pos + 1).astype(F32)
        dd = wsum / cnt - tok[:, sl]
        o = _dot(dd.astype(BF16), w_ref[gi]) * ps_ref[:, sl]
        o_ref[:, sl] = o.astype(o_ref.dtype)


def _pool_prompt(p, w_pool, pool_scale, *, tm):
    t, c = p.shape
    per = tm // POOL_HALO
    return pl.pallas_call(
        functools.partial(_pool_kernel, prompt=True, pos0=0),
        grid=(t // tm,),
        in_specs=[pl.BlockSpec((POOL_HALO, c), lambda i: (jnp.maximum(i * per - 1, 0), 0)),
                  pl.BlockSpec((tm, c), lambda i: (i, 0)),
                  pl.BlockSpec(w_pool.shape, lambda i: (0, 0, 0)),
                  pl.BlockSpec((1, c), lambda i: (0, 0))],
        out_specs=pl.BlockSpec((tm, c), lambda i: (i, 0)),
        out_shape=jax.ShapeDtypeStruct((t, c), BF16),
        compiler_params=_cp("parallel"),
        name="pool_prompt",
    )(p, p, w_pool, pool_scale)


def _pool_sample(buf, p, w_pool, pool_scale, *, pos0):
    t, c = p.shape
    return pl.pallas_call(
        functools.partial(_pool_kernel, prompt=False, pos0=pos0),
        grid=(1,),
        in_specs=[pl.BlockSpec(buf.shape, lambda i: (0, 0)),
                  pl.BlockSpec((t, c), lambda i: (0, 0)),
                  pl.BlockSpec(w_pool.shape, lambda i: (0, 0, 0)),
                  pl.BlockSpec((1, c), lambda i: (0, 0))],
        out_specs=pl.BlockSpec((t, c), lambda i: (0, 0)),
        out_shape=jax.ShapeDtypeStruct((t, c), BF16),
        compiler_params=_cp("arbitrary"),
        name="pool_sample",
    )(buf, p, w_pool, pool_scale)


def _softmax_step(carry, s, v):
    m, l, acc = carry
    m_new = jnp.maximum(m, jnp.max(s, axis=-1, keepdims=True))
    alpha = jnp.exp(m - m_new)
    p = jnp.exp(s - m_new)
    l = alpha * l + jnp.sum(p, axis=-1, keepdims=True)
    acc = alpha * acc + _dot(p.astype(BF16), v)
    return m_new, l, acc


def _attn_prompt_kernel(fend_ref, fstart_ref, bound_ref, q_ref, k_ref, v_ref, fk_ref, o_ref,
                        *, tq):
    h = pl.program_id(0)
    i = pl.program_id(1)
    q = q_ref[...]

    def chunk(kj, carry, masked):
        off = pl.multiple_of(kj * tq, tq)
        s = _dot_nt(q, k_ref[pl.ds(off, tq), :]) - fk_ref[:, pl.ds(off, tq)]
        if masked:
            r_io = lax.broadcasted_iota(jnp.int32, (tq, tq), 0)
            c_io = lax.broadcasted_iota(jnp.int32, (tq, tq), 1)
            s = jnp.where(c_io <= r_io, s, NEG_BIG)
        return _softmax_step(carry, s, v_ref[pl.ds(off, tq), :])

    limit = fstart_ref[h, i] + bound_ref[h]
    dead = lax.fori_loop(
        0, i, lambda kj, n: n + jnp.where(fend_ref[h, kj] > limit, 1, 0), jnp.int32(0))
    init = (jnp.full((tq, 1), NEG_BIG, F32), jnp.zeros((tq, 1), F32),
            jnp.zeros((tq, HEAD_DIM), F32))
    carry = lax.fori_loop(dead, i, lambda kj, c: chunk(kj, c, False), init)
    _, l, acc = chunk(i, carry, True)
    o_ref[...] = (acc / l).astype(o_ref.dtype)


def _attn_prompt(qb, kb, vb, fcum, score_bound, *, tq):
    t, w = qb.shape
    nh = w // HEAD_DIM
    fend = fcum[:, tq - 1::tq]
    fstart = fcum[:, ::tq]
    bound = 2.0 * score_bound + EXP_ZERO_BELOW
    smem = pl.BlockSpec(memory_space=pltpu.SMEM)
    return pl.pallas_call(
        functools.partial(_attn_prompt_kernel, tq=tq),
        grid=(nh, t // tq),
        in_specs=[smem, smem, smem,
                  pl.BlockSpec((tq, HEAD_DIM), lambda h, i: (i, h)),
                  pl.BlockSpec((t, HEAD_DIM), lambda h, i: (0, h)),
                  pl.BlockSpec((t, HEAD_DIM), lambda h, i: (0, h)),
                  pl.BlockSpec((None, 1, t), lambda h, i: (h, 0, 0))],
        out_specs=pl.BlockSpec((tq, HEAD_DIM), lambda h, i: (i, h)),
        out_shape=jax.ShapeDtypeStruct((t, w), BF16),
        compiler_params=_cp("parallel", "parallel"),
        name="attn_prompt",
    )(fend, fstart, bound, qb, kb, vb, fcum.reshape(nh, 1, t))


def _strided_lane_cumsum(x, stride, seg=None):
    n = seg or x.shape[-1]
    lane = lax.broadcasted_iota(jnp.int32, x.shape, x.ndim - 1)
    pos = lane if seg is None else lane & (seg - 1)
    s = stride
    while s < n:
        x = x + jnp.where(pos >= s, pltpu.roll(x, s, x.ndim - 1), 0.0)
        s *= 2
    return x


def _tile_last_lanes(x, period):
    n = x.shape[-1]
    lane = lax.broadcasted_iota(jnp.int32, x.shape, x.ndim - 1)
    y = jnp.where(lane >= n - period, x, 0.0)
    s = period
    while s < n:
        y = y + pltpu.roll(y, n - s, x.ndim - 1)
        s *= 2
    return y


def _chunk_suffix_kernel(lf_ref, o_ref, *, tk):
    x = lf_ref[...]
    p = x.shape[1]
    pos = lax.broadcasted_iota(jnp.int32, (p, LANES), 0)
    col = lax.broadcasted_iota(jnp.int32, (p, LANES), 1)
    later = jnp.where(pos >= (col + 1) * tk, 1.0, 0.0).astype(BF16)
    hi = x.astype(BF16)
    r1 = x - hi.astype(F32)
    mid = r1.astype(BF16)
    lo = (r1 - mid.astype(F32)).astype(BF16)
    o_ref[...] = _dot(hi, later) + _dot(mid, later) + _dot(lo, later)


def _chunk_suffix(lf_rows, *, tk):
    r, p = lf_rows.shape
    return pl.pallas_call(
        functools.partial(_chunk_suffix_kernel, tk=tk),
        grid=(1,),
        in_specs=[pl.BlockSpec((r, p), lambda i: (0, 0))],
        out_specs=pl.BlockSpec((r, LANES), lambda i: (0, 0)),
        out_shape=jax.ShapeDtypeStruct((r, LANES), F32),
        compiler_params=_cp("arbitrary"),
        name="cache_decay",
    )(lf_rows)


def _attn_sample_kernel(need_ref, q_ref, ck_ref, cv_ref, clf_ref, kn_ref, vn_ref, lfn_ref, o_ref,
                        q_scr, m_scr, l_scr, acc_scr, car_scr):
    b = pl.program_id(0)
    c = pl.program_id(1)
    nl = q_ref.shape[0]
    rows = q_scr.shape[0]

    def stack_heads(x):
        return jnp.concatenate(
            [x[:, h * HEAD_DIM:(h + 1) * HEAD_DIM] for h in range(FOX_HEADS)], axis=0)

    @pl.when(c == 0)
    def _():
        qs = stack_heads(q_ref[...])
        q_scr[...] = qs
        kn = stack_heads(kn_ref[...])
        r_io = lax.broadcasted_iota(jnp.int32, (rows, rows), 0)
        c_io = lax.broadcasted_iota(jnp.int32, (rows, rows), 1)
        ncum = _strided_lane_cumsum(lfn_ref[...], 1, seg=nl)
        sn = jnp.where((r_io // nl) == (c_io // nl),
                       jnp.where((c_io % nl) <= (r_io % nl), _dot_nt(qs, kn) - ncum, NEG_BIG),
                       NEG_BIG)
        init = (jnp.full((rows, 1), NEG_BIG, F32), jnp.zeros((rows, 1), F32),
                jnp.zeros((rows, HEAD_DIM), F32))
        m_scr[...], l_scr[...], acc_scr[...] = _softmax_step(init, sn, stack_heads(vn_ref[...]))
        car_scr[...] = jnp.zeros_like(car_scr)

    @pl.when(c < need_ref[b])
    def _():
        n = clf_ref.shape[1]
        incl = _strided_lane_cumsum(clf_ref[...], FOX_HEADS)
        total = _tile_last_lanes(incl, FOX_HEADS)
        after = car_scr[...] + total - incl
        car_scr[...] = car_scr[...] + total
        r_io = lax.broadcasted_iota(jnp.int32, (rows, n), 0)
        c_io = lax.broadcasted_iota(jnp.int32, (rows, n), 1)
        s = _dot_nt(q_scr[...], ck_ref[...].astype(BF16)) + after
        s = jnp.where((c_io % FOX_HEADS) == (r_io // nl), s, NEG_BIG)
        carry = _softmax_step((m_scr[...], l_scr[...], acc_scr[...]), s, cv_ref[...].astype(BF16))
        m_scr[...], l_scr[...], acc_scr[...] = carry

    @pl.when(c == pl.num_programs(1) - 1)
    def _():
        o = acc_scr[...] / l_scr[...]
        o_ref[...] = jnp.concatenate(
            [o[h * nl:(h + 1) * nl, :] for h in range(FOX_HEADS)], axis=-1).astype(o_ref.dtype)


def _attn_sample(qb, kb, vb, lf_new, cache_k, cache_v, cache_lf, score_bound, *, tk):
    b, past, nh, dh = cache_k.shape
    w = nh * dh
    nl = qb.shape[0] // b
    rows = nh * nl
    nch = past // tk
    after = _chunk_suffix(jnp.transpose(cache_lf, (0, 2, 1)).reshape(b * nh, past), tk=tk)
    alive = (after.reshape(b, nh, LANES)[:, :, :nch]
             + 2.0 * score_bound[None, :, None]) >= -EXP_ZERO_BELOW
    need = jnp.maximum(jnp.max(jnp.sum(alive, axis=2), axis=1), 1).astype(jnp.int32)
    lfn = jnp.transpose(lf_new.reshape(nh, b, nl), (1, 0, 2)).reshape(b, 1, rows)

    def newest_first(c, need_s):
        return nch - 1 - jnp.minimum(c, need_s - 1)

    chunk = lambda s, c, need: (s, newest_first(c, need[s]), 0)
    new = lambda s, c, need: (s, 0)
    return pl.pallas_call(
        _attn_sample_kernel,
        grid_spec=pltpu.PrefetchScalarGridSpec(
            num_scalar_prefetch=1,
            grid=(b, nch),
            in_specs=[pl.BlockSpec((nl, w), new),
                      pl.BlockSpec((None, tk * nh, dh), chunk),
                      pl.BlockSpec((None, tk * nh, dh), chunk),
                      pl.BlockSpec((None, 1, tk * nh),
                                   lambda s, c, need: (s, 0, newest_first(c, need[s]))),
                      pl.BlockSpec((nl, w), new),
                      pl.BlockSpec((nl, w), new),
                      pl.BlockSpec((None, 1, rows), lambda s, c, need: (s, 0, 0))],
            out_specs=pl.BlockSpec((nl, w), new),
            scratch_shapes=[pltpu.VMEM((rows, dh), BF16), pltpu.VMEM((rows, 1), F32),
                            pltpu.VMEM((rows, 1), F32), pltpu.VMEM((rows, dh), F32),
                            pltpu.VMEM((1, tk * nh), F32)]),
        out_shape=jax.ShapeDtypeStruct(qb.shape, BF16),
        compiler_params=_cp("parallel", "arbitrary"),
        name="attn_sample",
    )(need, qb, cache_k.reshape(b, past * nh, dh), cache_v.reshape(b, past * nh, dh),
      cache_lf.reshape(b, 1, past * nh), kb, vb, lfn)


def _residual_mod(x, gate, o, g2, sh2, sc2, x_ref, h_ref):
    xn = x + gate * o
    x_ref[...] = xn
    h_ref[...] = _modulate(xn, g2, sh2, sc2).astype(h_ref.dtype)


def _merge_kernel(a_ref, b_ref, w_ref, x_ref, gate_ref, g2_ref, sh2_ref, sc2_ref, xo_ref, h_ref):
    ca = a_ref.shape[1]
    o = _dot(a_ref[...], w_ref[:ca, :]) + _dot(b_ref[...], w_ref[ca:, :])
    _residual_mod(x_ref[...], gate_ref[...], o, g2_ref[...], sh2_ref[...], sc2_ref[...],
                  xo_ref, h_ref)


def _merge(a, b, w_out, x, gate, g2, sh2, sc2, *, tm, per_row):
    t, d = x.shape
    ca = a.shape[1]
    row = lambda i: (i, 0)
    mod = _mod_spec(per_row, tm, d, False)
    return pl.pallas_call(
        _merge_kernel,
        grid=(t // tm,),
        in_specs=[pl.BlockSpec((tm, ca), row), pl.BlockSpec((tm, ca), row),
                  pl.BlockSpec(w_out.shape, lambda i: (0, 0)),
                  pl.BlockSpec((tm, d), row), mod,
                  pl.BlockSpec((1, d), lambda i: (0, 0)), mod, mod],
        out_specs=[pl.BlockSpec((tm, d), row), pl.BlockSpec((tm, d), row)],
        out_shape=[jax.ShapeDtypeStruct((t, d), F32), jax.ShapeDtypeStruct((t, d), BF16)],
        compiler_params=_cp("parallel"),
        name="merge",
    )(a, b, w_out, x, gate, g2, sh2, sc2)


def _uv_kernel(x_ref, g_ref, sh_ref, sc_ref, w_ref, lng_ref, lnb_ref, u_ref, vn_ref, h_scr):
    j = pl.program_id(1)

    @pl.when(j == 0)
    def _():
        h_scr[...] = _modulate(x_ref[...], g_ref[...], sh_ref[...], sc_ref[...]).astype(BF16)

    z = _gelu(_dot(h_scr[...], w_ref[...]))

    @pl.when(j == 0)
    def _():
        u_ref[...] = z.astype(u_ref.dtype)

    @pl.when(j == 1)
    def _():
        mu = jnp.mean(z, axis=-1, keepdims=True)
        zc = z - mu
        var = jnp.mean(zc * zc, axis=-1, keepdims=True)
        vn_ref[...] = (zc * lax.rsqrt(var + EPS) * lng_ref[...] + lnb_ref[...]).astype(vn_ref.dtype)


def _uv(x, g, shift, scale, w_uv, ln_g, ln_b, *, tm, per_row, vn_dtype):
    t, d = x.shape
    sw = w_uv.shape[1] // 2
    row = lambda i, j: (i, 0)
    const = lambda i, j: (0, 0)
    return pl.pallas_call(
        _uv_kernel,
        grid=(t // tm, 2),
        in_specs=[pl.BlockSpec((tm, d), row), pl.BlockSpec((1, d), const),
                  _mod_spec(per_row, tm, d, True), _mod_spec(per_row, tm, d, True),
                  pl.BlockSpec((d, sw), lambda i, j: (0, j)),
                  pl.BlockSpec((1, sw), const), pl.BlockSpec((1, sw), const)],
        out_specs=[pl.BlockSpec((tm, sw), row), pl.BlockSpec((tm, sw), row)],
        out_shape=[jax.ShapeDtypeStruct((t, sw), BF16), jax.ShapeDtypeStruct((t, sw), vn_dtype)],
        scratch_shapes=[pltpu.VMEM((tm, d), BF16)],
        compiler_params=_cp("parallel", "arbitrary"),
        name="sg_uv",
    )(x, g, shift, scale, w_uv, ln_g, ln_b)


def _sgate_kernel(u_ref, vn_ref, ws_ref, bs_ref, w_ref, x_ref, gate_ref, g2_ref, sh2_ref, sc2_ref,
                  xo_ref, h_ref, y_scr):
    tm, sw = u_ref.shape
    gw = sw // SG_GROUPS
    for c in range(tm // SG_CHUNK):
        rs = slice(c * SG_CHUNK, (c + 1) * SG_CHUNK)
        for g in range(SG_GROUPS):
            cs = slice(g * gw, (g + 1) * gw)
            sv = _dot(ws_ref[g], vn_ref[rs, cs].astype(BF16)) + bs_ref[:, g:g + 1]
            y_scr[rs, cs] = (u_ref[rs, cs].astype(F32) * sv).astype(BF16)
    o = _dot(y_scr[...], w_ref[...])
    _residual_mod(x_ref[...], gate_ref[...], o, g2_ref[...], sh2_ref[...], sc2_ref[...],
                  xo_ref, h_ref)


def _sgate(u, vn, ws, bs_t, w_out, x, gate, g2, sh2, sc2, *, tm, per_row):
    t, d = x.shape
    sw = u.shape[1]
    row = lambda i: (i, 0)
    mod = _mod_spec(per_row, tm, d, False)
    return pl.pallas_call(
        _sgate_kernel,
        grid=(t // tm,),
        in_specs=[pl.BlockSpec((tm, sw), row), pl.BlockSpec((tm, sw), row),
                  pl.BlockSpec(ws.shape, lambda i: (0, 0, 0)),
                  pl.BlockSpec(bs_t.shape, lambda i: (0, 0)),
                  pl.BlockSpec(w_out.shape, lambda i: (0, 0)),
                  pl.BlockSpec((tm, d), row), mod,
                  pl.BlockSpec((1, d), lambda i: (0, 0)), mod, mod],
        out_specs=[pl.BlockSpec((tm, d), row), pl.BlockSpec((tm, d), row)],
        out_shape=[jax.ShapeDtypeStruct((t, d), F32), jax.ShapeDtypeStruct((t, d), BF16)],
        scratch_shapes=[pltpu.VMEM((tm, sw), BF16)],
        compiler_params=_cp("parallel"),
        name="sg_gate",
    )(u, vn, ws, bs_t, w_out, x, gate, g2, sh2, sc2)


GROUPS = 8
ROUTE_TOKENS = GROUPS * LANES
PAIR_R1 = [0] * 16 + [a for a in range(1, 8) for _ in range(8)] + list(range(8, 16))
PAIR_R2 = list(range(16)) + [i for _ in range(1, 8) for i in range(8)] + [0] * 8
PAIR_FLAT = [a * PEER_TOPK + b for a, b in zip(PAIR_R1, PAIR_R2)]


def _extract_topk(problems, vals_refs, idx_refs):
    chains = 4

    def one_round(r, prev):
        picked = []
        for p, (ref, ids) in enumerate(problems):
            n = len(ids)
            per = -(-n // chains)
            best = []
            for c0 in range(0, n, per):
                m = ix = None
                for e in range(c0, min(c0 + per, n)):
                    x = jnp.where(prev[p] == float(ids[e]), -jnp.inf, ref[e])
                    ref[e] = x
                    if m is None:
                        m, ix = x, jnp.full(x.shape, float(ids[e]), F32)
                    else:
                        ix = jnp.where(x > m, float(ids[e]), ix)
                        m = jnp.maximum(m, x)
                best.append((m, ix))
            m, ix = best[0]
            for mc, ic in best[1:]:
                ix = jnp.where(mc > m, ic, ix)
                m = jnp.maximum(m, mc)
            vals_refs[p][r] = m
            idx_refs[p][r] = ix
            picked.append(ix)
        return tuple(picked)

    none = jnp.full((GROUPS, LANES), -1.0, F32)
    return lax.fori_loop(0, PEER_TOPK, one_round, tuple(none for _ in problems))


def _peer_route_kernel(h_ref, wq_ref, kexp_ref, r2_ref, e2_ref, c_ref, e1_ref,
                       qt_scr, s_scr, so_scr, vals_scr, idx_scr, cand_scr, cval_scr, cidx_scr,
                       t2_scr):
    hd = pl.program_id(1)

    @pl.when(hd == 0)
    def _():
        qt_scr[...] = _dot_nt(wq_ref[...], h_ref[...]).astype(BF16)

    row = pl.multiple_of(hd * 2 * PEER_KEYS, 2 * PEER_KEYS)
    for side in range(2):
        q2 = jnp.concatenate(
            [qt_scr[pl.ds(row + side * PEER_KEYS, PEER_KEYS), g * LANES:(g + 1) * LANES]
             for g in range(GROUPS)], axis=0)
        s = _dot(kexp_ref[side], q2).reshape(PEER_KEYS, GROUPS, LANES)
        s_scr[side] = s
        so_scr[side] = s

    keys = list(range(PEER_KEYS))
    _extract_topk([(s_scr.at[0], keys), (s_scr.at[1], keys)],
                  [vals_scr.at[0], vals_scr.at[1]], [idx_scr.at[0], idx_scr.at[1]])
    v1 = [vals_scr[0, r] for r in range(PEER_TOPK)]
    v2 = [vals_scr[1, r] for r in range(PEER_TOPK)]
    ncand = len(PAIR_FLAT)
    for c in range(ncand):
        cand_scr[c] = v1[PAIR_R1[c]] + v2[PAIR_R2[c]]
    (last,) = _extract_topk([(cand_scr, PAIR_FLAT)], [cval_scr], [cidx_scr])

    top = v1[0] + v2[0]
    z = jnp.zeros((GROUPS, LANES), F32)
    for r in range(PEER_TOPK):
        z = z + jnp.exp(cval_scr[r] - top)
    inv_z = 1.0 / z
    cnt = [jnp.zeros((GROUPS, LANES), F32) for _ in range(PEER_TOPK)]
    for c in range(ncand):
        picked = jnp.where(cand_scr[c] == -jnp.inf, 1.0,
                           jnp.where(last == float(PAIR_FLAT[c]), 1.0, 0.0))
        cnt[PAIR_R1[c]] = cnt[PAIR_R1[c]] + picked
    idx1 = [idx_scr[0, r] for r in range(PEER_TOPK)]
    idx2 = [idx_scr[1, r] for r in range(PEER_TOPK)]

    def per_key(k, _):
        kf = lax.convert_element_type(k, F32)
        rank2 = jnp.full((GROUPS, LANES), RANK_NONE, F32)
        ck = jnp.zeros((GROUPS, LANES), F32)
        for r in range(PEER_TOPK):
            rank2 = jnp.where(idx2[r] == kf, float(r), rank2)
            ck = jnp.where(idx1[r] == kf, cnt[r], ck)
        base = pl.multiple_of(k * GROUPS, GROUPS)
        t2_scr[0, pl.ds(base, GROUPS), :] = rank2
        t2_scr[1, pl.ds(base, GROUPS), :] = jnp.exp(so_scr[1, k] - v2[0])
        c_ref[k] = ck
        e1_ref[k] = jnp.exp(so_scr[0, k] - v1[0]) * inv_z
        return 0

    lax.fori_loop(0, PEER_KEYS, per_key, 0, unroll=4)
    for g in range(GROUPS):
        cols = slice(g * LANES, (g + 1) * LANES)
        r2_ref[:, cols] = t2_scr[0, pl.ds(g, PEER_KEYS, stride=GROUPS), :].astype(BF16)
        e2_ref[:, cols] = t2_scr[1, pl.ds(g, PEER_KEYS, stride=GROUPS), :].astype(BF16)


def _peer_route(h, wq_t, kexp):
    t, d = h.shape
    nblk = t // ROUTE_TOKENS
    nq = wq_t.shape[0]
    std = jax.ShapeDtypeStruct((PEER_HEADS, PEER_KEYS, t), BF16)
    grp = jax.ShapeDtypeStruct((PEER_HEADS, PEER_KEYS, t // LANES, LANES), F32)
    std_spec = pl.BlockSpec((None, PEER_KEYS, ROUTE_TOKENS), lambda i, hd: (hd, 0, i))
    grp_spec = pl.BlockSpec((None, PEER_KEYS, GROUPS, LANES), lambda i, hd: (hd, 0, i, 0))
    tile = (GROUPS, LANES)
    return pl.pallas_call(
        _peer_route_kernel,
        grid=(nblk, PEER_HEADS),
        in_specs=[pl.BlockSpec((ROUTE_TOKENS, d), lambda i, hd: (i, 0)),
                  pl.BlockSpec(wq_t.shape, lambda i, hd: (0, 0)),
                  pl.BlockSpec((None,) + kexp.shape[1:], lambda i, hd: (hd, 0, 0, 0))],
        out_specs=[std_spec, std_spec, grp_spec, grp_spec],
        out_shape=[std, std, grp, grp],
        scratch_shapes=[pltpu.VMEM((nq, ROUTE_TOKENS), BF16),
                        pltpu.VMEM((2, PEER_KEYS) + tile, F32),
                        pltpu.VMEM((2, PEER_KEYS) + tile, F32),
                        pltpu.VMEM((2, PEER_TOPK) + tile, F32),
                        pltpu.VMEM((2, PEER_TOPK) + tile, F32),
                        pltpu.VMEM((len(PAIR_FLAT),) + tile, F32),
                        pltpu.VMEM((PEER_TOPK,) + tile, F32),
                        pltpu.VMEM((PEER_TOPK,) + tile, F32),
                        pltpu.VMEM((2, PEER_KEYS * GROUPS, LANES), F32)],
        compiler_params=_cp("parallel", "arbitrary"),
        name="peer_route",
    )(h, wq_t, kexp)


def _peer_dense_kernel(h_ref, u_ref, vt_ref, r2_ref, e2_ref, c_ref, e1_ref, o_ref):
    e = pl.program_id(1)
    tb = h_ref.shape[0]
    sub = r2_ref.shape[2]

    @pl.when(e == 0)
    def _():
        o_ref[...] = jnp.zeros_like(o_ref)

    tw = min(tb, PEER_UNIT_TOKENS)
    per = PEER_SUB // PEER_KEYS
    ngrp = tw // LANES
    goff = 0 if tb == ROUTE_TOKENS else (pl.program_id(0) % (ROUTE_TOKENS // tb)) * ngrp
    units = [(q, tp) for q in range(u_ref.shape[0] // PEER_SUB) for tp in range(tb // tw)]

    def token_rows(ref, h, jj, tp):
        row = jnp.concatenate(
            [ref[h, jj, pl.ds(goff + tp * ngrp + g, 1), :] for g in range(ngrp)], axis=1)
        return jnp.broadcast_to(row, (sub, tw)).astype(BF16)[None]

    def scores(unit):
        q, tp = unit
        return _dot_nt(u_ref[q * PEER_SUB:(q + 1) * PEER_SUB, :],
                       h_ref[tp * tw:(tp + 1) * tw, :])

    accs = [o_ref[:, tp * tw:(tp + 1) * tw] for tp in range(tb // tw)]
    at_next = scores(units[0])
    for n, (q, tp) in enumerate(units):
        at = at_next
        if n + 1 < len(units):
            at_next = scores(units[n + 1])
        cols = slice(tp * tw, (tp + 1) * tw)
        gs = []
        for jj in range(q * per, (q + 1) * per):
            w = jnp.zeros((PEER_KEYS // sub, sub, tw), BF16)
            for h in range(PEER_HEADS):
                picked = jnp.where(r2_ref[h, :, :, cols] < token_rows(c_ref, h, jj, tp),
                                   e2_ref[h, :, :, cols], jnp.zeros_like(w))
                w = w + picked * token_rows(e1_ref, h, jj, tp)
            lo = (jj - q * per) * PEER_KEYS
            a = _gelu(at[lo:lo + PEER_KEYS, :]).astype(BF16)
            gs.append(a * w.reshape(PEER_KEYS, tw))
        accs[tp] = accs[tp] + _dot(vt_ref[:, q * PEER_SUB:(q + 1) * PEER_SUB],
                                   jnp.concatenate(gs, axis=0))
    for tp, acc in enumerate(accs):
        o_ref[:, tp * tw:(tp + 1) * tw] = acc


def _peer_dense(h, u_all, v_all, layer, r2, e2, c, e1, *, tb, te):
    t, d = h.shape
    ne = u_all.shape[1]
    sub = 16
    tr = r2.shape[2]
    r2 = r2.reshape(PEER_HEADS, PEER_KEYS // sub, sub, tr)
    e2 = e2.reshape(PEER_HEADS, PEER_KEYS // sub, sub, tr)
    nchunk = te // PEER_KEYS
    per_route = ROUTE_TOKENS // tb
    full = pl.BlockSpec((PEER_HEADS, PEER_KEYS // sub, sub, tb), lambda i, e: (0, 0, 0, i))
    rows = pl.BlockSpec((PEER_HEADS, nchunk, GROUPS, LANES),
                        lambda i, e: (0, e, i // per_route, 0))
    return pl.pallas_call(
        _peer_dense_kernel,
        grid=(t // tb, ne // te),
        in_specs=[pl.BlockSpec((tb, d), lambda i, e: (i, 0)),
                  pl.BlockSpec((None, te, d), lambda i, e: (layer, e, 0)),
                  pl.BlockSpec((None, d, te), lambda i, e: (layer, 0, e)),
                  full, full, rows, rows],
        out_specs=pl.BlockSpec((d, tb), lambda i, e: (0, i)),
        out_shape=jax.ShapeDtypeStruct((d, t), F32),
        compiler_params=_cp("parallel", "arbitrary"),
        name="peer_dense",
    )(h, u_all, v_all, r2, e2, c, e1)


def _peer_out_kernel(ot_ref, x_ref, gate_ref, xo_ref):
    xo_ref[...] = x_ref[...] + gate_ref[...] * ot_ref[...].T


def _peer_out(ot, x, gate, *, tm, per_row):
    t, d = x.shape
    row = lambda i: (i, 0)
    return pl.pallas_call(
        _peer_out_kernel,
        grid=(t // tm,),
        in_specs=[pl.BlockSpec((d, tm), lambda i: (0, i)), pl.BlockSpec((tm, d), row),
                  _mod_spec(per_row, tm, d, False)],
        out_specs=pl.BlockSpec((tm, d), row),
        out_shape=jax.ShapeDtypeStruct((t, d), F32),
        compiler_params=_cp("parallel"),
        name="peer_out",
    )(ot, x, gate)


def _peer(h, x, gate, wq_t, kexp, u_all, v_all, layer, *, tb, tm, per_row):
    t = h.shape[0]
    h_route = jnp.pad(h, ((0, -t % ROUTE_TOKENS), (0, 0)))
    r2, e2, c, e1 = _peer_route(h_route, wq_t, kexp)
    ot = _peer_dense(h, u_all, v_all, layer, r2, e2, c, e1, tb=tb, te=PEER_EXPERT_BLOCK)
    return _peer_out(ot, x, gate, tm=tm, per_row=per_row)


def kernel(x_prompt, x_sample, c_prompt, c_sample, cache_pool, cache_k, cache_v, cache_logf, w_ada, b_ada, norm1, norm2, w_in_ab, b_f, w_pool, pool_scale, q_gain, k_gain, w_out_ab, w_uv, sg_ln_g, sg_ln_b, w_s, b_s, w_out_sg, peer_wq, peer_keys, peer_u, peer_v):
    nb, seq, d = x_prompt.shape
    db, dl, _ = x_sample.shape
    depth = w_ada.shape[0]
    past = cache_k.shape[2]
    pool_hist = cache_pool.shape[2]
    assert nb == 1 and db * dl == LANES and dl == 16 and pool_hist == dl - 1
    fox_w = FOX_HEADS * HEAD_DIM
    pool_w = w_pool.shape[1] * w_pool.shape[2]
    ts = db * dl

    c_rows = jnp.concatenate(
        [c_prompt, c_sample, jnp.zeros((16 - nb - db, d), F32)], axis=0)
    mods = _ada(c_rows, w_ada, b_ada)

    def mod_p(layer, k):
        return mods[layer, 0:1, k * d:(k + 1) * d]

    def mod_s(layer, k):
        return jnp.repeat(mods[layer, 1:1 + db, k * d:(k + 1) * d], dl, axis=0)

    xp = x_prompt.reshape(seq, d)
    xs = x_sample.reshape(ts, d)
    tm_p = 512 if seq % 512 == 0 else 256
    outs = {k: [] for k in ("pool_p", "pool_s", "kp", "ks", "vp", "vs", "lfp", "lfs", "sgv")}
    u_all = peer_u.astype(BF16)
    v_all = jnp.swapaxes(peer_v, 1, 2).astype(BF16)

    for layer in range(depth):
        j = layer // 2
        g1 = norm1[layer].reshape(1, d)
        g2 = norm2[layer].reshape(1, d)
        if layer % 2 == 0:
            w_in = w_in_ab[j]
            w_main = w_in[:, :pool_w + 3 * fox_w].astype(BF16)
            wf_t = w_in[:, pool_w + 3 * fox_w:].T.astype(BF16)
            bfc = b_f[j].reshape(FOX_HEADS, 1)
            qg = q_gain[j].reshape(1, fox_w)
            kg = k_gain[j].reshape(1, fox_w)
            wp = w_pool[j].astype(BF16)
            psc = pool_scale[j].reshape(1, pool_w)
            wo = w_out_ab[j].astype(BF16)

            p, qb, k, kb, v, vb, lf, fc = _proj(
                xp, g1, mod_p(layer, 0), mod_p(layer, 1), w_main, wf_t, bfc, qg, kg,
                tm=tm_p, per_row=False, do_cumsum=True)
            a_out = _pool_prompt(p, wp, psc, tm=256)
            score_bound = 1.02 * (HEAD_DIM ** 0.5) * (jnp.max(jnp.abs(q_gain[j]), axis=-1)
                                                      * jnp.max(jnp.abs(k_gain[j]), axis=-1))
            b_out = _attn_prompt(qb, kb, vb, fc, score_bound, tq=tm_p)
            xp, hp = _merge(a_out, b_out, wo, xp, mod_p(layer, 2), g2, mod_p(layer, 3),
                            mod_p(layer, 4), tm=256, per_row=False)
            outs["pool_p"].append(p[-pool_hist:].reshape(nb, pool_hist, pool_w))
            outs["kp"].append(k.reshape(nb, seq, FOX_HEADS, HEAD_DIM))
            outs["vp"].append(v.reshape(nb, seq, FOX_HEADS, HEAD_DIM))
            outs["lfp"].append(lf.T.reshape(nb, seq, FOX_HEADS))

            p, qb, k, kb, v, vb, lf, _ = _proj(
                xs, g1, mod_s(layer, 0), mod_s(layer, 1), w_main, wf_t, bfc, qg, kg,
                tm=ts, per_row=True, do_cumsum=False)
            p3 = p.reshape(db, dl, pool_w)
            buf = jnp.concatenate([jnp.zeros((db, 1, pool_w), F32), cache_pool[j], p3], axis=1)
            a_out = _pool_sample(buf.reshape(db * 2 * dl, pool_w), p, wp, psc, pos0=past)
            b_out = _attn_sample(qb, kb, vb, lf, cache_k[j], cache_v[j], cache_logf[j],
                                 score_bound, tk=min(past, SAMPLE_KEY_CHUNK))
            xs, hs = _merge(a_out, b_out, wo, xs, mod_s(layer, 2), g2, mod_s(layer, 3),
                            mod_s(layer, 4), tm=ts, per_row=True)
            outs["pool_s"].append(jnp.concatenate([cache_pool[j], p3], axis=1)[:, -pool_hist:])
            outs["ks"].append(k.reshape(db, dl, FOX_HEADS, HEAD_DIM))
            outs["vs"].append(v.reshape(db, dl, FOX_HEADS, HEAD_DIM))
            outs["lfs"].append(lf.T.reshape(db, dl, FOX_HEADS))
        else:
            wuv = w_uv[j].astype(BF16)
            lng = sg_ln_g[j].reshape(1, -1)
            lnb = sg_ln_b[j].reshape(1, -1)
            wo = w_out_sg[j].astype(BF16)
            tri = jnp.tril(jnp.ones((SG_CHUNK, SG_CHUNK), bool))
            ws_p = jnp.where(tri, w_s[j], 0.0).astype(BF16)
            bs_p = b_s[j].T
            ws16 = jnp.where(tri[:dl, :dl], w_s[j][:, :dl, :dl], 0.0)
            ws_s = jnp.einsum("ab,gts->gatbs", jnp.eye(db, dtype=F32), ws16)
            ws_s = ws_s.reshape(SG_GROUPS, ts, ts).astype(BF16)
            bs_s = jnp.tile(b_s[j][:, :dl], (1, db)).T

            u, vn = _uv(xp, g1, mod_p(layer, 0), mod_p(layer, 1), wuv, lng, lnb,
                        tm=tm_p, per_row=False, vn_dtype=BF16)
            xp, hp = _sgate(u, vn, ws_p, bs_p, wo, xp, mod_p(layer, 2), g2, mod_p(layer, 3),
                            mod_p(layer, 4), tm=256, per_row=False)
            u, vn = _uv(xs, g1, mod_s(layer, 0), mod_s(layer, 1), wuv, lng, lnb,
                        tm=ts, per_row=True, vn_dtype=F32)
            xs, hs = _sgate(u, vn, ws_s, bs_s, wo, xs, mod_s(layer, 2), g2, mod_s(layer, 3),
                            mod_s(layer, 4), tm=ts, per_row=True)
            outs["sgv"].append(vn.reshape(db, dl, -1))

        wq_t = peer_wq[layer].T.astype(BF16)
        kexp = jnp.einsum("hskd,gj->hskgjd", peer_keys[layer].astype(BF16),
                          jnp.eye(GROUPS, dtype=BF16))
        kexp = kexp.reshape(PEER_HEADS, 2, PEER_KEYS * GROUPS, GROUPS * kexp.shape[-1])
        xp = _peer(hp, xp, mod_p(layer, 5), wq_t, kexp, u_all, v_all, layer,
                   tb=PEER_TOKEN_BLOCK if seq % PEER_TOKEN_BLOCK == 0 else 256, tm=256,
                   per_row=False)
        xs = _peer(hs, xs, mod_s(layer, 5), wq_t, kexp, u_all, v_all, layer,
                   tb=ts, tm=ts, per_row=True)

    st = lambda key: jnp.stack(outs[key])
    return (xp.reshape(nb, seq, d), xs.reshape(db, dl, d), st("pool_p"), st("pool_s"),
            st("kp"), st("ks"), st("vp"), st("vs"), st("lfp"), st("lfs"), st("sgv"))
```

```python
import functools

import jax
import jax.numpy as jnp
from jax import lax
from jax.experimental import pallas as pl
from jax.experimental.pallas import tpu as pltpu

F32 = jnp.float32
BF16 = jnp.bfloat16
EPS = 1e-6

LANES = 128
POOL_WINDOWS = (2, 4, 8, 16)
POOL_HALO = 128
FOX_HEADS = 8
HEAD_DIM = 128
SAMPLE_KEY_CHUNK = 512
SG_GROUPS = 8
SG_CHUNK = 128
PEER_HEADS = 8
PEER_KEYS = 128
PEER_TOPK = 16
PEER_TOKEN_BLOCK = 512
PEER_EXPERT_BLOCK = 1024
PEER_SUB = 256
PEER_UNIT_TOKENS = 1024
NEG_BIG = -1e30
EXP_ZERO_BELOW = 110.0
RANK_NONE = 64.0
VMEM_LIMIT = 56 * 1024 * 1024

NT_DIMS = (((1,), (1,)), ((), ()))


def _cp(*sem):
    return pltpu.CompilerParams(dimension_semantics=sem, vmem_limit_bytes=VMEM_LIMIT)


def _dot(a, b):
    return jnp.dot(a, b, preferred_element_type=F32)


def _dot_nt(a, b):
    return lax.dot_general(a, b, NT_DIMS, preferred_element_type=F32)


def _modulate(x, g, shift, scale):
    ms = jnp.mean(x * x, axis=-1, keepdims=True)
    return x * lax.rsqrt(ms + EPS) * g * (1.0 + scale) + shift


def _gelu(x):
    c = 0.7978845608028654
    half = 0.5 * x
    return half + half * jnp.tanh(x * (c + (c * 0.044715) * (x * x)))


def _log_sigmoid(x):
    return -(jnp.maximum(-x, 0.0) + jnp.log(1.0 + jnp.exp(-jnp.abs(x))))


def _lane_cumsum(x):
    n = x.shape[-1]
    lane = lax.broadcasted_iota(jnp.int32, x.shape, x.ndim - 1)
    s = 1
    while s < n:
        x = x + jnp.where(lane >= s, pltpu.roll(x, s, x.ndim - 1), 0.0)
        s *= 2
    return x


def _ada_kernel(c_ref, w_ref, b_ref, o_ref):
    c = c_ref[...]
    a = (c * (1.0 / (1.0 + jnp.exp(-c)))).astype(BF16)
    o_ref[0] = _dot(a, w_ref[0].astype(BF16)) + b_ref[0]


def _ada(c_rows, w_ada, b_ada):
    nl, d, n = w_ada.shape
    r = c_rows.shape[0]
    tn = 1024
    return pl.pallas_call(
        _ada_kernel,
        grid=(nl, n // tn),
        in_specs=[pl.BlockSpec((r, d), lambda l, j: (0, 0)),
                  pl.BlockSpec((1, d, tn), lambda l, j: (l, 0, j)),
                  pl.BlockSpec((1, 1, tn), lambda l, j: (l, 0, j))],
        out_specs=pl.BlockSpec((1, r, tn), lambda l, j: (l, 0, j)),
        out_shape=jax.ShapeDtypeStruct((nl, r, n), F32),
        compiler_params=_cp("parallel", "parallel"),
        name="ada",
    )(c_rows, w_ada, b_ada.reshape(nl, 1, n))


def _mod_spec(per_row, tm, d, nj):
    if nj:
        return (pl.BlockSpec((tm, d), lambda i, j: (i, 0)) if per_row
                else pl.BlockSpec((1, d), lambda i, j: (0, 0)))
    return (pl.BlockSpec((tm, d), lambda i: (i, 0)) if per_row
            else pl.BlockSpec((1, d), lambda i: (0, 0)))


def _head_rms(z, gain):
    outs = []
    for h in range(FOX_HEADS):
        zh = z[:, h * HEAD_DIM:(h + 1) * HEAD_DIM]
        ms = jnp.mean(zh * zh, axis=-1, keepdims=True)
        outs.append(zh * lax.rsqrt(ms + EPS))
    return jnp.concatenate(outs, axis=-1) * gain


def _proj_kernel(x_ref, g_ref, sh_ref, sc_ref, w_ref, wf_ref, bf_ref, qg_ref, kg_ref,
                 p_ref, q_ref, k_ref, kb_ref, v_ref, vb_ref, lf_ref, fc_ref,
                 h_scr, carry_scr, *, do_cumsum):
    i = pl.program_id(0)
    j = pl.program_id(1)

    @pl.when(j == 0)
    def _():
        hb = _modulate(x_ref[...], g_ref[...], sh_ref[...], sc_ref[...]).astype(BF16)
        h_scr[...] = hb
        lf = _log_sigmoid(_dot_nt(wf_ref[...], hb) + bf_ref[...])
        lf_ref[...] = lf
        if do_cumsum:
            @pl.when(i == 0)
            def _():
                carry_scr[...] = jnp.zeros_like(carry_scr)
            fc = _lane_cumsum(lf) + carry_scr[:, 0:1]
            fc_ref[...] = fc
            carry_scr[...] = jnp.broadcast_to(fc[:, -1:], carry_scr.shape)
        else:
            fc_ref[...] = lf

    z = _dot(h_scr[...], w_ref[...])

    @pl.when(j == 0)
    def _():
        p_ref[...] = z

    @pl.when(j == 1)
    def _():
        q_ref[...] = (_head_rms(z, qg_ref[...]) * (HEAD_DIM ** -0.5)).astype(BF16)

    @pl.when(j == 2)
    def _():
        kn = _head_rms(z, kg_ref[...])
        k_ref[...] = kn
        kb_ref[...] = kn.astype(BF16)

    @pl.when(j == 3)
    def _():
        v_ref[...] = z
        vb_ref[...] = z.astype(BF16)


def _proj(x, g, shift, scale, w_main, wf_t, b_f, q_gain, k_gain, *, tm, per_row, do_cumsum):
    t, d = x.shape
    w4 = w_main.shape[1] // 4
    nh = wf_t.shape[0]
    row = lambda i, j: (i, 0)
    const = lambda i, j: (0, 0)
    tok_f32 = jax.ShapeDtypeStruct((t, w4), F32)
    tok_b16 = jax.ShapeDtypeStruct((t, w4), BF16)
    head_t = jax.ShapeDtypeStruct((nh, t), F32)
    return pl.pallas_call(
        functools.partial(_proj_kernel, do_cumsum=do_cumsum),
        grid=(t // tm, 4),
        in_specs=[pl.BlockSpec((tm, d), row),
                  pl.BlockSpec((1, d), const),
                  _mod_spec(per_row, tm, d, True),
                  _mod_spec(per_row, tm, d, True),
                  pl.BlockSpec((d, w4), lambda i, j: (0, j)),
                  pl.BlockSpec((nh, d), const),
                  pl.BlockSpec((nh, 1), const),
                  pl.BlockSpec((1, w4), const),
                  pl.BlockSpec((1, w4), const)],
        out_specs=[pl.BlockSpec((tm, w4), row)] * 6
                  + [pl.BlockSpec((nh, tm), lambda i, j: (0, i))] * 2,
        out_shape=[tok_f32, tok_b16, tok_f32, tok_b16, tok_f32, tok_b16, head_t, head_t],
        scratch_shapes=[pltpu.VMEM((tm, d), BF16), pltpu.VMEM((nh, LANES), F32)],
        compiler_params=_cp("arbitrary", "arbitrary"),
        name="proj",
    )(x, g, shift, scale, w_main, wf_t, b_f, q_gain, k_gain)


def _pool_kernel(ext_ref, tok_ref, w_ref, ps_ref, o_ref, *, prompt, pos0):
    i = pl.program_id(0)
    tok = tok_ref[...]
    tm = tok.shape[0]
    if prompt:
        halo = jnp.where(i == 0, 0.0, ext_ref[...])
        ext = jnp.concatenate([halo, tok], axis=0)
    else:
        ext = ext_ref[...]
    ke = ext.shape[0]
    hi = ext.astype(BF16)
    lo = (ext - hi.astype(F32)).astype(BF16)
    m_io = lax.broadcasted_iota(jnp.int32, (tm, ke), 0)
    c_io = lax.broadcasted_iota(jnp.int32, (tm, ke), 1)
    r_io = lax.broadcasted_iota(jnp.int32, (tm, 1), 0)
    if prompt:
        tgt = m_io + POOL_HALO
        pos = pos0 + i * tm + r_io
    else:
        tgt = m_io + ((m_io >> 4) << 4) + 16
        pos = pos0 + (r_io & 15)
    dlt = tgt - c_io
    gw = w_ref.shape[1]
    for gi, win in enumerate(POOL_WINDOWS):
        sl = slice(gi * gw, (gi + 1) * gw)
        band = jnp.where(dlt >= 0, jnp.where(dlt < win, 1.0, 0.0), 0.0).astype(BF16)
        wsum = _dot(band, hi[:, sl]) + _dot(band, lo[:, sl])
        cnt = jnp.minimum(win, pos + 1).astype(F32)
        dd = wsum / cnt - tok[:, sl]
        o = _dot(dd.astype(BF16), w_ref[gi]) * ps_ref[:, sl]
        o_ref[:, sl] = o.astype(o_ref.dtype)


def _pool_prompt(p, w_pool, pool_scale, *, tm):
    t, c = p.shape
    per = tm // POOL_HALO
    return pl.pallas_call(
        functools.partial(_pool_kernel, prompt=True, pos0=0),
        grid=(t // tm,),
        in_specs=[pl.BlockSpec((POOL_HALO, c), lambda i: (jnp.maximum(i * per - 1, 0), 0)),
                  pl.BlockSpec((tm, c), lambda i: (i, 0)),
                  pl.BlockSpec(w_pool.shape, lambda i: (0, 0, 0)),
                  pl.BlockSpec((1, c), lambda i: (0, 0))],
        out_specs=pl.BlockSpec((tm, c), lambda i: (i, 0)),
        out_shape=jax.ShapeDtypeStruct((t, c), BF16),
        compiler_params=_cp("parallel"),
        name="pool_prompt",
    )(p, p, w_pool, pool_scale)


def _pool_sample(buf, p, w_pool, pool_scale, *, pos0):
    t, c = p.shape
    return pl.pallas_call(
        functools.partial(_pool_kernel, prompt=False, pos0=pos0),
        grid=(1,),
        in_specs=[pl.BlockSpec(buf.shape, lambda i: (0, 0)),
                  pl.BlockSpec((t, c), lambda i: (0, 0)),
                  pl.BlockSpec(w_pool.shape, lambda i: (0, 0, 0)),
                  pl.BlockSpec((1, c), lambda i: (0, 0))],
        out_specs=pl.BlockSpec((t, c), lambda i: (0, 0)),
        out_shape=jax.ShapeDtypeStruct((t, c), BF16),
        compiler_params=_cp("arbitrary"),
        name="pool_sample",
    )(buf, p, w_pool, pool_scale)


def _softmax_step(carry, s, v):
    m, l, acc = carry
    m_new = jnp.maximum(m, jnp.max(s, axis=-1, keepdims=True))
    alpha = jnp.exp(m - m_new)
    p = jnp.exp(s - m_new)
    l = alpha * l + jnp.sum(p, axis=-1, keepdims=True)
    acc = alpha * acc + _dot(p.astype(BF16), v)
    return m_new, l, acc


def _attn_prompt_kernel(fend_ref, fstart_ref, bound_ref, q_ref, k_ref, v_ref, fk_ref, o_ref,
                        *, tq):
    h = pl.program_id(0)
    i = pl.program_id(1)
    q = q_ref[...]

    def chunk(kj, carry, masked):
        off = pl.multiple_of(kj * tq, tq)
        s = _dot_nt(q, k_ref[pl.ds(off, tq), :]) - fk_ref[:, pl.ds(off, tq)]
        if masked:
            r_io = lax.broadcasted_iota(jnp.int32, (tq, tq), 0)
            c_io = lax.broadcasted_iota(jnp.int32, (tq, tq), 1)
            s = jnp.where(c_io <= r_io, s, NEG_BIG)
        return _softmax_step(carry, s, v_ref[pl.ds(off, tq), :])

    limit = fstart_ref[h, i] + bound_ref[h]
    dead = lax.fori_loop(
        0, i, lambda kj, n: n + jnp.where(fend_ref[h, kj] > limit, 1, 0), jnp.int32(0))
    init = (jnp.full((tq, 1), NEG_BIG, F32), jnp.zeros((tq, 1), F32),
            jnp.zeros((tq, HEAD_DIM), F32))
    carry = lax.fori_loop(dead, i, lambda kj, c: chunk(kj, c, False), init)
    _, l, acc = chunk(i, carry, True)
    o_ref[...] = (acc / l).astype(o_ref.dtype)


def _attn_prompt(qb, kb, vb, fcum, score_bound, *, tq):
    t, w = qb.shape
    nh = w // HEAD_DIM
    fend = fcum[:, tq - 1::tq]
    fstart = fcum[:, ::tq]
    bound = 2.0 * score_bound + EXP_ZERO_BELOW
    smem = pl.BlockSpec(memory_space=pltpu.SMEM)
    return pl.pallas_call(
        functools.partial(_attn_prompt_kernel, tq=tq),
        grid=(nh, t // tq),
        in_specs=[smem, smem, smem,
                  pl.BlockSpec((tq, HEAD_DIM), lambda h, i: (i, h)),
                  pl.BlockSpec((t, HEAD_DIM), lambda h, i: (0, h)),
                  pl.BlockSpec((t, HEAD_DIM), lambda h, i: (0, h)),
                  pl.BlockSpec((None, 1, t), lambda h, i: (h, 0, 0))],
        out_specs=pl.BlockSpec((tq, HEAD_DIM), lambda h, i: (i, h)),
        out_shape=jax.ShapeDtypeStruct((t, w), BF16),
        compiler_params=_cp("parallel", "parallel"),
        name="attn_prompt",
    )(fend, fstart, bound, qb, kb, vb, fcum.reshape(nh, 1, t))


def _strided_lane_cumsum(x, stride, seg=None):
    n = seg or x.shape[-1]
    lane = lax.broadcasted_iota(jnp.int32, x.shape, x.ndim - 1)
    pos = lane if seg is None else lane & (seg - 1)
    s = stride
    while s < n:
        x = x + jnp.where(pos >= s, pltpu.roll(x, s, x.ndim - 1), 0.0)
        s *= 2
    return x


def _tile_last_lanes(x, period):
    n = x.shape[-1]
    lane = lax.broadcasted_iota(jnp.int32, x.shape, x.ndim - 1)
    y = jnp.where(lane >= n - period, x, 0.0)
    s = period
    while s < n:
        y = y + pltpu.roll(y, n - s, x.ndim - 1)
        s *= 2
    return y


def _chunk_suffix_kernel(lf_ref, o_ref, *, tk):
    x = lf_ref[...]
    p = x.shape[1]
    pos = lax.broadcasted_iota(jnp.int32, (p, LANES), 0)
    col = lax.broadcasted_iota(jnp.int32, (p, LANES), 1)
    later = jnp.where(pos >= (col + 1) * tk, 1.0, 0.0).astype(BF16)
    hi = x.astype(BF16)
    r1 = x - hi.astype(F32)
    mid = r1.astype(BF16)
    lo = (r1 - mid.astype(F32)).astype(BF16)
    o_ref[...] = _dot(hi, later) + _dot(mid, later) + _dot(lo, later)


def _chunk_suffix(lf_rows, *, tk):
    r, p = lf_rows.shape
    return pl.pallas_call(
        functools.partial(_chunk_suffix_kernel, tk=tk),
        grid=(1,),
        in_specs=[pl.BlockSpec((r, p), lambda i: (0, 0))],
        out_specs=pl.BlockSpec((r, LANES), lambda i: (0, 0)),
        out_shape=jax.ShapeDtypeStruct((r, LANES), F32),
        compiler_params=_cp("arbitrary"),
        name="cache_decay",
    )(lf_rows)


def _attn_sample_kernel(need_ref, q_ref, ck_ref, cv_ref, clf_ref, kn_ref, vn_ref, lfn_ref, o_ref,
                        q_scr, m_scr, l_scr, acc_scr, car_scr):
    b = pl.program_id(0)
    c = pl.program_id(1)
    nl = q_ref.shape[0]
    rows = q_scr.shape[0]

    def stack_heads(x):
        return jnp.concatenate(
            [x[:, h * HEAD_DIM:(h + 1) * HEAD_DIM] for h in range(FOX_HEADS)], axis=0)

    @pl.when(c == 0)
    def _():
        qs = stack_heads(q_ref[...])
        q_scr[...] = qs
        kn = stack_heads(kn_ref[...])
        r_io = lax.broadcasted_iota(jnp.int32, (rows, rows), 0)
        c_io = lax.broadcasted_iota(jnp.int32, (rows, rows), 1)
        ncum = _strided_lane_cumsum(lfn_ref[...], 1, seg=nl)
        sn = jnp.where((r_io // nl) == (c_io // nl),
                       jnp.where((c_io % nl) <= (r_io % nl), _dot_nt(qs, kn) - ncum, NEG_BIG),
                       NEG_BIG)
        init = (jnp.full((rows, 1), NEG_BIG, F32), jnp.zeros((rows, 1), F32),
                jnp.zeros((rows, HEAD_DIM), F32))
        m_scr[...], l_scr[...], acc_scr[...] = _softmax_step(init, sn, stack_heads(vn_ref[...]))
        car_scr[...] = jnp.zeros_like(car_scr)

    @pl.when(c < need_ref[b])
    def _():
        n = clf_ref.shape[1]
        incl = _strided_lane_cumsum(clf_ref[...], FOX_HEADS)
        total = _tile_last_lanes(incl, FOX_HEADS)
        after = car_scr[...] + total - incl
        car_scr[...] = car_scr[...] + total
        r_io = lax.broadcasted_iota(jnp.int32, (rows, n), 0)
        c_io = lax.broadcasted_iota(jnp.int32, (rows, n), 1)
        s = _dot_nt(q_scr[...], ck_ref[...].astype(BF16)) + after
        s = jnp.where((c_io % FOX_HEADS) == (r_io // nl), s, NEG_BIG)
        carry = _softmax_step((m_scr[...], l_scr[...], acc_scr[...]), s, cv_ref[...].astype(BF16))
        m_scr[...], l_scr[...], acc_scr[...] = carry

    @pl.when(c == pl.num_programs(1) - 1)
    def _():
        o = acc_scr[...] / l_scr[...]
        o_ref[...] = jnp.concatenate(
            [o[h * nl:(h + 1) * nl, :] for h in range(FOX_HEADS)], axis=-1).astype(o_ref.dtype)


def _attn_sample(qb, kb, vb, lf_new, cache_k, cache_v, cache_lf, score_bound, *, tk):
    b, past, nh, dh = cache_k.shape
    w = nh * dh
    nl = qb.shape[0] // b
    rows = nh * nl
    nch = past // tk
    after = _chunk_suffix(jnp.transpose(cache_lf, (0, 2, 1)).reshape(b * nh, past), tk=tk)
    alive = (after.reshape(b, nh, LANES)[:, :, :nch]
             + 2.0 * score_bound[None, :, None]) >= -EXP_ZERO_BELOW
    need = jnp.maximum(jnp.max(jnp.sum(alive, axis=2), axis=1), 1).astype(jnp.int32)
    lfn = jnp.transpose(lf_new.reshape(nh, b, nl), (1, 0, 2)).reshape(b, 1, rows)

    def newest_first(c, need_s):
        return nch - 1 - jnp.minimum(c, need_s - 1)

    chunk = lambda s, c, need: (s, newest_first(c, need[s]), 0)
    new = lambda s, c, need: (s, 0)
    return pl.pallas_call(
        _attn_sample_kernel,
        grid_spec=pltpu.PrefetchScalarGridSpec(
            num_scalar_prefetch=1,
            grid=(b, nch),
            in_specs=[pl.BlockSpec((nl, w), new),
                      pl.BlockSpec((None, tk * nh, dh), chunk),
                      pl.BlockSpec((None, tk * nh, dh), chunk),
                      pl.BlockSpec((None, 1, tk * nh),
                                   lambda s, c, need: (s, 0, newest_first(c, need[s]))),
                      pl.BlockSpec((nl, w), new),
                      pl.BlockSpec((nl, w), new),
                      pl.BlockSpec((None, 1, rows), lambda s, c, need: (s, 0, 0))],
            out_specs=pl.BlockSpec((nl, w), new),
            scratch_shapes=[pltpu.VMEM((rows, dh), BF16), pltpu.VMEM((rows, 1), F32),
                            pltpu.VMEM((rows, 1), F32), pltpu.VMEM((rows, dh), F32),
                            pltpu.VMEM((1, tk * nh), F32)]),
        out_shape=jax.ShapeDtypeStruct(qb.shape, BF16),
        compiler_params=_cp("parallel", "arbitrary"),
        name="attn_sample",
    )(need, qb, cache_k.reshape(b, past * nh, dh), cache_v.reshape(b, past * nh, dh),
      cache_lf.reshape(b, 1, past * nh), kb, vb, lfn)


def _residual_mod(x, gate, o, g2, sh2, sc2, x_ref, h_ref):
    xn = x + gate * o
    x_ref[...] = xn
    h_ref[...] = _modulate(xn, g2, sh2, sc2).astype(h_ref.dtype)


def _merge_kernel(a_ref, b_ref, w_ref, x_ref, gate_ref, g2_ref, sh2_ref, sc2_ref, xo_ref, h_ref):
    ca = a_ref.shape[1]
    o = _dot(a_ref[...], w_ref[:ca, :]) + _dot(b_ref[...], w_ref[ca:, :])
    _residual_mod(x_ref[...], gate_ref[...], o, g2_ref[...], sh2_ref[...], sc2_ref[...],
                  xo_ref, h_ref)


def _merge(a, b, w_out, x, gate, g2, sh2, sc2, *, tm, per_row):
    t, d = x.shape
    ca = a.shape[1]
    row = lambda i: (i, 0)
    mod = _mod_spec(per_row, tm, d, False)
    return pl.pallas_call(
        _merge_kernel,
        grid=(t // tm,),
        in_specs=[pl.BlockSpec((tm, ca), row), pl.BlockSpec((tm, ca), row),
                  pl.BlockSpec(w_out.shape, lambda i: (0, 0)),
                  pl.BlockSpec((tm, d), row), mod,
                  pl.BlockSpec((1, d), lambda i: (0, 0)), mod, mod],
        out_specs=[pl.BlockSpec((tm, d), row), pl.BlockSpec((tm, d), row)],
        out_shape=[jax.ShapeDtypeStruct((t, d), F32), jax.ShapeDtypeStruct((t, d), BF16)],
        compiler_params=_cp("parallel"),
        name="merge",
    )(a, b, w_out, x, gate, g2, sh2, sc2)


def _uv_kernel(x_ref, g_ref, sh_ref, sc_ref, w_ref, lng_ref, lnb_ref, u_ref, vn_ref, h_scr):
    j = pl.program_id(1)

    @pl.when(j == 0)
    def _():
        h_scr[...] = _modulate(x_ref[...], g_ref[...], sh_ref[...], sc_ref[...]).astype(BF16)

    z = _gelu(_dot(h_scr[...], w_ref[...]))

    @pl.when(j == 0)
    def _():
        u_ref[...] = z.astype(u_ref.dtype)

    @pl.when(j == 1)
    def _():
        mu = jnp.mean(z, axis=-1, keepdims=True)
        zc = z - mu
        var = jnp.mean(zc * zc, axis=-1, keepdims=True)
        vn_ref[...] = (zc * lax.rsqrt(var + EPS) * lng_ref[...] + lnb_ref[...]).astype(vn_ref.dtype)


def _uv(x, g, shift, scale, w_uv, ln_g, ln_b, *, tm, per_row, vn_dtype):
    t, d = x.shape
    sw = w_uv.shape[1] // 2
    row = lambda i, j: (i, 0)
    const = lambda i, j: (0, 0)
    return pl.pallas_call(
        _uv_kernel,
        grid=(t // tm, 2),
        in_specs=[pl.BlockSpec((tm, d), row), pl.BlockSpec((1, d), const),
                  _mod_spec(per_row, tm, d, True), _mod_spec(per_row, tm, d, True),
                  pl.BlockSpec((d, sw), lambda i, j: (0, j)),
                  pl.BlockSpec((1, sw), const), pl.BlockSpec((1, sw), const)],
        out_specs=[pl.BlockSpec((tm, sw), row), pl.BlockSpec((tm, sw), row)],
        out_shape=[jax.ShapeDtypeStruct((t, sw), BF16), jax.ShapeDtypeStruct((t, sw), vn_dtype)],
        scratch_shapes=[pltpu.VMEM((tm, d), BF16)],
        compiler_params=_cp("parallel", "arbitrary"),
        name="sg_uv",
    )(x, g, shift, scale, w_uv, ln_g, ln_b)


def _sgate_kernel(u_ref, vn_ref, ws_ref, bs_ref, w_ref, x_ref, gate_ref, g2_ref, sh2_ref, sc2_ref,
                  xo_ref, h_ref, y_scr):
    tm, sw = u_ref.shape
    gw = sw // SG_GROUPS
    for c in range(tm // SG_CHUNK):
        rs = slice(c * SG_CHUNK, (c + 1) * SG_CHUNK)
        for g in range(SG_GROUPS):
            cs = slice(g * gw, (g + 1) * gw)
            sv = _dot(ws_ref[g], vn_ref[rs, cs].astype(BF16)) + bs_ref[:, g:g + 1]
            y_scr[rs, cs] = (u_ref[rs, cs].astype(F32) * sv).astype(BF16)
    o = _dot(y_scr[...], w_ref[...])
    _residual_mod(x_ref[...], gate_ref[...], o, g2_ref[...], sh2_ref[...], sc2_ref[...],
                  xo_ref, h_ref)


def _sgate(u, vn, ws, bs_t, w_out, x, gate, g2, sh2, sc2, *, tm, per_row):
    t, d = x.shape
    sw = u.shape[1]
    row = lambda i: (i, 0)
    mod = _mod_spec(per_row, tm, d, False)
    return pl.pallas_call(
        _sgate_kernel,
        grid=(t // tm,),
        in_specs=[pl.BlockSpec((tm, sw), row), pl.BlockSpec((tm, sw), row),
                  pl.BlockSpec(ws.shape, lambda i: (0, 0, 0)),
                  pl.BlockSpec(bs_t.shape, lambda i: (0, 0)),
                  pl.BlockSpec(w_out.shape, lambda i: (0, 0)),
                  pl.BlockSpec((tm, d), row), mod,
                  pl.BlockSpec((1, d), lambda i: (0, 0)), mod, mod],
        out_specs=[pl.BlockSpec((tm, d), row), pl.BlockSpec((tm, d), row)],
        out_shape=[jax.ShapeDtypeStruct((t, d), F32), jax.ShapeDtypeStruct((t, d), BF16)],
        scratch_shapes=[pltpu.VMEM((tm, sw), BF16)],
        compiler_params=_cp("parallel"),
        name="sg_gate",
    )(u, vn, ws, bs_t, w_out, x, gate, g2, sh2, sc2)


GROUPS = 8
ROUTE_TOKENS = GROUPS * LANES
PAIR_R1 = [0] * 16 + [a for a in range(1, 8) for _ in range(8)] + list(range(8, 16))
PAIR_R2 = list(range(16)) + [i for _ in range(1, 8) for i in range(8)] + [0] * 8
PAIR_FLAT = [a * PEER_TOPK + b for a, b in zip(PAIR_R1, PAIR_R2)]


def _extract_topk(problems, vals_refs, idx_refs):
    chains = 4

    def one_round(r, prev):
        picked = []
        for p, (ref, ids) in enumerate(problems):
            n = len(ids)
            per = -(-n // chains)
            best = []
            for c0 in range(0, n, per):
                m = ix = None
                for e in range(c0, min(c0 + per, n)):
                    x = jnp.where(prev[p] == float(ids[e]), -jnp.inf, ref[e])
                    ref[e] = x
                    if m is None:
                        m, ix = x, jnp.full(x.shape, float(ids[e]), F32)
                    else:
                        ix = jnp.where(x > m, float(ids[e]), ix)
                        m = jnp.maximum(m, x)
                best.append((m, ix))
            m, ix = best[0]
            for mc, ic in best[1:]:
                ix = jnp.where(mc > m, ic, ix)
                m = jnp.maximum(m, mc)
            vals_refs[p][r] = m
            idx_refs[p][r] = ix
            picked.append(ix)
        return tuple(picked)

    none = jnp.full((GROUPS, LANES), -1.0, F32)
    return lax.fori_loop(0, PEER_TOPK, one_round, tuple(none for _ in problems))


def _peer_route_kernel(h_ref, wq_ref, kexp_ref, r2_ref, e2_ref, c_ref, e1_ref,
                       qt_scr, s_scr, so_scr, vals_scr, idx_scr, cand_scr, cval_scr, cidx_scr,
                       t2_scr, *, by_head):
    if by_head:
        qt_scr[...] = _dot_nt(wq_ref[...], h_ref[...]).astype(BF16)

        def stacked_queries(side):
            return jnp.concatenate(
                [qt_scr[(2 * h + side) * PEER_KEYS:(2 * h + side + 1) * PEER_KEYS, :]
                 for h in range(PEER_HEADS)], axis=0)
    else:
        hd = pl.program_id(1)

        @pl.when(hd == 0)
        def _():
            qt_scr[...] = _dot_nt(wq_ref[...], h_ref[...]).astype(BF16)

        row = pl.multiple_of(hd * 2 * PEER_KEYS, 2 * PEER_KEYS)

        def stacked_queries(side):
            return jnp.concatenate(
                [qt_scr[pl.ds(row + side * PEER_KEYS, PEER_KEYS), g * LANES:(g + 1) * LANES]
                 for g in range(GROUPS)], axis=0)

    for side in range(2):
        s = _dot(kexp_ref[side], stacked_queries(side)).reshape(PEER_KEYS, GROUPS, LANES)
        s_scr[side] = s
        so_scr[side] = s

    keys = list(range(PEER_KEYS))
    _extract_topk([(s_scr.at[0], keys), (s_scr.at[1], keys)],
                  [vals_scr.at[0], vals_scr.at[1]], [idx_scr.at[0], idx_scr.at[1]])
    v1 = [vals_scr[0, r] for r in range(PEER_TOPK)]
    v2 = [vals_scr[1, r] for r in range(PEER_TOPK)]
    ncand = len(PAIR_FLAT)
    for c in range(ncand):
        cand_scr[c] = v1[PAIR_R1[c]] + v2[PAIR_R2[c]]
    (last,) = _extract_topk([(cand_scr, PAIR_FLAT)], [cval_scr], [cidx_scr])

    top = v1[0] + v2[0]
    z = jnp.zeros((GROUPS, LANES), F32)
    for r in range(PEER_TOPK):
        z = z + jnp.exp(cval_scr[r] - top)
    inv_z = 1.0 / z
    cnt = [jnp.zeros((GROUPS, LANES), F32) for _ in range(PEER_TOPK)]
    for c in range(ncand):
        picked = jnp.where(cand_scr[c] == -jnp.inf, 1.0,
                           jnp.where(last == float(PAIR_FLAT[c]), 1.0, 0.0))
        cnt[PAIR_R1[c]] = cnt[PAIR_R1[c]] + picked
    idx1 = [idx_scr[0, r] for r in range(PEER_TOPK)]
    idx2 = [idx_scr[1, r] for r in range(PEER_TOPK)]

    def per_key(k, _):
        kf = lax.convert_element_type(k, F32)
        rank2 = jnp.full((GROUPS, LANES), RANK_NONE, F32)
        ck = jnp.zeros((GROUPS, LANES), F32)
        for r in range(PEER_TOPK):
            rank2 = jnp.where(idx2[r] == kf, float(r), rank2)
            ck = jnp.where(idx1[r] == kf, cnt[r], ck)
        base = pl.multiple_of(k * GROUPS, GROUPS)
        t2_scr[0, pl.ds(base, GROUPS), :] = rank2
        t2_scr[1, pl.ds(base, GROUPS), :] = jnp.exp(so_scr[1, k] - v2[0])
        c_ref[k] = ck
        e1_ref[k] = jnp.exp(so_scr[0, k] - v1[0]) * inv_z
        return 0

    lax.fori_loop(0, PEER_KEYS, per_key, 0, unroll=4)
    for g in range(GROUPS):
        rank2 = t2_scr[0, pl.ds(g, PEER_KEYS, stride=GROUPS), :].astype(BF16)
        fac2 = t2_scr[1, pl.ds(g, PEER_KEYS, stride=GROUPS), :].astype(BF16)
        if by_head:
            r2_ref[g] = rank2
            e2_ref[g] = fac2
        else:
            r2_ref[:, g * LANES:(g + 1) * LANES] = rank2
            e2_ref[:, g * LANES:(g + 1) * LANES] = fac2


def _route_scratch(nq, tokens):
    tile = (GROUPS, LANES)
    return [pltpu.VMEM((nq, tokens), BF16),
            pltpu.VMEM((2, PEER_KEYS) + tile, F32), pltpu.VMEM((2, PEER_KEYS) + tile, F32),
            pltpu.VMEM((2, PEER_TOPK) + tile, F32), pltpu.VMEM((2, PEER_TOPK) + tile, F32),
            pltpu.VMEM((len(PAIR_FLAT),) + tile, F32),
            pltpu.VMEM((PEER_TOPK,) + tile, F32), pltpu.VMEM((PEER_TOPK,) + tile, F32),
            pltpu.VMEM((2, PEER_KEYS * GROUPS, LANES), F32)]


def _peer_route_heads(h, wq_t, kexp_h):
    t, d = h.shape
    assert t == LANES and GROUPS == PEER_HEADS
    std = jax.ShapeDtypeStruct((PEER_HEADS, PEER_KEYS, t), BF16)
    by_key = jax.ShapeDtypeStruct((PEER_KEYS, PEER_HEADS, t), F32)
    full = lambda shape: pl.BlockSpec(shape, lambda i: (0,) * len(shape))
    r2, e2, c, e1 = pl.pallas_call(
        functools.partial(_peer_route_kernel, by_head=True),
        grid=(1,),
        in_specs=[full(h.shape), full(wq_t.shape), full(kexp_h.shape)],
        out_specs=[full(std.shape), full(std.shape), full(by_key.shape), full(by_key.shape)],
        out_shape=[std, std, by_key, by_key],
        scratch_shapes=_route_scratch(wq_t.shape[0], t),
        compiler_params=_cp("arbitrary"),
        name="peer_route_heads",
    )(h, wq_t, kexp_h)
    to_groups = lambda x: jnp.transpose(x, (1, 0, 2)).reshape(PEER_HEADS, PEER_KEYS, 1, t)
    return r2, e2, to_groups(c), to_groups(e1)


def _peer_route(h, wq_t, kexp):
    t, d = h.shape
    nblk = t // ROUTE_TOKENS
    nq = wq_t.shape[0]
    std = jax.ShapeDtypeStruct((PEER_HEADS, PEER_KEYS, t), BF16)
    grp = jax.ShapeDtypeStruct((PEER_HEADS, PEER_KEYS, t // LANES, LANES), F32)
    std_spec = pl.BlockSpec((None, PEER_KEYS, ROUTE_TOKENS), lambda i, hd: (hd, 0, i))
    grp_spec = pl.BlockSpec((None, PEER_KEYS, GROUPS, LANES), lambda i, hd: (hd, 0, i, 0))
    return pl.pallas_call(
        functools.partial(_peer_route_kernel, by_head=False),
        grid=(nblk, PEER_HEADS),
        in_specs=[pl.BlockSpec((ROUTE_TOKENS, d), lambda i, hd: (i, 0)),
                  pl.BlockSpec(wq_t.shape, lambda i, hd: (0, 0)),
                  pl.BlockSpec((None,) + kexp.shape[1:], lambda i, hd: (hd, 0, 0, 0))],
        out_specs=[std_spec, std_spec, grp_spec, grp_spec],
        out_shape=[std, std, grp, grp],
        scratch_shapes=_route_scratch(nq, ROUTE_TOKENS),
        compiler_params=_cp("parallel", "arbitrary"),
        name="peer_route",
    )(h, wq_t, kexp)


def _peer_dense_kernel(h_ref, u_ref, vt_ref, r2_ref, e2_ref, c_ref, e1_ref, o_ref):
    e = pl.program_id(1)
    tb = h_ref.shape[0]
    sub = r2_ref.shape[2]

    @pl.when(e == 0)
    def _():
        o_ref[...] = jnp.zeros_like(o_ref)

    tw = min(tb, PEER_UNIT_TOKENS)
    per = PEER_SUB // PEER_KEYS
    ngrp = tw // LANES
    if c_ref.shape[2] * LANES == tb:
        goff = 0
    else:
        goff = (pl.program_id(0) % (ROUTE_TOKENS // tb)) * ngrp
    units = [(q, tp) for q in range(u_ref.shape[0] // PEER_SUB) for tp in range(tb // tw)]

    def token_rows(ref, h, jj, tp):
        row = jnp.concatenate(
            [ref[h, jj, pl.ds(goff + tp * ngrp + g, 1), :] for g in range(ngrp)], axis=1)
        return jnp.broadcast_to(row, (sub, tw)).astype(BF16)[None]

    def scores(unit):
        q, tp = unit
        return _dot_nt(u_ref[q * PEER_SUB:(q + 1) * PEER_SUB, :],
                       h_ref[tp * tw:(tp + 1) * tw, :])

    accs = [o_ref[:, tp * tw:(tp + 1) * tw] for tp in range(tb // tw)]
    at_next = scores(units[0])
    for n, (q, tp) in enumerate(units):
        at = at_next
        if n + 1 < len(units):
            at_next = scores(units[n + 1])
        cols = slice(tp * tw, (tp + 1) * tw)
        gs = []
        for jj in range(q * per, (q + 1) * per):
            w = jnp.zeros((PEER_KEYS // sub, sub, tw), BF16)
            for h in range(PEER_HEADS):
                picked = jnp.where(r2_ref[h, :, :, cols] < token_rows(c_ref, h, jj, tp),
                                   e2_ref[h, :, :, cols], jnp.zeros_like(w))
                w = w + picked * token_rows(e1_ref, h, jj, tp)
            lo = (jj - q * per) * PEER_KEYS
            a = _gelu(at[lo:lo + PEER_KEYS, :]).astype(BF16)
            gs.append(a * w.reshape(PEER_KEYS, tw))
        accs[tp] = accs[tp] + _dot(vt_ref[:, q * PEER_SUB:(q + 1) * PEER_SUB],
                                   jnp.concatenate(gs, axis=0))
    for tp, acc in enumerate(accs):
        o_ref[:, tp * tw:(tp + 1) * tw] = acc


def _peer_dense(h, u_all, v_all, layer, r2, e2, c, e1, *, tb, te):
    t, d = h.shape
    ne = u_all.shape[1]
    sub = 16
    tr = r2.shape[2]
    r2 = r2.reshape(PEER_HEADS, PEER_KEYS // sub, sub, tr)
    e2 = e2.reshape(PEER_HEADS, PEER_KEYS // sub, sub, tr)
    nchunk = te // PEER_KEYS
    per_route = ROUTE_TOKENS // tb
    full = pl.BlockSpec((PEER_HEADS, PEER_KEYS // sub, sub, tb), lambda i, e: (0, 0, 0, i))
    ngroups = min(GROUPS, c.shape[2])
    rows = pl.BlockSpec((PEER_HEADS, nchunk, ngroups, LANES),
                        lambda i, e: (0, e, i // per_route, 0))
    return pl.pallas_call(
        _peer_dense_kernel,
        grid=(t // tb, ne // te),
        in_specs=[pl.BlockSpec((tb, d), lambda i, e: (i, 0)),
                  pl.BlockSpec((None, te, d), lambda i, e: (layer, e, 0)),
                  pl.BlockSpec((None, d, te), lambda i, e: (layer, 0, e)),
                  full, full, rows, rows],
        out_specs=pl.BlockSpec((d, tb), lambda i, e: (0, i)),
        out_shape=jax.ShapeDtypeStruct((d, t), F32),
        compiler_params=_cp("parallel", "arbitrary"),
        name="peer_dense",
    )(h, u_all, v_all, r2, e2, c, e1)


def _peer_out_kernel(ot_ref, x_ref, gate_ref, xo_ref):
    xo_ref[...] = x_ref[...] + gate_ref[...] * ot_ref[...].T


def _peer_out(ot, x, gate, *, tm, per_row):
    t, d = x.shape
    row = lambda i: (i, 0)
    return pl.pallas_call(
        _peer_out_kernel,
        grid=(t // tm,),
        in_specs=[pl.BlockSpec((d, tm), lambda i: (0, i)), pl.BlockSpec((tm, d), row),
                  _mod_spec(per_row, tm, d, False)],
        out_specs=pl.BlockSpec((tm, d), row),
        out_shape=jax.ShapeDtypeStruct((t, d), F32),
        compiler_params=_cp("parallel"),
        name="peer_out",
    )(ot, x, gate)


def _peer(h, x, gate, wq_t, kexp, kexp_h, u_all, v_all, layer, *, tb, tm, per_row):
    t = h.shape[0]
    if t == LANES:
        r2, e2, c, e1 = _peer_route_heads(h, wq_t, kexp_h)
    else:
        h_route = jnp.pad(h, ((0, -t % ROUTE_TOKENS), (0, 0)))
        r2, e2, c, e1 = _peer_route(h_route, wq_t, kexp)
    ot = _peer_dense(h, u_all, v_all, layer, r2, e2, c, e1, tb=tb, te=PEER_EXPERT_BLOCK)
    return _peer_out(ot, x, gate, tm=tm, per_row=per_row)


def kernel(x_prompt, x_sample, c_prompt, c_sample, cache_pool, cache_k, cache_v, cache_logf, w_ada, b_ada, norm1, norm2, w_in_ab, b_f, w_pool, pool_scale, q_gain, k_gain, w_out_ab, w_uv, sg_ln_g, sg_ln_b, w_s, b_s, w_out_sg, peer_wq, peer_keys, peer_u, peer_v):
    nb, seq, d = x_prompt.shape
    db, dl, _ = x_sample.shape
    depth = w_ada.shape[0]
    past = cache_k.shape[2]
    pool_hist = cache_pool.shape[2]
    assert nb == 1 and db * dl == LANES and dl == 16 and pool_hist == dl - 1
    fox_w = FOX_HEADS * HEAD_DIM
    pool_w = w_pool.shape[1] * w_pool.shape[2]
    ts = db * dl

    c_rows = jnp.concatenate(
        [c_prompt, c_sample, jnp.zeros((16 - nb - db, d), F32)], axis=0)
    mods = _ada(c_rows, w_ada, b_ada)

    def mod_p(layer, k):
        return mods[layer, 0:1, k * d:(k + 1) * d]

    def mod_s(layer, k):
        return jnp.repeat(mods[layer, 1:1 + db, k * d:(k + 1) * d], dl, axis=0)

    xp = x_prompt.reshape(seq, d)
    xs = x_sample.reshape(ts, d)
    tm_p = 512 if seq % 512 == 0 else 256
    outs = {k: [] for k in ("pool_p", "pool_s", "kp", "ks", "vp", "vs", "lfp", "lfs", "sgv")}
    u_all = peer_u.astype(BF16)
    v_all = jnp.swapaxes(peer_v, 1, 2).astype(BF16)

    for layer in range(depth):
        j = layer // 2
        g1 = norm1[layer].reshape(1, d)
        g2 = norm2[layer].reshape(1, d)
        if layer % 2 == 0:
            w_in = w_in_ab[j]
            w_main = w_in[:, :pool_w + 3 * fox_w].astype(BF16)
            wf_t = w_in[:, pool_w + 3 * fox_w:].T.astype(BF16)
            bfc = b_f[j].reshape(FOX_HEADS, 1)
            qg = q_gain[j].reshape(1, fox_w)
            kg = k_gain[j].reshape(1, fox_w)
            wp = w_pool[j].astype(BF16)
            psc = pool_scale[j].reshape(1, pool_w)
            wo = w_out_ab[j].astype(BF16)

            p, qb, k, kb, v, vb, lf, fc = _proj(
                xp, g1, mod_p(layer, 0), mod_p(layer, 1), w_main, wf_t, bfc, qg, kg,
                tm=tm_p, per_row=False, do_cumsum=True)
            a_out = _pool_prompt(p, wp, psc, tm=256)
            score_bound = 1.02 * (HEAD_DIM ** 0.5) * (jnp.max(jnp.abs(q_gain[j]), axis=-1)
                                                      * jnp.max(jnp.abs(k_gain[j]), axis=-1))
            b_out = _attn_prompt(qb, kb, vb, fc, score_bound, tq=tm_p)
            xp, hp = _merge(a_out, b_out, wo, xp, mod_p(layer, 2), g2, mod_p(layer, 3),
                            mod_p(layer, 4), tm=256, per_row=False)
            outs["pool_p"].append(p[-pool_hist:].reshape(nb, pool_hist, pool_w))
            outs["kp"].append(k.reshape(nb, seq, FOX_HEADS, HEAD_DIM))
            outs["vp"].append(v.reshape(nb, seq, FOX_HEADS, HEAD_DIM))
            outs["lfp"].append(lf.T.reshape(nb, seq, FOX_HEADS))

            p, qb, k, kb, v, vb, lf, _ = _proj(
                xs, g1, mod_s(layer, 0), mod_s(layer, 1), w_main, wf_t, bfc, qg, kg,
                tm=ts, per_row=True, do_cumsum=False)
            p3 = p.reshape(db, dl, pool_w)
            buf = jnp.concatenate([jnp.zeros((db, 1, pool_w), F32), cache_pool[j], p3], axis=1)
            a_out = _pool_sample(buf.reshape(db * 2 * dl, pool_w), p, wp, psc, pos0=past)
            b_out = _attn_sample(qb, kb, vb, lf, cache_k[j], cache_v[j], cache_logf[j],
                                 score_bound, tk=min(past, SAMPLE_KEY_CHUNK))
            xs, hs = _merge(a_out, b_out, wo, xs, mod_s(layer, 2), g2, mod_s(layer, 3),
                            mod_s(layer, 4), tm=ts, per_row=True)
            outs["pool_s"].append(jnp.concatenate([cache_pool[j], p3], axis=1)[:, -pool_hist:])
            outs["ks"].append(k.reshape(db, dl, FOX_HEADS, HEAD_DIM))
            outs["vs"].append(v.reshape(db, dl, FOX_HEADS, HEAD_DIM))
            outs["lfs"].append(lf.T.reshape(db, dl, FOX_HEADS))
        else:
            wuv = w_uv[j].astype(BF16)
            lng = sg_ln_g[j].reshape(1, -1)
            lnb = sg_ln_b[j].reshape(1, -1)
            wo = w_out_sg[j].astype(BF16)
            tri = jnp.tril(jnp.ones((SG_CHUNK, SG_CHUNK), bool))
            ws_p = jnp.where(tri, w_s[j], 0.0).astype(BF16)
            bs_p = b_s[j].T
            ws16 = jnp.where(tri[:dl, :dl], w_s[j][:, :dl, :dl], 0.0)
            ws_s = jnp.einsum("ab,gts->gatbs", jnp.eye(db, dtype=F32), ws16)
            ws_s = ws_s.reshape(SG_GROUPS, ts, ts).astype(BF16)
            bs_s = jnp.tile(b_s[j][:, :dl], (1, db)).T

            u, vn = _uv(xp, g1, mod_p(layer, 0), mod_p(layer, 1), wuv, lng, lnb,
                        tm=tm_p, per_row=False, vn_dtype=BF16)
            xp, hp = _sgate(u, vn, ws_p, bs_p, wo, xp, mod_p(layer, 2), g2, mod_p(layer, 3),
                            mod_p(layer, 4), tm=256, per_row=False)
            u, vn = _uv(xs, g1, mod_s(layer, 0), mod_s(layer, 1), wuv, lng, lnb,
                        tm=ts, per_row=True, vn_dtype=F32)
            xs, hs = _sgate(u, vn, ws_s, bs_s, wo, xs, mod_s(layer, 2), g2, mod_s(layer, 3),
                            mod_s(layer, 4), tm=ts, per_row=True)
            outs["sgv"].append(vn.reshape(db, dl, -1))

        wq_t = peer_wq[layer].T.astype(BF16)
        kexp = jnp.einsum("hskd,gj->hskgjd", peer_keys[layer].astype(BF16),
                          jnp.eye(GROUPS, dtype=BF16))
        kexp = kexp.reshape(PEER_HEADS, 2, PEER_KEYS * GROUPS, GROUPS * kexp.shape[-1])
        kexp_h = jnp.einsum("hskd,hj->skhjd", peer_keys[layer].astype(BF16),
                            jnp.eye(PEER_HEADS, dtype=BF16))
        kexp_h = kexp_h.reshape(2, PEER_KEYS * PEER_HEADS, PEER_HEADS * kexp_h.shape[-1])
        xp = _peer(hp, xp, mod_p(layer, 5), wq_t, kexp, kexp_h, u_all, v_all, layer,
                   tb=PEER_TOKEN_BLOCK if seq % PEER_TOKEN_BLOCK == 0 else 256, tm=256,
                   per_row=False)
        xs = _peer(hs, xs, mod_s(layer, 5), wq_t, kexp, kexp_h, u_all, v_all, layer,
                   tb=ts, tm=ts, per_row=True)

    st = lambda key: jnp.stack(outs[key])
    return (xp.reshape(nb, seq, d), xs.reshape(db, dl, d), st("pool_p"), st("pool_s"),
            st("kp"), st("ks"), st("vp"), st("vs"), st("lfp"), st("lfs"), st("sgv"))
```

```python
import functools

import jax
import jax.numpy as jnp
from jax import lax
from jax.experimental import pallas as pl
from jax.experimental.pallas import tpu as pltpu

F32 = jnp.float32
BF16 = jnp.bfloat16
EPS = 1e-6

LANES = 128
POOL_WINDOWS = (2, 4, 8, 16)
POOL_HALO = 128
FOX_HEADS = 8
HEAD_DIM = 128
ATTN_HEADS_PER_STEP = 2
SAMPLE_KEY_CHUNK = 512
SG_GROUPS = 8
SG_CHUNK = 128
PEER_HEADS = 8
PEER_KEYS = 128
PEER_TOPK = 16
PEER_TOKEN_BLOCK = 512
PEER_EXPERT_BLOCK = 1024
PEER_SUB = 256
PEER_UNIT_TOKENS = 1024
NEG_BIG = -1e30
EXP_ZERO_BELOW = 110.0
RANK_NONE = 64.0
VMEM_LIMIT = 56 * 1024 * 1024

NT_DIMS = (((1,), (1,)), ((), ()))


def _cp(*sem):
    return pltpu.CompilerParams(dimension_semantics=sem, vmem_limit_bytes=VMEM_LIMIT)


def _dot(a, b):
    return jnp.dot(a, b, preferred_element_type=F32)


def _dot_nt(a, b):
    return lax.dot_general(a, b, NT_DIMS, preferred_element_type=F32)


def _modulate(x, g, shift, scale):
    ms = jnp.mean(x * x, axis=-1, keepdims=True)
    return x * lax.rsqrt(ms + EPS) * g * (1.0 + scale) + shift


def _gelu(x):
    c = 0.7978845608028654
    half = 0.5 * x
    return half + half * jnp.tanh(x * (c + (c * 0.044715) * (x * x)))


def _log_sigmoid(x):
    return -(jnp.maximum(-x, 0.0) + jnp.log(1.0 + jnp.exp(-jnp.abs(x))))


def _lane_cumsum(x):
    n = x.shape[-1]
    lane = lax.broadcasted_iota(jnp.int32, x.shape, x.ndim - 1)
    s = 1
    while s < n:
        x = x + jnp.where(lane >= s, pltpu.roll(x, s, x.ndim - 1), 0.0)
        s *= 2
    return x


def _ada_kernel(c_ref, w_ref, b_ref, o_ref):
    c = c_ref[...]
    a = (c * (1.0 / (1.0 + jnp.exp(-c)))).astype(BF16)
    o_ref[0] = _dot(a, w_ref[0].astype(BF16)) + b_ref[0]


def _ada(c_rows, w_ada, b_ada):
    nl, d, n = w_ada.shape
    r = c_rows.shape[0]
    tn = 1024
    return pl.pallas_call(
        _ada_kernel,
        grid=(nl, n // tn),
        in_specs=[pl.BlockSpec((r, d), lambda l, j: (0, 0)),
                  pl.BlockSpec((1, d, tn), lambda l, j: (l, 0, j)),
                  pl.BlockSpec((1, 1, tn), lambda l, j: (l, 0, j))],
        out_specs=pl.BlockSpec((1, r, tn), lambda l, j: (l, 0, j)),
        out_shape=jax.ShapeDtypeStruct((nl, r, n), F32),
        compiler_params=_cp("parallel", "parallel"),
        name="ada",
    )(c_rows, w_ada, b_ada.reshape(nl, 1, n))


def _mod_spec(per_row, tm, d, nj):
    if nj:
        return (pl.BlockSpec((tm, d), lambda i, j: (i, 0)) if per_row
                else pl.BlockSpec((1, d), lambda i, j: (0, 0)))
    return (pl.BlockSpec((tm, d), lambda i: (i, 0)) if per_row
            else pl.BlockSpec((1, d), lambda i: (0, 0)))


def _head_rms(z, gain):
    outs = []
    for h in range(FOX_HEADS):
        zh = z[:, h * HEAD_DIM:(h + 1) * HEAD_DIM]
        ms = jnp.mean(zh * zh, axis=-1, keepdims=True)
        outs.append(zh * lax.rsqrt(ms + EPS))
    return jnp.concatenate(outs, axis=-1) * gain


def _proj_kernel(x_ref, g_ref, sh_ref, sc_ref, w_ref, wf_ref, bf_ref, qg_ref, kg_ref,
                 p_ref, q_ref, k_ref, kb_ref, v_ref, vb_ref, lf_ref, fc_ref,
                 h_scr, carry_scr, *, do_cumsum):
    i = pl.program_id(0)
    j = pl.program_id(1)

    @pl.when(j == 0)
    def _():
        hb = _modulate(x_ref[...], g_ref[...], sh_ref[...], sc_ref[...]).astype(BF16)
        h_scr[...] = hb
        lf = _log_sigmoid(_dot_nt(wf_ref[...], hb) + bf_ref[...])
        lf_ref[...] = lf
        if do_cumsum:
            @pl.when(i == 0)
            def _():
                carry_scr[...] = jnp.zeros_like(carry_scr)
            fc = _lane_cumsum(lf) + carry_scr[:, 0:1]
            fc_ref[...] = fc
            carry_scr[...] = jnp.broadcast_to(fc[:, -1:], carry_scr.shape)
        else:
            fc_ref[...] = lf

    z = _dot(h_scr[...], w_ref[...])

    @pl.when(j == 0)
    def _():
        p_ref[...] = z

    @pl.when(j == 1)
    def _():
        q_ref[...] = (_head_rms(z, qg_ref[...]) * (HEAD_DIM ** -0.5)).astype(BF16)

    @pl.when(j == 2)
    def _():
        kn = _head_rms(z, kg_ref[...])
        k_ref[...] = kn
        kb_ref[...] = kn.astype(BF16)

    @pl.when(j == 3)
    def _():
        v_ref[...] = z
        vb_ref[...] = z.astype(BF16)


def _proj(x, g, shift, scale, w_main, wf_t, b_f, q_gain, k_gain, *, tm, per_row, do_cumsum):
    t, d = x.shape
    w4 = w_main.shape[1] // 4
    nh = wf_t.shape[0]
    row = lambda i, j: (i, 0)
    const = lambda i, j: (0, 0)
    tok_f32 = jax.ShapeDtypeStruct((t, w4), F32)
    tok_b16 = jax.ShapeDtypeStruct((t, w4), BF16)
    head_t = jax.ShapeDtypeStruct((nh, t), F32)
    return pl.pallas_call(
        functools.partial(_proj_kernel, do_cumsum=do_cumsum),
        grid=(t // tm, 4),
        in_specs=[pl.BlockSpec((tm, d), row),
                  pl.BlockSpec((1, d), const),
                  _mod_spec(per_row, tm, d, True),
                  _mod_spec(per_row, tm, d, True),
                  pl.BlockSpec((d, w4), lambda i, j: (0, j)),
                  pl.BlockSpec((nh, d), const),
                  pl.BlockSpec((nh, 1), const),
                  pl.BlockSpec((1, w4), const),
                  pl.BlockSpec((1, w4), const)],
        out_specs=[pl.BlockSpec((tm, w4), row)] * 6
                  + [pl.BlockSpec((nh, tm), lambda i, j: (0, i))] * 2,
        out_shape=[tok_f32, tok_b16, tok_f32, tok_b16, tok_f32, tok_b16, head_t, head_t],
        scratch_shapes=[pltpu.VMEM((tm, d), BF16), pltpu.VMEM((nh, LANES), F32)],
        compiler_params=_cp("arbitrary", "arbitrary"),
        name="proj",
    )(x, g, shift, scale, w_main, wf_t, b_f, q_gain, k_gain)


def _pool_kernel(ext_ref, tok_ref, w_ref, ps_ref, o_ref, *, prompt, pos0):
    i = pl.program_id(0)
    tok = tok_ref[...]
    tm = tok.shape[0]
    if prompt:
        halo = jnp.where(i == 0, 0.0, ext_ref[...])
        ext = jnp.concatenate([halo, tok], axis=0)
    else:
        ext = ext_ref[...]
    ke = ext.shape[0]
    hi = ext.astype(BF16)
    lo = (ext - hi.astype(F32)).astype(BF16)
    m_io = lax.broadcasted_iota(jnp.int32, (tm, ke), 0)
    c_io = lax.broadcasted_iota(jnp.int32, (tm, ke), 1)
    r_io = lax.broadcasted_iota(jnp.int32, (tm, 1), 0)
    if prompt:
        tgt = m_io + POOL_HALO
        pos = pos0 + i * tm + r_io
    else:
        tgt = m_io + ((m_io >> 4) << 4) + 16
        pos = pos0 + (r_io & 15)
    dlt = tgt - c_io
    gw = w_ref.shape[1]
    for gi, win in enumerate(POOL_WINDOWS):
        sl = slice(gi * gw, (gi + 1) * gw)
        band = jnp.where(dlt >= 0, jnp.where(dlt < win, 1.0, 0.0), 0.0).astype(BF16)
        wsum = _dot(band, hi[:, sl]) + _dot(band, lo[:, sl])
        cnt = jnp.minimum(win, pos + 1).astype(F32)
        dd = wsum / cnt - tok[:, sl]
        o = _dot(dd.astype(BF16), w_ref[gi]) * ps_ref[:, sl]
        o_ref[:, sl] = o.astype(o_ref.dtype)


def _pool_prompt(p, w_pool, pool_scale, *, tm):
    t, c = p.shape
    per = tm // POOL_HALO
    return pl.pallas_call(
        functools.partial(_pool_kernel, prompt=True, pos0=0),
        grid=(t // tm,),
        in_specs=[pl.BlockSpec((POOL_HALO, c), lambda i: (jnp.maximum(i * per - 1, 0), 0)),
                  pl.BlockSpec((tm, c), lambda i: (i, 0)),
                  pl.BlockSpec(w_pool.shape, lambda i: (0, 0, 0)),
                  pl.BlockSpec((1, c), lambda i: (0, 0))],
        out_specs=pl.BlockSpec((tm, c), lambda i: (i, 0)),
        out_shape=jax.ShapeDtypeStruct((t, c), BF16),
        compiler_params=_cp("parallel"),
        name="pool_prompt",
    )(p, p, w_pool, pool_scale)


def _pool_sample(buf, p, w_pool, pool_scale, *, pos0):
    t, c = p.shape
    return pl.pallas_call(
        functools.partial(_pool_kernel, prompt=False, pos0=pos0),
        grid=(1,),
        in_specs=[pl.BlockSpec(buf.shape, lambda i: (0, 0)),
                  pl.BlockSpec((t, c), lambda i: (0, 0)),
                  pl.BlockSpec(w_pool.shape, lambda i: (0, 0, 0)),
                  pl.BlockSpec((1, c), lambda i: (0, 0))],
        out_specs=pl.BlockSpec((t, c), lambda i: (0, 0)),
        out_shape=jax.ShapeDtypeStruct((t, c), BF16),
        compiler_params=_cp("arbitrary"),
        name="pool_sample",
    )(buf, p, w_pool, pool_scale)


def _softmax_step(carry, s, v):
    m, l, acc = carry
    m_new = jnp.maximum(m, jnp.max(s, axis=-1, keepdims=True))
    alpha = jnp.exp(m - m_new)
    p = jnp.exp(s - m_new)
    l = alpha * l + jnp.sum(p, axis=-1, keepdims=True)
    acc = alpha * acc + _dot(p.astype(BF16), v)
    return m_new, l, acc


def _attn_prompt_kernel(fend_ref, fstart_ref, bound_ref, q_ref, k_ref, v_ref, fk_ref, o_ref,
                        *, tq, nhead):
    hg = pl.program_id(0)
    i = pl.program_id(1)
    lanes = [slice(a * HEAD_DIM, (a + 1) * HEAD_DIM) for a in range(nhead)]
    qs = [q_ref[:, ln] for ln in lanes]

    def chunk(kj, carries, masked):
        off = pl.multiple_of(kj * tq, tq)
        out = []
        for a, ln in enumerate(lanes):
            s = _dot_nt(qs[a], k_ref[pl.ds(off, tq), ln]) - fk_ref[a, :, pl.ds(off, tq)]
            if masked:
                r_io = lax.broadcasted_iota(jnp.int32, (tq, tq), 0)
                c_io = lax.broadcasted_iota(jnp.int32, (tq, tq), 1)
                s = jnp.where(c_io <= r_io, s, NEG_BIG)
            out.append(_softmax_step(carries[a], s, v_ref[pl.ds(off, tq), ln]))
        return tuple(out)

    dead = i
    for a in range(nhead):
        h = hg * nhead + a
        limit = fstart_ref[h, i] + bound_ref[h]
        dead = jnp.minimum(dead, lax.fori_loop(
            0, i, lambda kj, n: n + jnp.where(fend_ref[h, kj] > limit, 1, 0), jnp.int32(0)))
    init = tuple((jnp.full((tq, 1), NEG_BIG, F32), jnp.zeros((tq, 1), F32),
                  jnp.zeros((tq, HEAD_DIM), F32)) for _ in lanes)
    carries = lax.fori_loop(dead, i, lambda kj, c: chunk(kj, c, False), init)
    for ln, (_, l, acc) in zip(lanes, chunk(i, carries, True)):
        o_ref[:, ln] = (acc / l).astype(o_ref.dtype)


def _attn_prompt(qb, kb, vb, fcum, score_bound, *, tq):
    t, w = qb.shape
    nh = w // HEAD_DIM
    fend = fcum[:, tq - 1::tq]
    fstart = fcum[:, ::tq]
    bound = 2.0 * score_bound + EXP_ZERO_BELOW
    smem = pl.BlockSpec(memory_space=pltpu.SMEM)
    nhead = ATTN_HEADS_PER_STEP
    wide = nhead * HEAD_DIM
    return pl.pallas_call(
        functools.partial(_attn_prompt_kernel, tq=tq, nhead=nhead),
        grid=(nh // nhead, t // tq),
        in_specs=[smem, smem, smem,
                  pl.BlockSpec((tq, wide), lambda h, i: (i, h)),
                  pl.BlockSpec((t, wide), lambda h, i: (0, h)),
                  pl.BlockSpec((t, wide), lambda h, i: (0, h)),
                  pl.BlockSpec((nhead, 1, t), lambda h, i: (h, 0, 0))],
        out_specs=pl.BlockSpec((tq, wide), lambda h, i: (i, h)),
        out_shape=jax.ShapeDtypeStruct((t, w), BF16),
        compiler_params=_cp("parallel", "parallel"),
        name="attn_prompt",
    )(fend, fstart, bound, qb, kb, vb, fcum.reshape(nh, 1, t))


def _strided_lane_cumsum(x, stride, seg=None):
    n = seg or x.shape[-1]
    lane = lax.broadcasted_iota(jnp.int32, x.shape, x.ndim - 1)
    pos = lane if seg is None else lane & (seg - 1)
    s = stride
    while s < n:
        x = x + jnp.where(pos >= s, pltpu.roll(x, s, x.ndim - 1), 0.0)
        s *= 2
    return x


def _tile_last_lanes(x, period):
    n = x.shape[-1]
    lane = lax.broadcasted_iota(jnp.int32, x.shape, x.ndim - 1)
    y = jnp.where(lane >= n - period, x, 0.0)
    s = period
    while s < n:
        y = y + pltpu.roll(y, n - s, x.ndim - 1)
        s *= 2
    return y


def _chunk_suffix_kernel(lf_ref, o_ref, *, tk):
    x = lf_ref[...]
    p = x.shape[1]
    pos = lax.broadcasted_iota(jnp.int32, (p, LANES), 0)
    col = lax.broadcasted_iota(jnp.int32, (p, LANES), 1)
    later = jnp.where(pos >= (col + 1) * tk, 1.0, 0.0).astype(BF16)
    hi = x.astype(BF16)
    r1 = x - hi.astype(F32)
    mid = r1.astype(BF16)
    lo = (r1 - mid.astype(F32)).astype(BF16)
    o_ref[...] = _dot(hi, later) + _dot(mid, later) + _dot(lo, later)


def _chunk_suffix(lf_rows, *, tk):
    r, p = lf_rows.shape
    return pl.pallas_call(
        functools.partial(_chunk_suffix_kernel, tk=tk),
        grid=(1,),
        in_specs=[pl.BlockSpec((r, p), lambda i: (0, 0))],
        out_specs=pl.BlockSpec((r, LANES), lambda i: (0, 0)),
        out_shape=jax.ShapeDtypeStruct((r, LANES), F32),
        compiler_params=_cp("arbitrary"),
        name="cache_decay",
    )(lf_rows)


def _attn_sample_kernel(need_ref, q_ref, ck_ref, cv_ref, clf_ref, kn_ref, vn_ref, lfn_ref, o_ref,
                        q_scr, m_scr, l_scr, acc_scr, car_scr):
    b = pl.program_id(0)
    c = pl.program_id(1)
    nl = q_ref.shape[0]
    rows = q_scr.shape[0]

    def stack_heads(x):
        return jnp.concatenate(
            [x[:, h * HEAD_DIM:(h + 1) * HEAD_DIM] for h in range(FOX_HEADS)], axis=0)

    @pl.when(c == 0)
    def _():
        qs = stack_heads(q_ref[...])
        q_scr[...] = qs
        kn = stack_heads(kn_ref[...])
        r_io = lax.broadcasted_iota(jnp.int32, (rows, rows), 0)
        c_io = lax.broadcasted_iota(jnp.int32, (rows, rows), 1)
        ncum = _strided_lane_cumsum(lfn_ref[...], 1, seg=nl)
        sn = jnp.where((r_io // nl) == (c_io // nl),
                       jnp.where((c_io % nl) <= (r_io % nl), _dot_nt(qs, kn) - ncum, NEG_BIG),
                       NEG_BIG)
        init = (jnp.full((rows, 1), NEG_BIG, F32), jnp.zeros((rows, 1), F32),
                jnp.zeros((rows, HEAD_DIM), F32))
        m_scr[...], l_scr[...], acc_scr[...] = _softmax_step(init, sn, stack_heads(vn_ref[...]))
        car_scr[...] = jnp.zeros_like(car_scr)

    @pl.when(c < need_ref[b])
    def _():
        n = clf_ref.shape[1]
        incl = _strided_lane_cumsum(clf_ref[...], FOX_HEADS)
        total = _tile_last_lanes(incl, FOX_HEADS)
        after = car_scr[...] + total - incl
        car_scr[...] = car_scr[...] + total
        r_io = lax.broadcasted_iota(jnp.int32, (rows, n), 0)
        c_io = lax.broadcasted_iota(jnp.int32, (rows, n), 1)
        s = _dot_nt(q_scr[...], ck_ref[...].astype(BF16)) + after
        s = jnp.where((c_io % FOX_HEADS) == (r_io // nl), s, NEG_BIG)
        carry = _softmax_step((m_scr[...], l_scr[...], acc_scr[...]), s, cv_ref[...].astype(BF16))
        m_scr[...], l_scr[...], acc_scr[...] = carry

    @pl.when(c == pl.num_programs(1) - 1)
    def _():
        o = acc_scr[...] / l_scr[...]
        o_ref[...] = jnp.concatenate(
            [o[h * nl:(h + 1) * nl, :] for h in range(FOX_HEADS)], axis=-1).astype(o_ref.dtype)


def _attn_sample(qb, kb, vb, lf_new, cache_k, cache_v, cache_lf, score_bound, *, tk):
    b, past, nh, dh = cache_k.shape
    w = nh * dh
    nl = qb.shape[0] // b
    rows = nh * nl
    nch = past // tk
    after = _chunk_suffix(jnp.transpose(cache_lf, (0, 2, 1)).reshape(b * nh, past), tk=tk)
    alive = (after.reshape(b, nh, LANES)[:, :, :nch]
             + 2.0 * score_bound[None, :, None]) >= -EXP_ZERO_BELOW
    need = jnp.maximum(jnp.max(jnp.sum(alive, axis=2), axis=1), 1).astype(jnp.int32)
    lfn = jnp.transpose(lf_new.reshape(nh, b, nl), (1, 0, 2)).reshape(b, 1, rows)

    def newest_first(c, need_s):
        return nch - 1 - jnp.minimum(c, need_s - 1)

    chunk = lambda s, c, need: (s, newest_first(c, need[s]), 0)
    new = lambda s, c, need: (s, 0)
    return pl.pallas_call(
        _attn_sample_kernel,
        grid_spec=pltpu.PrefetchScalarGridSpec(
            num_scalar_prefetch=1,
            grid=(b, nch),
            in_specs=[pl.BlockSpec((nl, w), new),
                      pl.BlockSpec((None, tk * nh, dh), chunk),
                      pl.BlockSpec((None, tk * nh, dh), chunk),
                      pl.BlockSpec((None, 1, tk * nh),
                                   lambda s, c, need: (s, 0, newest_first(c, need[s]))),
                      pl.BlockSpec((nl, w), new),
                      pl.BlockSpec((nl, w), new),
                      pl.BlockSpec((None, 1, rows), lambda s, c, need: (s, 0, 0))],
            out_specs=pl.BlockSpec((nl, w), new),
            scratch_shapes=[pltpu.VMEM((rows, dh), BF16), pltpu.VMEM((rows, 1), F32),
                            pltpu.VMEM((rows, 1), F32), pltpu.VMEM((rows, dh), F32),
                            pltpu.VMEM((1, tk * nh), F32)]),
        out_shape=jax.ShapeDtypeStruct(qb.shape, BF16),
        compiler_params=_cp("parallel", "arbitrary"),
        name="attn_sample",
    )(need, qb, cache_k.reshape(b, past * nh, dh), cache_v.reshape(b, past * nh, dh),
      cache_lf.reshape(b, 1, past * nh), kb, vb, lfn)


def _residual_mod(x, gate, o, g2, sh2, sc2, x_ref, h_ref):
    xn = x + gate * o
    x_ref[...] = xn
    h_ref[...] = _modulate(xn, g2, sh2, sc2).astype(h_ref.dtype)


def _merge_kernel(a_ref, b_ref, w_ref, x_ref, gate_ref, g2_ref, sh2_ref, sc2_ref, xo_ref, h_ref):
    ca = a_ref.shape[1]
    o = _dot(a_ref[...], w_ref[:ca, :]) + _dot(b_ref[...], w_ref[ca:, :])
    _residual_mod(x_ref[...], gate_ref[...], o, g2_ref[...], sh2_ref[...], sc2_ref[...],
                  xo_ref, h_ref)


def _merge(a, b, w_out, x, gate, g2, sh2, sc2, *, tm, per_row):
    t, d = x.shape
    ca = a.shape[1]
    row = lambda i: (i, 0)
    mod = _mod_spec(per_row, tm, d, False)
    return pl.pallas_call(
        _merge_kernel,
        grid=(t // tm,),
        in_specs=[pl.BlockSpec((tm, ca), row), pl.BlockSpec((tm, ca), row),
                  pl.BlockSpec(w_out.shape, lambda i: (0, 0)),
                  pl.BlockSpec((tm, d), row), mod,
                  pl.BlockSpec((1, d), lambda i: (0, 0)), mod, mod],
        out_specs=[pl.BlockSpec((tm, d), row), pl.BlockSpec((tm, d), row)],
        out_shape=[jax.ShapeDtypeStruct((t, d), F32), jax.ShapeDtypeStruct((t, d), BF16)],
        compiler_params=_cp("parallel"),
        name="merge",
    )(a, b, w_out, x, gate, g2, sh2, sc2)


def _uv_kernel(x_ref, g_ref, sh_ref, sc_ref, w_ref, lng_ref, lnb_ref, u_ref, vn_ref, h_scr):
    j = pl.program_id(1)

    @pl.when(j == 0)
    def _():
        h_scr[...] = _modulate(x_ref[...], g_ref[...], sh_ref[...], sc_ref[...]).astype(BF16)

    z = _gelu(_dot(h_scr[...], w_ref[...]))

    @pl.when(j == 0)
    def _():
        u_ref[...] = z.astype(u_ref.dtype)

    @pl.when(j == 1)
    def _():
        mu = jnp.mean(z, axis=-1, keepdims=True)
        zc = z - mu
        var = jnp.mean(zc * zc, axis=-1, keepdims=True)
        vn_ref[...] = (zc * lax.rsqrt(var + EPS) * lng_ref[...] + lnb_ref[...]).astype(vn_ref.dtype)


def _uv(x, g, shift, scale, w_uv, ln_g, ln_b, *, tm, per_row, vn_dtype):
    t, d = x.shape
    sw = w_uv.shape[1] // 2
    row = lambda i, j: (i, 0)
    const = lambda i, j: (0, 0)
    return pl.pallas_call(
        _uv_kernel,
        grid=(t // tm, 2),
        in_specs=[pl.BlockSpec((tm, d), row), pl.BlockSpec((1, d), const),
                  _mod_spec(per_row, tm, d, True), _mod_spec(per_row, tm, d, True),
                  pl.BlockSpec((d, sw), lambda i, j: (0, j)),
                  pl.BlockSpec((1, sw), const), pl.BlockSpec((1, sw), const)],
        out_specs=[pl.BlockSpec((tm, sw), row), pl.BlockSpec((tm, sw), row)],
        out_shape=[jax.ShapeDtypeStruct((t, sw), BF16), jax.ShapeDtypeStruct((t, sw), vn_dtype)],
        scratch_shapes=[pltpu.VMEM((tm, d), BF16)],
        compiler_params=_cp("parallel", "arbitrary"),
        name="sg_uv",
    )(x, g, shift, scale, w_uv, ln_g, ln_b)


def _sgate_kernel(u_ref, vn_ref, ws_ref, bs_ref, w_ref, x_ref, gate_ref, g2_ref, sh2_ref, sc2_ref,
                  xo_ref, h_ref, y_scr):
    tm, sw = u_ref.shape
    gw = sw // SG_GROUPS
    for c in range(tm // SG_CHUNK):
        rs = slice(c * SG_CHUNK, (c + 1) * SG_CHUNK)
        for g in range(SG_GROUPS):
            cs = slice(g * gw, (g + 1) * gw)
            sv = _dot(ws_ref[g], vn_ref[rs, cs].astype(BF16)) + bs_ref[:, g:g + 1]
            y_scr[rs, cs] = (u_ref[rs, cs].astype(F32) * sv).astype(BF16)
    o = _dot(y_scr[...], w_ref[...])
    _residual_mod(x_ref[...], gate_ref[...], o, g2_ref[...], sh2_ref[...], sc2_ref[...],
                  xo_ref, h_ref)


def _sgate(u, vn, ws, bs_t, w_out, x, gate, g2, sh2, sc2, *, tm, per_row):
    t, d = x.shape
    sw = u.shape[1]
    row = lambda i: (i, 0)
    mod = _mod_spec(per_row, tm, d, False)
    return pl.pallas_call(
        _sgate_kernel,
        grid=(t // tm,),
        in_specs=[pl.BlockSpec((tm, sw), row), pl.BlockSpec((tm, sw), row),
                  pl.BlockSpec(ws.shape, lambda i: (0, 0, 0)),
                  pl.BlockSpec(bs_t.shape, lambda i: (0, 0)),
                  pl.BlockSpec(w_out.shape, lambda i: (0, 0)),
                  pl.BlockSpec((tm, d), row), mod,
                  pl.BlockSpec((1, d), lambda i: (0, 0)), mod, mod],
        out_specs=[pl.BlockSpec((tm, d), row), pl.BlockSpec((tm, d), row)],
        out_shape=[jax.ShapeDtypeStruct((t, d), F32), jax.ShapeDtypeStruct((t, d), BF16)],
        scratch_shapes=[pltpu.VMEM((tm, sw), BF16)],
        compiler_params=_cp("parallel"),
        name="sg_gate",
    )(u, vn, ws, bs_t, w_out, x, gate, g2, sh2, sc2)


GROUPS = 8
ROUTE_TOKENS = GROUPS * LANES
PAIR_R1 = [0] * 16 + [a for a in range(1, 8) for _ in range(8)] + list(range(8, 16))
PAIR_R2 = list(range(16)) + [i for _ in range(1, 8) for i in range(8)] + [0] * 8
PAIR_FLAT = [a * PEER_TOPK + b for a, b in zip(PAIR_R1, PAIR_R2)]


def _extract_topk(problems, vals_refs, idx_refs):
    chains = 4

    def one_round(r, prev):
        picked = []
        for p, (ref, ids) in enumerate(problems):
            n = len(ids)
            per = -(-n // chains)
            best = []
            for c0 in range(0, n, per):
                m = ix = None
                for e in range(c0, min(c0 + per, n)):
                    x = jnp.where(prev[p] == float(ids[e]), -jnp.inf, ref[e])
                    ref[e] = x
                    if m is None:
                        m, ix = x, jnp.full(x.shape, float(ids[e]), F32)
                    else:
                        ix = jnp.where(x > m, float(ids[e]), ix)
                        m = jnp.maximum(m, x)
                best.append((m, ix))
            m, ix = best[0]
            for mc, ic in best[1:]:
                ix = jnp.where(mc > m, ic, ix)
                m = jnp.maximum(m, mc)
            vals_refs[p][r] = m
            idx_refs[p][r] = ix
            picked.append(ix)
        return tuple(picked)

    none = jnp.full((GROUPS, LANES), -1.0, F32)
    return lax.fori_loop(0, PEER_TOPK, one_round, tuple(none for _ in problems))


def _peer_route_kernel(h_ref, wq_ref, kexp_ref, r2_ref, e2_ref, c_ref, e1_ref,
                       qt_scr, s_scr, so_scr, vals_scr, idx_scr, cand_scr, cval_scr, cidx_scr,
                       t2_scr, *, by_head):
    if by_head:
        qt_scr[...] = _dot_nt(wq_ref[...], h_ref[...]).astype(BF16)

        def stacked_queries(side):
            return jnp.concatenate(
                [qt_scr[(2 * h + side) * PEER_KEYS:(2 * h + side + 1) * PEER_KEYS, :]
                 for h in range(PEER_HEADS)], axis=0)
    else:
        hd = pl.program_id(1)

        @pl.when(hd == 0)
        def _():
            qt_scr[...] = _dot_nt(wq_ref[...], h_ref[...]).astype(BF16)

        row = pl.multiple_of(hd * 2 * PEER_KEYS, 2 * PEER_KEYS)

        def stacked_queries(side):
            return jnp.concatenate(
                [qt_scr[pl.ds(row + side * PEER_KEYS, PEER_KEYS), g * LANES:(g + 1) * LANES]
                 for g in range(GROUPS)], axis=0)

    for side in range(2):
        s = _dot(kexp_ref[side], stacked_queries(side)).reshape(PEER_KEYS, GROUPS, LANES)
        s_scr[side] = s
        so_scr[side] = s

    keys = list(range(PEER_KEYS))
    _extract_topk([(s_scr.at[0], keys), (s_scr.at[1], keys)],
                  [vals_scr.at[0], vals_scr.at[1]], [idx_scr.at[0], idx_scr.at[1]])
    v1 = [vals_scr[0, r] for r in range(PEER_TOPK)]
    v2 = [vals_scr[1, r] for r in range(PEER_TOPK)]
    ncand = len(PAIR_FLAT)
    for c in range(ncand):
        cand_scr[c] = v1[PAIR_R1[c]] + v2[PAIR_R2[c]]
    (last,) = _extract_topk([(cand_scr, PAIR_FLAT)], [cval_scr], [cidx_scr])

    top = v1[0] + v2[0]
    z = jnp.zeros((GROUPS, LANES), F32)
    for r in range(PEER_TOPK):
        z = z + jnp.exp(cval_scr[r] - top)
    inv_z = 1.0 / z
    cnt = [jnp.zeros((GROUPS, LANES), F32) for _ in range(PEER_TOPK)]
    for c in range(ncand):
        picked = jnp.where(cand_scr[c] == -jnp.inf, 1.0,
                           jnp.where(last == float(PAIR_FLAT[c]), 1.0, 0.0))
        cnt[PAIR_R1[c]] = cnt[PAIR_R1[c]] + picked
    idx1 = [idx_scr[0, r] for r in range(PEER_TOPK)]
    idx2 = [idx_scr[1, r] for r in range(PEER_TOPK)]

    def per_key(k, _):
        kf = lax.convert_element_type(k, F32)
        rank2 = jnp.full((GROUPS, LANES), RANK_NONE, F32)
        ck = jnp.zeros((GROUPS, LANES), F32)
        for r in range(PEER_TOPK):
            rank2 = jnp.where(idx2[r] == kf, float(r), rank2)
            ck = jnp.where(idx1[r] == kf, cnt[r], ck)
        base = pl.multiple_of(k * GROUPS, GROUPS)
        t2_scr[0, pl.ds(base, GROUPS), :] = rank2
        t2_scr[1, pl.ds(base, GROUPS), :] = jnp.exp(so_scr[1, k] - v2[0])
        c_ref[k] = ck
        e1_ref[k] = jnp.exp(so_scr[0, k] - v1[0]) * inv_z
        return 0

    lax.fori_loop(0, PEER_KEYS, per_key, 0, unroll=4)
    for g in range(GROUPS):
        rank2 = t2_scr[0, pl.ds(g, PEER_KEYS, stride=GROUPS), :].astype(BF16)
        fac2 = t2_scr[1, pl.ds(g, PEER_KEYS, stride=GROUPS), :].astype(BF16)
        if by_head:
            r2_ref[g] = rank2
            e2_ref[g] = fac2
        else:
            r2_ref[:, g * LANES:(g + 1) * LANES] = rank2
            e2_ref[:, g * LANES:(g + 1) * LANES] = fac2


def _route_scratch(nq, tokens):
    tile = (GROUPS, LANES)
    return [pltpu.VMEM((nq, tokens), BF16),
            pltpu.VMEM((2, PEER_KEYS) + tile, F32), pltpu.VMEM((2, PEER_KEYS) + tile, F32),
            pltpu.VMEM((2, PEER_TOPK) + tile, F32), pltpu.VMEM((2, PEER_TOPK) + tile, F32),
            pltpu.VMEM((len(PAIR_FLAT),) + tile, F32),
            pltpu.VMEM((PEER_TOPK,) + tile, F32), pltpu.VMEM((PEER_TOPK,) + tile, F32),
            pltpu.VMEM((2, PEER_KEYS * GROUPS, LANES), F32)]


def _peer_route_heads(h, wq_t, kexp_h):
    t, d = h.shape
    assert t == LANES and GROUPS == PEER_HEADS
    std = jax.ShapeDtypeStruct((PEER_HEADS, PEER_KEYS, t), BF16)
    by_key = jax.ShapeDtypeStruct((PEER_KEYS, PEER_HEADS, t), F32)
    full = lambda shape: pl.BlockSpec(shape, lambda i: (0,) * len(shape))
    r2, e2, c, e1 = pl.pallas_call(
        functools.partial(_peer_route_kernel, by_head=True),
        grid=(1,),
        in_specs=[full(h.shape), full(wq_t.shape), full(kexp_h.shape)],
        out_specs=[full(std.shape), full(std.shape), full(by_key.shape), full(by_key.shape)],
        out_shape=[std, std, by_key, by_key],
        scratch_shapes=_route_scratch(wq_t.shape[0], t),
        compiler_params=_cp("arbitrary"),
        name="peer_route_heads",
    )(h, wq_t, kexp_h)
    to_groups = lambda x: jnp.transpose(x, (1, 0, 2)).reshape(PEER_HEADS, PEER_KEYS, 1, t)
    return r2, e2, to_groups(c), to_groups(e1)


def _peer_route(h, wq_t, kexp):
    t, d = h.shape
    nblk = t // ROUTE_TOKENS
    nq = wq_t.shape[0]
    std = jax.ShapeDtypeStruct((PEER_HEADS, PEER_KEYS, t), BF16)
    grp = jax.ShapeDtypeStruct((PEER_HEADS, PEER_KEYS, t // LANES, LANES), F32)
    std_spec = pl.BlockSpec((None, PEER_KEYS, ROUTE_TOKENS), lambda i, hd: (hd, 0, i))
    grp_spec = pl.BlockSpec((None, PEER_KEYS, GROUPS, LANES), lambda i, hd: (hd, 0, i, 0))
    return pl.pallas_call(
        functools.partial(_peer_route_kernel, by_head=False),
        grid=(nblk, PEER_HEADS),
        in_specs=[pl.BlockSpec((ROUTE_TOKENS, d), lambda i, hd: (i, 0)),
                  pl.BlockSpec(wq_t.shape, lambda i, hd: (0, 0)),
                  pl.BlockSpec((None,) + kexp.shape[1:], lambda i, hd: (hd, 0, 0, 0))],
        out_specs=[std_spec, std_spec, grp_spec, grp_spec],
        out_shape=[std, std, grp, grp],
        scratch_shapes=_route_scratch(nq, ROUTE_TOKENS),
        compiler_params=_cp("parallel", "arbitrary"),
        name="peer_route",
    )(h, wq_t, kexp)


def _peer_dense_kernel(h_ref, u_ref, vt_ref, r2_ref, e2_ref, c_ref, e1_ref, o_ref):
    e = pl.program_id(1)
    tb = h_ref.shape[0]
    sub = r2_ref.shape[2]

    @pl.when(e == 0)
    def _():
        o_ref[...] = jnp.zeros_like(o_ref)

    tw = min(tb, PEER_UNIT_TOKENS)
    per = PEER_SUB // PEER_KEYS
    ngrp = tw // LANES
    if c_ref.shape[2] * LANES == tb:
        goff = 0
    else:
        goff = (pl.program_id(0) % (ROUTE_TOKENS // tb)) * ngrp
    units = [(q, tp) for q in range(u_ref.shape[0] // PEER_SUB) for tp in range(tb // tw)]

    def token_rows(ref, h, jj, tp):
        row = jnp.concatenate(
            [ref[h, jj, pl.ds(goff + tp * ngrp + g, 1), :] for g in range(ngrp)], axis=1)
        return jnp.broadcast_to(row, (sub, tw)).astype(BF16)[None]

    def scores(unit):
        q, tp = unit
        return _dot_nt(u_ref[q * PEER_SUB:(q + 1) * PEER_SUB, :],
                       h_ref[tp * tw:(tp + 1) * tw, :])

    accs = [o_ref[:, tp * tw:(tp + 1) * tw] for tp in range(tb // tw)]
    at_next = scores(units[0])
    for n, (q, tp) in enumerate(units):
        at = at_next
        if n + 1 < len(units):
            at_next = scores(units[n + 1])
        cols = slice(tp * tw, (tp + 1) * tw)
        gs = []
        for jj in range(q * per, (q + 1) * per):
            w = jnp.zeros((PEER_KEYS // sub, sub, tw), BF16)
            for h in range(PEER_HEADS):
                picked = jnp.where(r2_ref[h, :, :, cols] < token_rows(c_ref, h, jj, tp),
                                   e2_ref[h, :, :, cols], jnp.zeros_like(w))
                w = w + picked * token_rows(e1_ref, h, jj, tp)
            lo = (jj - q * per) * PEER_KEYS
            a = _gelu(at[lo:lo + PEER_KEYS, :]).astype(BF16)
            gs.append(a * w.reshape(PEER_KEYS, tw))
        accs[tp] = accs[tp] + _dot(vt_ref[:, q * PEER_SUB:(q + 1) * PEER_SUB],
                                   jnp.concatenate(gs, axis=0))
    for tp, acc in enumerate(accs):
        o_ref[:, tp * tw:(tp + 1) * tw] = acc


def _peer_dense(h, u_all, v_all, layer, r2, e2, c, e1, *, tb, te):
    t, d = h.shape
    ne = u_all.shape[1]
    sub = 16
    tr = r2.shape[2]
    r2 = r2.reshape(PEER_HEADS, PEER_KEYS // sub, sub, tr)
    e2 = e2.reshape(PEER_HEADS, PEER_KEYS // sub, sub, tr)
    nchunk = te // PEER_KEYS
    per_route = ROUTE_TOKENS // tb
    full = pl.BlockSpec((PEER_HEADS, PEER_KEYS // sub, sub, tb), lambda i, e: (0, 0, 0, i))
    ngroups = min(GROUPS, c.shape[2])
    rows = pl.BlockSpec((PEER_HEADS, nchunk, ngroups, LANES),
                        lambda i, e: (0, e, i // per_route, 0))
    return pl.pallas_call(
        _peer_dense_kernel,
        grid=(t // tb, ne // te),
        in_specs=[pl.BlockSpec((tb, d), lambda i, e: (i, 0)),
                  pl.BlockSpec((None, te, d), lambda i, e: (layer, e, 0)),
                  pl.BlockSpec((None, d, te), lambda i, e: (layer, 0, e)),
                  full, full, rows, rows],
        out_specs=pl.BlockSpec((d, tb), lambda i, e: (0, i)),
        out_shape=jax.ShapeDtypeStruct((d, t), F32),
        compiler_params=_cp("parallel", "arbitrary"),
        name="peer_dense",
    )(h, u_all, v_all, r2, e2, c, e1)


def _peer_out_kernel(ot_ref, x_ref, gate_ref, xo_ref):
    xo_ref[...] = x_ref[...] + gate_ref[...] * ot_ref[...].T


def _peer_out(ot, x, gate, *, tm, per_row):
    t, d = x.shape
    row = lambda i: (i, 0)
    return pl.pallas_call(
        _peer_out_kernel,
        grid=(t // tm,),
        in_specs=[pl.BlockSpec((d, tm), lambda i: (0, i)), pl.BlockSpec((tm, d), row),
                  _mod_spec(per_row, tm, d, False)],
        out_specs=pl.BlockSpec((tm, d), row),
        out_shape=jax.ShapeDtypeStruct((t, d), F32),
        compiler_params=_cp("parallel"),
        name="peer_out",
    )(ot, x, gate)


def _peer(h, x, gate, wq_t, kexp, kexp_h, u_all, v_all, layer, *, tb, tm, per_row):
    t = h.shape[0]
    if t == LANES:
        r2, e2, c, e1 = _peer_route_heads(h, wq_t, kexp_h)
    else:
        h_route = jnp.pad(h, ((0, -t % ROUTE_TOKENS), (0, 0)))
        r2, e2, c, e1 = _peer_route(h_route, wq_t, kexp)
    ot = _peer_dense(h, u_all, v_all, layer, r2, e2, c, e1, tb=tb, te=PEER_EXPERT_BLOCK)
    return _peer_out(ot, x, gate, tm=tm, per_row=per_row)


def kernel(x_prompt, x_sample, c_prompt, c_sample, cache_pool, cache_k, cache_v, cache_logf, w_ada, b_ada, norm1, norm2, w_in_ab, b_f, w_pool, pool_scale, q_gain, k_gain, w_out_ab, w_uv, sg_ln_g, sg_ln_b, w_s, b_s, w_out_sg, peer_wq, peer_keys, peer_u, peer_v):
    nb, seq, d = x_prompt.shape
    db, dl, _ = x_sample.shape
    depth = w_ada.shape[0]
    past = cache_k.shape[2]
    pool_hist = cache_pool.shape[2]
    assert nb == 1 and db * dl == LANES and dl == 16 and pool_hist == dl - 1
    fox_w = FOX_HEADS * HEAD_DIM
    pool_w = w_pool.shape[1] * w_pool.shape[2]
    ts = db * dl

    c_rows = jnp.concatenate(
        [c_prompt, c_sample, jnp.zeros((16 - nb - db, d), F32)], axis=0)
    mods = _ada(c_rows, w_ada, b_ada)

    def mod_p(layer, k):
        return mods[layer, 0:1, k * d:(k + 1) * d]

    def mod_s(layer, k):
        return jnp.repeat(mods[layer, 1:1 + db, k * d:(k + 1) * d], dl, axis=0)

    xp = x_prompt.reshape(seq, d)
    xs = x_sample.reshape(ts, d)
    tm_p = 512 if seq % 512 == 0 else 256
    outs = {k: [] for k in ("pool_p", "pool_s", "kp", "ks", "vp", "vs", "lfp", "lfs", "sgv")}
    u_all = peer_u.astype(BF16)
    v_all = jnp.swapaxes(peer_v, 1, 2).astype(BF16)

    for layer in range(depth):
        j = layer // 2
        g1 = norm1[layer].reshape(1, d)
        g2 = norm2[layer].reshape(1, d)
        if layer % 2 == 0:
            w_in = w_in_ab[j]
            w_main = w_in[:, :pool_w + 3 * fox_w].astype(BF16)
            wf_t = w_in[:, pool_w + 3 * fox_w:].T.astype(BF16)
            bfc = b_f[j].reshape(FOX_HEADS, 1)
            qg = q_gain[j].reshape(1, fox_w)
            kg = k_gain[j].reshape(1, fox_w)
            wp = w_pool[j].astype(BF16)
            psc = pool_scale[j].reshape(1, pool_w)
            wo = w_out_ab[j].astype(BF16)

            p, qb, k, kb, v, vb, lf, fc = _proj(
                xp, g1, mod_p(layer, 0), mod_p(layer, 1), w_main, wf_t, bfc, qg, kg,
                tm=tm_p, per_row=False, do_cumsum=True)
            a_out = _pool_prompt(p, wp, psc, tm=256)
            score_bound = 1.02 * (HEAD_DIM ** 0.5) * (jnp.max(jnp.abs(q_gain[j]), axis=-1)
                                                      * jnp.max(jnp.abs(k_gain[j]), axis=-1))
            b_out = _attn_prompt(qb, kb, vb, fc, score_bound, tq=tm_p)
            xp, hp = _merge(a_out, b_out, wo, xp, mod_p(layer, 2), g2, mod_p(layer, 3),
                            mod_p(layer, 4), tm=256, per_row=False)
            outs["pool_p"].append(p[-pool_hist:].reshape(nb, pool_hist, pool_w))
            outs["kp"].append(k.reshape(nb, seq, FOX_HEADS, HEAD_DIM))
            outs["vp"].append(v.reshape(nb, seq, FOX_HEADS, HEAD_DIM))
            outs["lfp"].append(lf.T.reshape(nb, seq, FOX_HEADS))

            p, qb, k, kb, v, vb, lf, _ = _proj(
                xs, g1, mod_s(layer, 0), mod_s(layer, 1), w_main, wf_t, bfc, qg, kg,
                tm=ts, per_row=True, do_cumsum=False)
            p3 = p.reshape(db, dl, pool_w)
            buf = jnp.concatenate([jnp.zeros((db, 1, pool_w), F32), cache_pool[j], p3], axis=1)
            a_out = _pool_sample(buf.reshape(db * 2 * dl, pool_w), p, wp, psc, pos0=past)
            b_out = _attn_sample(qb, kb, vb, lf, cache_k[j], cache_v[j], cache_logf[j],
                                 score_bound, tk=min(past, SAMPLE_KEY_CHUNK))
            xs, hs = _merge(a_out, b_out, wo, xs, mod_s(layer, 2), g2, mod_s(layer, 3),
                            mod_s(layer, 4), tm=ts, per_row=True)
            outs["pool_s"].append(jnp.concatenate([cache_pool[j], p3], axis=1)[:, -pool_hist:])
            outs["ks"].append(k.reshape(db, dl, FOX_HEADS, HEAD_DIM))
            outs["vs"].append(v.reshape(db, dl, FOX_HEADS, HEAD_DIM))
            outs["lfs"].append(lf.T.reshape(db, dl, FOX_HEADS))
        else:
            wuv = w_uv[j].astype(BF16)
            lng = sg_ln_g[j].reshape(1, -1)
            lnb = sg_ln_b[j].reshape(1, -1)
            wo = w_out_sg[j].astype(BF16)
            tri = jnp.tril(jnp.ones((SG_CHUNK, SG_CHUNK), bool))
            ws_p = jnp.where(tri, w_s[j], 0.0).astype(BF16)
            bs_p = b_s[j].T
            ws16 = jnp.where(tri[:dl, :dl], w_s[j][:, :dl, :dl], 0.0)
            ws_s = jnp.einsum("ab,gts->gatbs", jnp.eye(db, dtype=F32), ws16)
            ws_s = ws_s.reshape(SG_GROUPS, ts, ts).astype(BF16)
            bs_s = jnp.tile(b_s[j][:, :dl], (1, db)).T

            u, vn = _uv(xp, g1, mod_p(layer, 0), mod_p(layer, 1), wuv, lng, lnb,
                        tm=tm_p, per_row=False, vn_dtype=BF16)
            xp, hp = _sgate(u, vn, ws_p, bs_p, wo, xp, mod_p(layer, 2), g2, mod_p(layer, 3),
                            mod_p(layer, 4), tm=256, per_row=False)
            u, vn = _uv(xs, g1, mod_s(layer, 0), mod_s(layer, 1), wuv, lng, lnb,
                        tm=ts, per_row=True, vn_dtype=F32)
            xs, hs = _sgate(u, vn, ws_s, bs_s, wo, xs, mod_s(layer, 2), g2, mod_s(layer, 3),
                            mod_s(layer, 4), tm=ts, per_row=True)
            outs["sgv"].append(vn.reshape(db, dl, -1))

        wq_t = peer_wq[layer].T.astype(BF16)
        kexp = jnp.einsum("hskd,gj->hskgjd", peer_keys[layer].astype(BF16),
                          jnp.eye(GROUPS, dtype=BF16))
        kexp = kexp.reshape(PEER_HEADS, 2, PEER_KEYS * GROUPS, GROUPS * kexp.shape[-1])
        kexp_h = jnp.einsum("hskd,hj->skhjd", peer_keys[layer].astype(BF16),
                            jnp.eye(PEER_HEADS, dtype=BF16))
        kexp_h = kexp_h.reshape(2, PEER_KEYS * PEER_HEADS, PEER_HEADS * kexp_h.shape[-1])
        xp = _peer(hp, xp, mod_p(layer, 5), wq_t, kexp, kexp_h, u_all, v_all, layer,
                   tb=PEER_TOKEN_BLOCK if seq % PEER_TOKEN_BLOCK == 0 else 256, tm=256,
                   per_row=False)
        xs = _peer(hs, xs, mod_s(layer, 5), wq_t, kexp, kexp_h, u_all, v_all, layer,
                   tb=ts, tm=ts, per_row=True)

    st = lambda key: jnp.stack(outs[key])
    return (xp.reshape(nb, seq, d), xs.reshape(db, dl, d), st("pool_p"), st("pool_s"),
            st("kp"), st("ks"), st("vp"), st("vs"), st("lfp"), st("lfs"), st("sgv"))
```

```python
import functools

import jax
import jax.numpy as jnp
from jax import lax
from jax.experimental import pallas as pl
from jax.experimental.pallas import tpu as pltpu

F32 = jnp.float32
BF16 = jnp.bfloat16
EPS = 1e-6

LANES = 128
POOL_WINDOWS = (2, 4, 8, 16)
POOL_HALO = 128
FOX_HEADS = 8
HEAD_DIM = 128
ATTN_HEADS_PER_STEP = 4
SAMPLE_KEY_CHUNK = 512
SG_GROUPS = 8
SG_CHUNK = 128
PEER_HEADS = 8
PEER_KEYS = 128
PEER_TOPK = 16
PEER_TOKEN_BLOCK = 512
PEER_EXPERT_BLOCK = 1024
PEER_SUB = 256
PEER_UNIT_TOKENS = 1024
NEG_BIG = -1e30
EXP_ZERO_BELOW = 110.0
RANK_NONE = 64.0
VMEM_LIMIT = 56 * 1024 * 1024

NT_DIMS = (((1,), (1,)), ((), ()))


def _cp(*sem):
    return pltpu.CompilerParams(dimension_semantics=sem, vmem_limit_bytes=VMEM_LIMIT)


def _dot(a, b):
    return jnp.dot(a, b, preferred_element_type=F32)


def _dot_nt(a, b):
    return lax.dot_general(a, b, NT_DIMS, preferred_element_type=F32)


def _modulate(x, g, shift, scale):
    ms = jnp.mean(x * x, axis=-1, keepdims=True)
    return x * lax.rsqrt(ms + EPS) * g * (1.0 + scale) + shift


def _gelu(x):
    c = 0.7978845608028654
    half = 0.5 * x
    return half + half * jnp.tanh(x * (c + (c * 0.044715) * (x * x)))


def _log_sigmoid(x):
    return -(jnp.maximum(-x, 0.0) + jnp.log(1.0 + jnp.exp(-jnp.abs(x))))


def _lane_cumsum(x):
    n = x.shape[-1]
    lane = lax.broadcasted_iota(jnp.int32, x.shape, x.ndim - 1)
    s = 1
    while s < n:
        x = x + jnp.where(lane >= s, pltpu.roll(x, s, x.ndim - 1), 0.0)
        s *= 2
    return x


def _ada_kernel(c_ref, w_ref, b_ref, o_ref):
    c = c_ref[...]
    a = (c * (1.0 / (1.0 + jnp.exp(-c)))).astype(BF16)
    o_ref[0] = _dot(a, w_ref[0].astype(BF16)) + b_ref[0]


def _ada(c_rows, w_ada, b_ada):
    nl, d, n = w_ada.shape
    r = c_rows.shape[0]
    tn = 1024
    return pl.pallas_call(
        _ada_kernel,
        grid=(nl, n // tn),
        in_specs=[pl.BlockSpec((r, d), lambda l, j: (0, 0)),
                  pl.BlockSpec((1, d, tn), lambda l, j: (l, 0, j)),
                  pl.BlockSpec((1, 1, tn), lambda l, j: (l, 0, j))],
        out_specs=pl.BlockSpec((1, r, tn), lambda l, j: (l, 0, j)),
        out_shape=jax.ShapeDtypeStruct((nl, r, n), F32),
        compiler_params=_cp("parallel", "parallel"),
        name="ada",
    )(c_rows, w_ada, b_ada.reshape(nl, 1, n))


def _mod_spec(per_row, tm, d, nj):
    if nj:
        return (pl.BlockSpec((tm, d), lambda i, j: (i, 0)) if per_row
                else pl.BlockSpec((1, d), lambda i, j: (0, 0)))
    return (pl.BlockSpec((tm, d), lambda i: (i, 0)) if per_row
            else pl.BlockSpec((1, d), lambda i: (0, 0)))


def _head_rms(z, gain):
    outs = []
    for h in range(FOX_HEADS):
        zh = z[:, h * HEAD_DIM:(h + 1) * HEAD_DIM]
        ms = jnp.mean(zh * zh, axis=-1, keepdims=True)
        outs.append(zh * lax.rsqrt(ms + EPS))
    return jnp.concatenate(outs, axis=-1) * gain


def _proj_kernel(x_ref, g_ref, sh_ref, sc_ref, w_ref, wf_ref, bf_ref, qg_ref, kg_ref,
                 p_ref, q_ref, k_ref, kb_ref, v_ref, vb_ref, lf_ref, fc_ref,
                 h_scr, carry_scr, *, do_cumsum):
    i = pl.program_id(0)
    j = pl.program_id(1)

    @pl.when(j == 0)
    def _():
        hb = _modulate(x_ref[...], g_ref[...], sh_ref[...], sc_ref[...]).astype(BF16)
        h_scr[...] = hb
        lf = _log_sigmoid(_dot_nt(wf_ref[...], hb) + bf_ref[...])
        lf_ref[...] = lf
        if do_cumsum:
            @pl.when(i == 0)
            def _():
                carry_scr[...] = jnp.zeros_like(carry_scr)
            fc = _lane_cumsum(lf) + carry_scr[:, 0:1]
            fc_ref[...] = fc
            carry_scr[...] = jnp.broadcast_to(fc[:, -1:], carry_scr.shape)
        else:
            fc_ref[...] = lf

    z = _dot(h_scr[...], w_ref[...])

    @pl.when(j == 0)
    def _():
        p_ref[...] = z

    @pl.when(j == 1)
    def _():
        q_ref[...] = (_head_rms(z, qg_ref[...]) * (HEAD_DIM ** -0.5)).astype(BF16)

    @pl.when(j == 2)
    def _():
        kn = _head_rms(z, kg_ref[...])
        k_ref[...] = kn
        kb_ref[...] = kn.astype(BF16)

    @pl.when(j == 3)
    def _():
        v_ref[...] = z
        vb_ref[...] = z.astype(BF16)


def _proj(x, g, shift, scale, w_main, wf_t, b_f, q_gain, k_gain, *, tm, per_row, do_cumsum):
    t, d = x.shape
    w4 = w_main.shape[1] // 4
    nh = wf_t.shape[0]
    row = lambda i, j: (i, 0)
    const = lambda i, j: (0, 0)
    tok_f32 = jax.ShapeDtypeStruct((t, w4), F32)
    tok_b16 = jax.ShapeDtypeStruct((t, w4), BF16)
    head_t = jax.ShapeDtypeStruct((nh, t), F32)
    return pl.pallas_call(
        functools.partial(_proj_kernel, do_cumsum=do_cumsum),
        grid=(t // tm, 4),
        in_specs=[pl.BlockSpec((tm, d), row),
                  pl.BlockSpec((1, d), const),
                  _mod_spec(per_row, tm, d, True),
                  _mod_spec(per_row, tm, d, True),
                  pl.BlockSpec((d, w4), lambda i, j: (0, j)),
                  pl.BlockSpec((nh, d), const),
                  pl.BlockSpec((nh, 1), const),
                  pl.BlockSpec((1, w4), const),
                  pl.BlockSpec((1, w4), const)],
        out_specs=[pl.BlockSpec((tm, w4), row)] * 6
                  + [pl.BlockSpec((nh, tm), lambda i, j: (0, i))] * 2,
        out_shape=[tok_f32, tok_b16, tok_f32, tok_b16, tok_f32, tok_b16, head_t, head_t],
        scratch_shapes=[pltpu.VMEM((tm, d), BF16), pltpu.VMEM((nh, LANES), F32)],
        compiler_params=_cp("arbitrary", "arbitrary"),
        name="proj",
    )(x, g, shift, scale, w_main, wf_t, b_f, q_gain, k_gain)


def _pool_kernel(ext_ref, tok_ref, w_ref, ps_ref, o_ref, *, prompt, pos0):
    i = pl.program_id(0)
    tok = tok_ref[...]
    tm = tok.shape[0]
    if prompt:
        halo = jnp.where(i == 0, 0.0, ext_ref[...])
        ext = jnp.concatenate([halo, tok], axis=0)
    else:
        ext = ext_ref[...]
    ke = ext.shape[0]
    hi = ext.astype(BF16)
    lo = (ext - hi.astype(F32)).astype(BF16)
    m_io = lax.broadcasted_iota(jnp.int32, (tm, ke), 0)
    c_io = lax.broadcasted_iota(jnp.int32, (tm, ke), 1)
    r_io = lax.broadcasted_iota(jnp.int32, (tm, 1), 0)
    if prompt:
        tgt = m_io + POOL_HALO
        pos = pos0 + i * tm + r_io
    else:
        tgt = m_io + ((m_io >> 4) << 4) + 16
        pos = pos0 + (r_io & 15)
    dlt = tgt - c_io
    gw = w_ref.shape[1]
    for gi, win in enumerate(POOL_WINDOWS):
        sl = slice(gi * gw, (gi + 1) * gw)
        band = jnp.where(dlt >= 0, jnp.where(dlt < win, 1.0, 0.0), 0.0).astype(BF16)
        wsum = _dot(band, hi[:, sl]) + _dot(band, lo[:, sl])
        cnt = jnp.minimum(win, pos + 1).astype(F32)
        dd = wsum / cnt - tok[:, sl]
        o = _dot(dd.astype(BF16), w_ref[gi]) * ps_ref[:, sl]
        o_ref[:, sl] = o.astype(o_ref.dtype)


def _pool_prompt(p, w_pool, pool_scale, *, tm):
    t, c = p.shape
    per = tm // POOL_HALO
    return pl.pallas_call(
        functools.partial(_pool_kernel, prompt=True, pos0=0),
        grid=(t // tm,),
        in_specs=[pl.BlockSpec((POOL_HALO, c), lambda i: (jnp.maximum(i * per - 1, 0), 0)),
                  pl.BlockSpec((tm, c), lambda i: (i, 0)),
                  pl.BlockSpec(w_pool.shape, lambda i: (0, 0, 0)),
                  pl.BlockSpec((1, c), lambda i: (0, 0))],
        out_specs=pl.BlockSpec((tm, c), lambda i: (i, 0)),
        out_shape=jax.ShapeDtypeStruct((t, c), BF16),
        compiler_params=_cp("parallel"),
        name="pool_prompt",
    )(p, p, w_pool, pool_scale)


def _pool_sample(buf, p, w_pool, pool_scale, *, pos0):
    t, c = p.shape
    return pl.pallas_call(
        functools.partial(_pool_kernel, prompt=False, pos0=pos0),
        grid=(1,),
        in_specs=[pl.BlockSpec(buf.shape, lambda i: (0, 0)),
                  pl.BlockSpec((t, c), lambda i: (0, 0)),
                  pl.BlockSpec(w_pool.shape, lambda i: (0, 0, 0)),
                  pl.BlockSpec((1, c), lambda i: (0, 0))],
        out_specs=pl.BlockSpec((t, c), lambda i: (0, 0)),
        out_shape=jax.ShapeDtypeStruct((t, c), BF16),
        compiler_params=_cp("arbitrary"),
        name="pool_sample",
    )(buf, p, w_pool, pool_scale)


def _softmax_step(carry, s, v):
    m, l, acc = carry
    m_new = jnp.maximum(m, jnp.max(s, axis=-1, keepdims=True))
    alpha = jnp.exp(m - m_new)
    p = jnp.exp(s - m_new)
    l = alpha * l + jnp.sum(p, axis=-1, keepdims=True)
    acc = alpha * acc + _dot(p.astype(BF16), v)
    return m_new, l, acc


def _attn_prompt_kernel(fend_ref, fstart_ref, bound_ref, q_ref, k_ref, v_ref, fk_ref, o_ref,
                        *, tq, nhead):
    hg = pl.program_id(0)
    i = pl.program_id(1)
    lanes = [slice(a * HEAD_DIM, (a + 1) * HEAD_DIM) for a in range(nhead)]
    qs = [q_ref[:, ln] for ln in lanes]

    def chunk(kj, carries, masked):
        off = pl.multiple_of(kj * tq, tq)
        out = []
        for a, ln in enumerate(lanes):
            s = _dot_nt(qs[a], k_ref[pl.ds(off, tq), ln]) - fk_ref[a, :, pl.ds(off, tq)]
            if masked:
                r_io = lax.broadcasted_iota(jnp.int32, (tq, tq), 0)
                c_io = lax.broadcasted_iota(jnp.int32, (tq, tq), 1)
                s = jnp.where(c_io <= r_io, s, NEG_BIG)
            out.append(_softmax_step(carries[a], s, v_ref[pl.ds(off, tq), ln]))
        return tuple(out)

    dead = i
    for a in range(nhead):
        h = hg * nhead + a
        limit = fstart_ref[h, i] + bound_ref[h]
        dead = jnp.minimum(dead, lax.fori_loop(
            0, i, lambda kj, n: n + jnp.where(fend_ref[h, kj] > limit, 1, 0), jnp.int32(0)))
    init = tuple((jnp.full((tq, 1), NEG_BIG, F32), jnp.zeros((tq, 1), F32),
                  jnp.zeros((tq, HEAD_DIM), F32)) for _ in lanes)
    carries = lax.fori_loop(dead, i, lambda kj, c: chunk(kj, c, False), init)
    for ln, (_, l, acc) in zip(lanes, chunk(i, carries, True)):
        o_ref[:, ln] = (acc / l).astype(o_ref.dtype)


def _attn_prompt(qb, kb, vb, fcum, score_bound, *, tq):
    t, w = qb.shape
    nh = w // HEAD_DIM
    fend = fcum[:, tq - 1::tq]
    fstart = fcum[:, ::tq]
    bound = 2.0 * score_bound + EXP_ZERO_BELOW
    smem = pl.BlockSpec(memory_space=pltpu.SMEM)
    nhead = ATTN_HEADS_PER_STEP
    wide = nhead * HEAD_DIM
    return pl.pallas_call(
        functools.partial(_attn_prompt_kernel, tq=tq, nhead=nhead),
        grid=(nh // nhead, t // tq),
        in_specs=[smem, smem, smem,
                  pl.BlockSpec((tq, wide), lambda h, i: (i, h)),
                  pl.BlockSpec((t, wide), lambda h, i: (0, h)),
                  pl.BlockSpec((t, wide), lambda h, i: (0, h)),
                  pl.BlockSpec((nhead, 1, t), lambda h, i: (h, 0, 0))],
        out_specs=pl.BlockSpec((tq, wide), lambda h, i: (i, h)),
        out_shape=jax.ShapeDtypeStruct((t, w), BF16),
        compiler_params=_cp("parallel", "parallel"),
        name="attn_prompt",
    )(fend, fstart, bound, qb, kb, vb, fcum.reshape(nh, 1, t))


def _strided_lane_cumsum(x, stride, seg=None):
    n = seg or x.shape[-1]
    lane = lax.broadcasted_iota(jnp.int32, x.shape, x.ndim - 1)
    pos = lane if seg is None else lane & (seg - 1)
    s = stride
    while s < n:
        x = x + jnp.where(pos >= s, pltpu.roll(x, s, x.ndim - 1), 0.0)
        s *= 2
    return x


def _tile_last_lanes(x, period):
    n = x.shape[-1]
    lane = lax.broadcasted_iota(jnp.int32, x.shape, x.ndim - 1)
    y = jnp.where(lane >= n - period, x, 0.0)
    s = period
    while s < n:
        y = y + pltpu.roll(y, n - s, x.ndim - 1)
        s *= 2
    return y


def _chunk_suffix_kernel(lf_ref, o_ref, *, tk):
    x = lf_ref[...]
    p = x.shape[1]
    pos = lax.broadcasted_iota(jnp.int32, (p, LANES), 0)
    col = lax.broadcasted_iota(jnp.int32, (p, LANES), 1)
    later = jnp.where(pos >= (col + 1) * tk, 1.0, 0.0).astype(BF16)
    hi = x.astype(BF16)
    r1 = x - hi.astype(F32)
    mid = r1.astype(BF16)
    lo = (r1 - mid.astype(F32)).astype(BF16)
    o_ref[...] = _dot(hi, later) + _dot(mid, later) + _dot(lo, later)


def _chunk_suffix(lf_rows, *, tk):
    r, p = lf_rows.shape
    return pl.pallas_call(
        functools.partial(_chunk_suffix_kernel, tk=tk),
        grid=(1,),
        in_specs=[pl.BlockSpec((r, p), lambda i: (0, 0))],
        out_specs=pl.BlockSpec((r, LANES), lambda i: (0, 0)),
        out_shape=jax.ShapeDtypeStruct((r, LANES), F32),
        compiler_params=_cp("arbitrary"),
        name="cache_decay",
    )(lf_rows)


def _attn_sample_kernel(need_ref, q_ref, ck_ref, cv_ref, clf_ref, kn_ref, vn_ref, lfn_ref, o_ref,
                        q_scr, m_scr, l_scr, acc_scr, car_scr):
    b = pl.program_id(0)
    c = pl.program_id(1)
    nl = q_ref.shape[0]
    rows = q_scr.shape[0]

    def stack_heads(x):
        return jnp.concatenate(
            [x[:, h * HEAD_DIM:(h + 1) * HEAD_DIM] for h in range(FOX_HEADS)], axis=0)

    @pl.when(c == 0)
    def _():
        qs = stack_heads(q_ref[...])
        q_scr[...] = qs
        kn = stack_heads(kn_ref[...])
        r_io = lax.broadcasted_iota(jnp.int32, (rows, rows), 0)
        c_io = lax.broadcasted_iota(jnp.int32, (rows, rows), 1)
        ncum = _strided_lane_cumsum(lfn_ref[...], 1, seg=nl)
        sn = jnp.where((r_io // nl) == (c_io // nl),
                       jnp.where((c_io % nl) <= (r_io % nl), _dot_nt(qs, kn) - ncum, NEG_BIG),
                       NEG_BIG)
        init = (jnp.full((rows, 1), NEG_BIG, F32), jnp.zeros((rows, 1), F32),
                jnp.zeros((rows, HEAD_DIM), F32))
        m_scr[...], l_scr[...], acc_scr[...] = _softmax_step(init, sn, stack_heads(vn_ref[...]))
        car_scr[...] = jnp.zeros_like(car_scr)

    @pl.when(c < need_ref[b])
    def _():
        n = clf_ref.shape[1]
        incl = _strided_lane_cumsum(clf_ref[...], FOX_HEADS)
        total = _tile_last_lanes(incl, FOX_HEADS)
        after = car_scr[...] + total - incl
        car_scr[...] = car_scr[...] + total
        r_io = lax.broadcasted_iota(jnp.int32, (rows, n), 0)
        c_io = lax.broadcasted_iota(jnp.int32, (rows, n), 1)
        s = _dot_nt(q_scr[...], ck_ref[...].astype(BF16)) + after
        s = jnp.where((c_io % FOX_HEADS) == (r_io // nl), s, NEG_BIG)
        carry = _softmax_step((m_scr[...], l_scr[...], acc_scr[...]), s, cv_ref[...].astype(BF16))
        m_scr[...], l_scr[...], acc_scr[...] = carry

    @pl.when(c == pl.num_programs(1) - 1)
    def _():
        o = acc_scr[...] / l_scr[...]
        o_ref[...] = jnp.concatenate(
            [o[h * nl:(h + 1) * nl, :] for h in range(FOX_HEADS)], axis=-1).astype(o_ref.dtype)


def _attn_sample(qb, kb, vb, lf_new, cache_k, cache_v, cache_lf, score_bound, *, tk):
    b, past, nh, dh = cache_k.shape
    w = nh * dh
    nl = qb.shape[0] // b
    rows = nh * nl
    nch = past // tk
    after = _chunk_suffix(jnp.transpose(cache_lf, (0, 2, 1)).reshape(b * nh, past), tk=tk)
    alive = (after.reshape(b, nh, LANES)[:, :, :nch]
             + 2.0 * score_bound[None, :, None]) >= -EXP_ZERO_BELOW
    need = jnp.maximum(jnp.max(jnp.sum(alive, axis=2), axis=1), 1).astype(jnp.int32)
    lfn = jnp.transpose(lf_new.reshape(nh, b, nl), (1, 0, 2)).reshape(b, 1, rows)

    def newest_first(c, need_s):
        return nch - 1 - jnp.minimum(c, need_s - 1)

    chunk = lambda s, c, need: (s, newest_first(c, need[s]), 0)
    new = lambda s, c, need: (s, 0)
    return pl.pallas_call(
        _attn_sample_kernel,
        grid_spec=pltpu.PrefetchScalarGridSpec(
            num_scalar_prefetch=1,
            grid=(b, nch),
            in_specs=[pl.BlockSpec((nl, w), new),
                      pl.BlockSpec((None, tk * nh, dh), chunk),
                      pl.BlockSpec((None, tk * nh, dh), chunk),
                      pl.BlockSpec((None, 1, tk * nh),
                                   lambda s, c, need: (s, 0, newest_first(c, need[s]))),
                      pl.BlockSpec((nl, w), new),
                      pl.BlockSpec((nl, w), new),
                      pl.BlockSpec((None, 1, rows), lambda s, c, need: (s, 0, 0))],
            out_specs=pl.BlockSpec((nl, w), new),
            scratch_shapes=[pltpu.VMEM((rows, dh), BF16), pltpu.VMEM((rows, 1), F32),
                            pltpu.VMEM((rows, 1), F32), pltpu.VMEM((rows, dh), F32),
                            pltpu.VMEM((1, tk * nh), F32)]),
        out_shape=jax.ShapeDtypeStruct(qb.shape, BF16),
        compiler_params=_cp("parallel", "arbitrary"),
        name="attn_sample",
    )(need, qb, cache_k.reshape(b, past * nh, dh), cache_v.reshape(b, past * nh, dh),
      cache_lf.reshape(b, 1, past * nh), kb, vb, lfn)


def _residual_mod(x, gate, o, g2, sh2, sc2, x_ref, h_ref):
    xn = x + gate * o
    x_ref[...] = xn
    h_ref[...] = _modulate(xn, g2, sh2, sc2).astype(h_ref.dtype)


def _merge_kernel(a_ref, b_ref, w_ref, x_ref, gate_ref, g2_ref, sh2_ref, sc2_ref, xo_ref, h_ref):
    ca = a_ref.shape[1]
    o = _dot(a_ref[...], w_ref[:ca, :]) + _dot(b_ref[...], w_ref[ca:, :])
    _residual_mod(x_ref[...], gate_ref[...], o, g2_ref[...], sh2_ref[...], sc2_ref[...],
                  xo_ref, h_ref)


def _merge(a, b, w_out, x, gate, g2, sh2, sc2, *, tm, per_row):
    t, d = x.shape
    ca = a.shape[1]
    row = lambda i: (i, 0)
    mod = _mod_spec(per_row, tm, d, False)
    return pl.pallas_call(
        _merge_kernel,
        grid=(t // tm,),
        in_specs=[pl.BlockSpec((tm, ca), row), pl.BlockSpec((tm, ca), row),
                  pl.BlockSpec(w_out.shape, lambda i: (0, 0)),
                  pl.BlockSpec((tm, d), row), mod,
                  pl.BlockSpec((1, d), lambda i: (0, 0)), mod, mod],
        out_specs=[pl.BlockSpec((tm, d), row), pl.BlockSpec((tm, d), row)],
        out_shape=[jax.ShapeDtypeStruct((t, d), F32), jax.ShapeDtypeStruct((t, d), BF16)],
        compiler_params=_cp("parallel"),
        name="merge",
    )(a, b, w_out, x, gate, g2, sh2, sc2)


def _uv_kernel(x_ref, g_ref, sh_ref, sc_ref, w_ref, lng_ref, lnb_ref, u_ref, vn_ref, h_scr):
    j = pl.program_id(1)

    @pl.when(j == 0)
    def _():
        h_scr[...] = _modulate(x_ref[...], g_ref[...], sh_ref[...], sc_ref[...]).astype(BF16)

    z = _gelu(_dot(h_scr[...], w_ref[...]))

    @pl.when(j == 0)
    def _():
        u_ref[...] = z.astype(u_ref.dtype)

    @pl.when(j == 1)
    def _():
        mu = jnp.mean(z, axis=-1, keepdims=True)
        zc = z - mu
        var = jnp.mean(zc * zc, axis=-1, keepdims=True)
        vn_ref[...] = (zc * lax.rsqrt(var + EPS) * lng_ref[...] + lnb_ref[...]).astype(vn_ref.dtype)


def _uv(x, g, shift, scale, w_uv, ln_g, ln_b, *, tm, per_row, vn_dtype):
    t, d = x.shape
    sw = w_uv.shape[1] // 2
    row = lambda i, j: (i, 0)
    const = lambda i, j: (0, 0)
    return pl.pallas_call(
        _uv_kernel,
        grid=(t // tm, 2),
        in_specs=[pl.BlockSpec((tm, d), row), pl.BlockSpec((1, d), const),
                  _mod_spec(per_row, tm, d, True), _mod_spec(per_row, tm, d, True),
                  pl.BlockSpec((d, sw), lambda i, j: (0, j)),
                  pl.BlockSpec((1, sw), const), pl.BlockSpec((1, sw), const)],
        out_specs=[pl.BlockSpec((tm, sw), row), pl.BlockSpec((tm, sw), row)],
        out_shape=[jax.ShapeDtypeStruct((t, sw), BF16), jax.ShapeDtypeStruct((t, sw), vn_dtype)],
        scratch_shapes=[pltpu.VMEM((tm, d), BF16)],
        compiler_params=_cp("parallel", "arbitrary"),
        name="sg_uv",
    )(x, g, shift, scale, w_uv, ln_g, ln_b)


def _sgate_kernel(u_ref, vn_ref, ws_ref, bs_ref, w_ref, x_ref, gate_ref, g2_ref, sh2_ref, sc2_ref,
                  xo_ref, h_ref, y_scr):
    tm, sw = u_ref.shape
    gw = sw // SG_GROUPS
    for c in range(tm // SG_CHUNK):
        rs = slice(c * SG_CHUNK, (c + 1) * SG_CHUNK)
        for g in range(SG_GROUPS):
            cs = slice(g * gw, (g + 1) * gw)
            sv = _dot(ws_ref[g], vn_ref[rs, cs].astype(BF16)) + bs_ref[:, g:g + 1]
            y_scr[rs, cs] = (u_ref[rs, cs].astype(F32) * sv).astype(BF16)
    o = _dot(y_scr[...], w_ref[...])
    _residual_mod(x_ref[...], gate_ref[...], o, g2_ref[...], sh2_ref[...], sc2_ref[...],
                  xo_ref, h_ref)


def _sgate(u, vn, ws, bs_t, w_out, x, gate, g2, sh2, sc2, *, tm, per_row):
    t, d = x.shape
    sw = u.shape[1]
    row = lambda i: (i, 0)
    mod = _mod_spec(per_row, tm, d, False)
    return pl.pallas_call(
        _sgate_kernel,
        grid=(t // tm,),
        in_specs=[pl.BlockSpec((tm, sw), row), pl.BlockSpec((tm, sw), row),
                  pl.BlockSpec(ws.shape, lambda i: (0, 0, 0)),
                  pl.BlockSpec(bs_t.shape, lambda i: (0, 0)),
                  pl.BlockSpec(w_out.shape, lambda i: (0, 0)),
                  pl.BlockSpec((tm, d), row), mod,
                  pl.BlockSpec((1, d), lambda i: (0, 0)), mod, mod],
        out_specs=[pl.BlockSpec((tm, d), row), pl.BlockSpec((tm, d), row)],
        out_shape=[jax.ShapeDtypeStruct((t, d), F32), jax.ShapeDtypeStruct((t, d), BF16)],
        scratch_shapes=[pltpu.VMEM((tm, sw), BF16)],
        compiler_params=_cp("parallel"),
        name="sg_gate",
    )(u, vn, ws, bs_t, w_out, x, gate, g2, sh2, sc2)


GROUPS = 8
ROUTE_TOKENS = GROUPS * LANES
PAIR_R1 = [0] * 16 + [a for a in range(1, 8) for _ in range(8)] + list(range(8, 16))
PAIR_R2 = list(range(16)) + [i for _ in range(1, 8) for i in range(8)] + [0] * 8
PAIR_FLAT = [a * PEER_TOPK + b for a, b in zip(PAIR_R1, PAIR_R2)]


def _extract_topk(problems, vals_refs, idx_refs):
    chains = 4

    def one_round(r, prev):
        picked = []
        for p, (ref, ids) in enumerate(problems):
            n = len(ids)
            per = -(-n // chains)
            best = []
            for c0 in range(0, n, per):
                m = ix = None
                for e in range(c0, min(c0 + per, n)):
                    x = jnp.where(prev[p] == float(ids[e]), -jnp.inf, ref[e])
                    ref[e] = x
                    if m is None:
                        m, ix = x, jnp.full(x.shape, float(ids[e]), F32)
                    else:
                        ix = jnp.where(x > m, float(ids[e]), ix)
                        m = jnp.maximum(m, x)
                best.append((m, ix))
            m, ix = best[0]
            for mc, ic in best[1:]:
                ix = jnp.where(mc > m, ic, ix)
                m = jnp.maximum(m, mc)
            vals_refs[p][r] = m
            idx_refs[p][r] = ix
            picked.append(ix)
        return tuple(picked)

    none = jnp.full((GROUPS, LANES), -1.0, F32)
    return lax.fori_loop(0, PEER_TOPK, one_round, tuple(none for _ in problems))


def _peer_route_kernel(h_ref, wq_ref, kexp_ref, r2_ref, e2_ref, c_ref, e1_ref,
                       qt_scr, s_scr, so_scr, vals_scr, idx_scr, cand_scr, cval_scr, cidx_scr,
                       t2_scr, *, by_head):
    if by_head:
        qt_scr[...] = _dot_nt(wq_ref[...], h_ref[...]).astype(BF16)

        def stacked_queries(side):
            return jnp.concatenate(
                [qt_scr[(2 * h + side) * PEER_KEYS:(2 * h + side + 1) * PEER_KEYS, :]
                 for h in range(PEER_HEADS)], axis=0)
    else:
        hd = pl.program_id(1)

        @pl.when(hd == 0)
        def _():
            qt_scr[...] = _dot_nt(wq_ref[...], h_ref[...]).astype(BF16)

        row = pl.multiple_of(hd * 2 * PEER_KEYS, 2 * PEER_KEYS)

        def stacked_queries(side):
            return jnp.concatenate(
                [qt_scr[pl.ds(row + side * PEER_KEYS, PEER_KEYS), g * LANES:(g + 1) * LANES]
                 for g in range(GROUPS)], axis=0)

    for side in range(2):
        s = _dot(kexp_ref[side], stacked_queries(side)).reshape(PEER_KEYS, GROUPS, LANES)
        s_scr[side] = s
        so_scr[side] = s

    keys = list(range(PEER_KEYS))
    _extract_topk([(s_scr.at[0], keys), (s_scr.at[1], keys)],
                  [vals_scr.at[0], vals_scr.at[1]], [idx_scr.at[0], idx_scr.at[1]])
    v1 = [vals_scr[0, r] for r in range(PEER_TOPK)]
    v2 = [vals_scr[1, r] for r in range(PEER_TOPK)]
    ncand = len(PAIR_FLAT)
    for c in range(ncand):
        cand_scr[c] = v1[PAIR_R1[c]] + v2[PAIR_R2[c]]
    (last,) = _extract_topk([(cand_scr, PAIR_FLAT)], [cval_scr], [cidx_scr])

    top = v1[0] + v2[0]
    z = jnp.zeros((GROUPS, LANES), F32)
    for r in range(PEER_TOPK):
        z = z + jnp.exp(cval_scr[r] - top)
    inv_z = 1.0 / z
    cnt = [jnp.zeros((GROUPS, LANES), F32) for _ in range(PEER_TOPK)]
    for c in range(ncand):
        picked = jnp.where(cand_scr[c] == -jnp.inf, 1.0,
                           jnp.where(last == float(PAIR_FLAT[c]), 1.0, 0.0))
        cnt[PAIR_R1[c]] = cnt[PAIR_R1[c]] + picked
    idx1 = [idx_scr[0, r] for r in range(PEER_TOPK)]
    idx2 = [idx_scr[1, r] for r in range(PEER_TOPK)]

    def per_key(k, _):
        kf = lax.convert_element_type(k, F32)
        rank2 = jnp.full((GROUPS, LANES), RANK_NONE, F32)
        ck = jnp.zeros((GROUPS, LANES), F32)
        for r in range(PEER_TOPK):
            rank2 = jnp.where(idx2[r] == kf, float(r), rank2)
            ck = jnp.where(idx1[r] == kf, cnt[r], ck)
        base = pl.multiple_of(k * GROUPS, GROUPS)
        t2_scr[0, pl.ds(base, GROUPS), :] = rank2
        t2_scr[1, pl.ds(base, GROUPS), :] = jnp.exp(so_scr[1, k] - v2[0])
        c_ref[k] = ck
        e1_ref[k] = jnp.exp(so_scr[0, k] - v1[0]) * inv_z
        return 0

    lax.fori_loop(0, PEER_KEYS, per_key, 0, unroll=4)
    for g in range(GROUPS):
        rank2 = t2_scr[0, pl.ds(g, PEER_KEYS, stride=GROUPS), :].astype(BF16)
        fac2 = t2_scr[1, pl.ds(g, PEER_KEYS, stride=GROUPS), :].astype(BF16)
        if by_head:
            r2_ref[g] = rank2
            e2_ref[g] = fac2
        else:
            r2_ref[:, g * LANES:(g + 1) * LANES] = rank2
            e2_ref[:, g * LANES:(g + 1) * LANES] = fac2


def _route_scratch(nq, tokens):
    tile = (GROUPS, LANES)
    return [pltpu.VMEM((nq, tokens), BF16),
            pltpu.VMEM((2, PEER_KEYS) + tile, F32), pltpu.VMEM((2, PEER_KEYS) + tile, F32),
            pltpu.VMEM((2, PEER_TOPK) + tile, F32), pltpu.VMEM((2, PEER_TOPK) + tile, F32),
            pltpu.VMEM((len(PAIR_FLAT),) + tile, F32),
            pltpu.VMEM((PEER_TOPK,) + tile, F32), pltpu.VMEM((PEER_TOPK,) + tile, F32),
            pltpu.VMEM((2, PEER_KEYS * GROUPS, LANES), F32)]


def _peer_route_heads(h, wq_t, kexp_h):
    t, d = h.shape
    assert t == LANES and GROUPS == PEER_HEADS
    std = jax.ShapeDtypeStruct((PEER_HEADS, PEER_KEYS, t), BF16)
    by_key = jax.ShapeDtypeStruct((PEER_KEYS, PEER_HEADS, t), F32)
    full = lambda shape: pl.BlockSpec(shape, lambda i: (0,) * len(shape))
    r2, e2, c, e1 = pl.pallas_call(
        functools.partial(_peer_route_kernel, by_head=True),
        grid=(1,),
        in_specs=[full(h.shape), full(wq_t.shape), full(kexp_h.shape)],
        out_specs=[full(std.shape), full(std.shape), full(by_key.shape), full(by_key.shape)],
        out_shape=[std, std, by_key, by_key],
        scratch_shapes=_route_scratch(wq_t.shape[0], t),
        compiler_params=_cp("arbitrary"),
        name="peer_route_heads",
    )(h, wq_t, kexp_h)
    to_groups = lambda x: jnp.transpose(x, (1, 0, 2)).reshape(PEER_HEADS, PEER_KEYS, 1, t)
    return r2, e2, to_groups(c), to_groups(e1)


def _peer_route(h, wq_t, kexp):
    t, d = h.shape
    nblk = t // ROUTE_TOKENS
    nq = wq_t.shape[0]
    std = jax.ShapeDtypeStruct((PEER_HEADS, PEER_KEYS, t), BF16)
    grp = jax.ShapeDtypeStruct((PEER_HEADS, PEER_KEYS, t // LANES, LANES), F32)
    std_spec = pl.BlockSpec((None, PEER_KEYS, ROUTE_TOKENS), lambda i, hd: (hd, 0, i))
    grp_spec = pl.BlockSpec((None, PEER_KEYS, GROUPS, LANES), lambda i, hd: (hd, 0, i, 0))
    return pl.pallas_call(
        functools.partial(_peer_route_kernel, by_head=False),
        grid=(nblk, PEER_HEADS),
        in_specs=[pl.BlockSpec((ROUTE_TOKENS, d), lambda i, hd: (i, 0)),
                  pl.BlockSpec(wq_t.shape, lambda i, hd: (0, 0)),
                  pl.BlockSpec((None,) + kexp.shape[1:], lambda i, hd: (hd, 0, 0, 0))],
        out_specs=[std_spec, std_spec, grp_spec, grp_spec],
        out_shape=[std, std, grp, grp],
        scratch_shapes=_route_scratch(nq, ROUTE_TOKENS),
        compiler_params=_cp("parallel", "arbitrary"),
        name="peer_route",
    )(h, wq_t, kexp)


def _peer_dense_kernel(h_ref, u_ref, vt_ref, r2_ref, e2_ref, c_ref, e1_ref, o_ref):
    e = pl.program_id(1)
    tb = h_ref.shape[0]
    sub = r2_ref.shape[2]

    @pl.when(e == 0)
    def _():
        o_ref[...] = jnp.zeros_like(o_ref)

    tw = min(tb, PEER_UNIT_TOKENS)
    per = PEER_SUB // PEER_KEYS
    ngrp = tw // LANES
    if c_ref.shape[2] * LANES == tb:
        goff = 0
    else:
        goff = (pl.program_id(0) % (ROUTE_TOKENS // tb)) * ngrp
    units = [(q, tp) for q in range(u_ref.shape[0] // PEER_SUB) for tp in range(tb // tw)]

    def token_rows(ref, h, jj, tp):
        row = jnp.concatenate(
            [ref[h, jj, pl.ds(goff + tp * ngrp + g, 1), :] for g in range(ngrp)], axis=1)
        return jnp.broadcast_to(row, (sub, tw)).astype(BF16)[None]

    def scores(unit):
        q, tp = unit
        return _dot_nt(u_ref[q * PEER_SUB:(q + 1) * PEER_SUB, :],
                       h_ref[tp * tw:(tp + 1) * tw, :])

    accs = [o_ref[:, tp * tw:(tp + 1) * tw] for tp in range(tb // tw)]
    at_next = scores(units[0])
    for n, (q, tp) in enumerate(units):
        at = at_next
        if n + 1 < len(units):
            at_next = scores(units[n + 1])
        cols = slice(tp * tw, (tp + 1) * tw)
        gs = []
        for jj in range(q * per, (q + 1) * per):
            w = jnp.zeros((PEER_KEYS // sub, sub, tw), BF16)
            for h in range(PEER_HEADS):
                picked = jnp.where(r2_ref[h, :, :, cols] < token_rows(c_ref, h, jj, tp),
                                   e2_ref[h, :, :, cols], jnp.zeros_like(w))
                w = w + picked * token_rows(e1_ref, h, jj, tp)
            lo = (jj - q * per) * PEER_KEYS
            a = _gelu(at[lo:lo + PEER_KEYS, :]).astype(BF16)
            gs.append(a * w.reshape(PEER_KEYS, tw))
        accs[tp] = accs[tp] + _dot(vt_ref[:, q * PEER_SUB:(q + 1) * PEER_SUB],
                                   jnp.concatenate(gs, axis=0))
    for tp, acc in enumerate(accs):
        o_ref[:, tp * tw:(tp + 1) * tw] = acc


def _peer_dense(h, u_all, v_all, layer, r2, e2, c, e1, *, tb, te):
    t, d = h.shape
    ne = u_all.shape[1]
    sub = 16
    tr = r2.shape[2]
    r2 = r2.reshape(PEER_HEADS, PEER_KEYS // sub, sub, tr)
    e2 = e2.reshape(PEER_HEADS, PEER_KEYS // sub, sub, tr)
    nchunk = te // PEER_KEYS
    per_route = ROUTE_TOKENS // tb
    full = pl.BlockSpec((PEER_HEADS, PEER_KEYS // sub, sub, tb), lambda i, e: (0, 0, 0, i))
    ngroups = min(GROUPS, c.shape[2])
    rows = pl.BlockSpec((PEER_HEADS, nchunk, ngroups, LANES),
                        lambda i, e: (0, e, i // per_route, 0))
    return pl.pallas_call(
        _peer_dense_kernel,
        grid=(t // tb, ne // te),
        in_specs=[pl.BlockSpec((tb, d), lambda i, e: (i, 0)),
                  pl.BlockSpec((None, te, d), lambda i, e: (layer, e, 0)),
                  pl.BlockSpec((None, d, te), lambda i, e: (layer, 0, e)),
                  full, full, rows, rows],
        out_specs=pl.BlockSpec((d, tb), lambda i, e: (0, i)),
        out_shape=jax.ShapeDtypeStruct((d, t), F32),
        compiler_params=_cp("parallel", "arbitrary"),
        name="peer_dense",
    )(h, u_all, v_all, r2, e2, c, e1)


def _peer_out_kernel(ot_ref, x_ref, gate_ref, xo_ref):
    xo_ref[...] = x_ref[...] + gate_ref[...] * ot_ref[...].T


def _peer_out(ot, x, gate, *, tm, per_row):
    t, d = x.shape
    row = lambda i: (i, 0)
    return pl.pallas_call(
        _peer_out_kernel,
        grid=(t // tm,),
        in_specs=[pl.BlockSpec((d, tm), lambda i: (0, i)), pl.BlockSpec((tm, d), row),
                  _mod_spec(per_row, tm, d, False)],
        out_specs=pl.BlockSpec((tm, d), row),
        out_shape=jax.ShapeDtypeStruct((t, d), F32),
        compiler_params=_cp("parallel"),
        name="peer_out",
    )(ot, x, gate)


def _peer(h, x, gate, wq_t, kexp, kexp_h, u_all, v_all, layer, *, tb, tm, per_row):
    t = h.shape[0]
    if t == LANES:
        r2, e2, c, e1 = _peer_route_heads(h, wq_t, kexp_h)
    else:
        h_route = jnp.pad(h, ((0, -t % ROUTE_TOKENS), (0, 0)))
        r2, e2, c, e1 = _peer_route(h_route, wq_t, kexp)
    ot = _peer_dense(h, u_all, v_all, layer, r2, e2, c, e1, tb=tb, te=PEER_EXPERT_BLOCK)
    return _peer_out(ot, x, gate, tm=tm, per_row=per_row)


def kernel(x_prompt, x_sample, c_prompt, c_sample, cache_pool, cache_k, cache_v, cache_logf, w_ada, b_ada, norm1, norm2, w_in_ab, b_f, w_pool, pool_scale, q_gain, k_gain, w_out_ab, w_uv, sg_ln_g, sg_ln_b, w_s, b_s, w_out_sg, peer_wq, peer_keys, peer_u, peer_v):
    nb, seq, d = x_prompt.shape
    db, dl, _ = x_sample.shape
    depth = w_ada.shape[0]
    past = cache_k.shape[2]
    pool_hist = cache_pool.shape[2]
    assert nb == 1 and db * dl == LANES and dl == 16 and pool_hist == dl - 1
    fox_w = FOX_HEADS * HEAD_DIM
    pool_w = w_pool.shape[1] * w_pool.shape[2]
    ts = db * dl

    c_rows = jnp.concatenate(
        [c_prompt, c_sample, jnp.zeros((16 - nb - db, d), F32)], axis=0)
    mods = _ada(c_rows, w_ada, b_ada)

    def mod_p(layer, k):
        return mods[layer, 0:1, k * d:(k + 1) * d]

    def mod_s(layer, k):
        return jnp.repeat(mods[layer, 1:1 + db, k * d:(k + 1) * d], dl, axis=0)

    xp = x_prompt.reshape(seq, d)
    xs = x_sample.reshape(ts, d)
    tm_p = 512 if seq % 512 == 0 else 256
    outs = {k: [] for k in ("pool_p", "pool_s", "kp", "ks", "vp", "vs", "lfp", "lfs", "sgv")}
    u_all = peer_u.astype(BF16)
    v_all = jnp.swapaxes(peer_v, 1, 2).astype(BF16)

    for layer in range(depth):
        j = layer // 2
        g1 = norm1[layer].reshape(1, d)
        g2 = norm2[layer].reshape(1, d)
        if layer % 2 == 0:
            w_in = w_in_ab[j]
            w_main = w_in[:, :pool_w + 3 * fox_w].astype(BF16)
            wf_t = w_in[:, pool_w + 3 * fox_w:].T.astype(BF16)
            bfc = b_f[j].reshape(FOX_HEADS, 1)
            qg = q_gain[j].reshape(1, fox_w)
            kg = k_gain[j].reshape(1, fox_w)
            wp = w_pool[j].astype(BF16)
            psc = pool_scale[j].reshape(1, pool_w)
            wo = w_out_ab[j].astype(BF16)

            p, qb, k, kb, v, vb, lf, fc = _proj(
                xp, g1, mod_p(layer, 0), mod_p(layer, 1), w_main, wf_t, bfc, qg, kg,
                tm=tm_p, per_row=False, do_cumsum=True)
            a_out = _pool_prompt(p, wp, psc, tm=256)
            score_bound = 1.02 * (HEAD_DIM ** 0.5) * (jnp.max(jnp.abs(q_gain[j]), axis=-1)
                                                      * jnp.max(jnp.abs(k_gain[j]), axis=-1))
            b_out = _attn_prompt(qb, kb, vb, fc, score_bound, tq=tm_p)
            xp, hp = _merge(a_out, b_out, wo, xp, mod_p(layer, 2), g2, mod_p(layer, 3),
                            mod_p(layer, 4), tm=256, per_row=False)
            outs["pool_p"].append(p[-pool_hist:].reshape(nb, pool_hist, pool_w))
            outs["kp"].append(k.reshape(nb, seq, FOX_HEADS, HEAD_DIM))
            outs["vp"].append(v.reshape(nb, seq, FOX_HEADS, HEAD_DIM))
            outs["lfp"].append(lf.T.reshape(nb, seq, FOX_HEADS))

            p, qb, k, kb, v, vb, lf, _ = _proj(
                xs, g1, mod_s(layer, 0), mod_s(layer, 1), w_main, wf_t, bfc, qg, kg,
                tm=ts, per_row=True, do_cumsum=False)
            p3 = p.reshape(db, dl, pool_w)
            buf = jnp.concatenate([jnp.zeros((db, 1, pool_w), F32), cache_pool[j], p3], axis=1)
            a_out = _pool_sample(buf.reshape(db * 2 * dl, pool_w), p, wp, psc, pos0=past)
            b_out = _attn_sample(qb, kb, vb, lf, cache_k[j], cache_v[j], cache_logf[j],
                                 score_bound, tk=min(past, SAMPLE_KEY_CHUNK))
            xs, hs = _merge(a_out, b_out, wo, xs, mod_s(layer, 2), g2, mod_s(layer, 3),
                            mod_s(layer, 4), tm=ts, per_row=True)
            outs["pool_s"].append(jnp.concatenate([cache_pool[j], p3], axis=1)[:, -pool_hist:])
            outs["ks"].append(k.reshape(db, dl, FOX_HEADS, HEAD_DIM))
            outs["vs"].append(v.reshape(db, dl, FOX_HEADS, HEAD_DIM))
            outs["lfs"].append(lf.T.reshape(db, dl, FOX_HEADS))
        else:
            wuv = w_uv[j].astype(BF16)
            lng = sg_ln_g[j].reshape(1, -1)
            lnb = sg_ln_b[j].reshape(1, -1)
            wo = w_out_sg[j].astype(BF16)
            tri = jnp.tril(jnp.ones((SG_CHUNK, SG_CHUNK), bool))
            ws_p = jnp.where(tri, w_s[j], 0.0).astype(BF16)
            bs_p = b_s[j].T
            ws16 = jnp.where(tri[:dl, :dl], w_s[j][:, :dl, :dl], 0.0)
            ws_s = jnp.einsum("ab,gts->gatbs", jnp.eye(db, dtype=F32), ws16)
            ws_s = ws_s.reshape(SG_GROUPS, ts, ts).astype(BF16)
            bs_s = jnp.tile(b_s[j][:, :dl], (1, db)).T

            u, vn = _uv(xp, g1, mod_p(layer, 0), mod_p(layer, 1), wuv, lng, lnb,
                        tm=tm_p, per_row=False, vn_dtype=BF16)
            xp, hp = _sgate(u, vn, ws_p, bs_p, wo, xp, mod_p(layer, 2), g2, mod_p(layer, 3),
                            mod_p(layer, 4), tm=256, per_row=False)
            u, vn = _uv(xs, g1, mod_s(layer, 0), mod_s(layer, 1), wuv, lng, lnb,
                        tm=ts, per_row=True, vn_dtype=F32)
            xs, hs = _sgate(u, vn, ws_s, bs_s, wo, xs, mod_s(layer, 2), g2, mod_s(layer, 3),
                            mod_s(layer, 4), tm=ts, per_row=True)
            outs["sgv"].append(vn.reshape(db, dl, -1))

        wq_t = peer_wq[layer].T.astype(BF16)
        kexp = jnp.einsum("hskd,gj->hskgjd", peer_keys[layer].astype(BF16),
                          jnp.eye(GROUPS, dtype=BF16))
        kexp = kexp.reshape(PEER_HEADS, 2, PEER_KEYS * GROUPS, GROUPS * kexp.shape[-1])
        kexp_h = jnp.einsum("hskd,hj->skhjd", peer_keys[layer].astype(BF16),
                            jnp.eye(PEER_HEADS, dtype=BF16))
        kexp_h = kexp_h.reshape(2, PEER_KEYS * PEER_HEADS, PEER_HEADS * kexp_h.shape[-1])
        xp = _peer(hp, xp, mod_p(layer, 5), wq_t, kexp, kexp_h, u_all, v_all, layer,
                   tb=PEER_TOKEN_BLOCK if seq % PEER_TOKEN_BLOCK == 0 else 256, tm=tm_p,
                   per_row=False)
        xs = _peer(hs, xs, mod_s(layer, 5), wq_t, kexp, kexp_h, u_all, v_all, layer,
                   tb=ts, tm=ts, per_row=True)

    st = lambda key: jnp.stack(outs[key])
    return (xp.reshape(nb, seq, d), xs.reshape(db, dl, d), st("pool_p"), st("pool_s"),
            st("kp"), st("ks"), st("vp"), st("vs"), st("lfp"), st("lfs"), st("sgv"))
```
